```python
import math
import jax, jax.numpy as jnp
from jax import lax
import numpy as np


D_MODEL = 1024
BATCH = 16
SEQ = 2048
DEPTH = 1

CHUNK = 64
H_A = 8
D_A = 64
H_I = 4
D_I = 64
TOPK_MAX = 256
SPARSE_Q_BLOCK = 32
H_B = 4
D_B = 64
DENSE_Q_BLOCK = 128
D_FF = ((8 * D_MODEL + 3 * 256 - 1) // (3 * 256)) * 256
RMS_EPS = 1e-6
MASK_VALUE = -1e30

W_QA = H_A * D_A
W_KA = H_A * D_A
W_VA = H_A * D_A
W_QB = H_B * 2 * D_B
W_KB = H_B * 2 * D_B
W_VB = H_B * 2 * D_B
W_QI = H_I * D_I
W_KI = D_I
W_WI = H_I
W_G = 2 * D_MODEL
_WIDTHS = [W_QA, W_KA, W_VA, W_QB, W_KB, W_VB, W_QI, W_KI, W_WI, W_G]
SPLITS = [int(v) for v in np.cumsum(_WIDTHS)[:-1]]
W_IN = int(sum(_WIDTHS))

kernel_name = 'hybrid_dsa_diffattn_gated_block'


def rms_norm(x, g):
    xf = x.astype(jnp.float32)
    y = xf * lax.rsqrt(jnp.mean(xf * xf, axis=-1, keepdims=True) + RMS_EPS)
    return (y * g.astype(jnp.float32)).astype(x.dtype)


def alibi_slopes(n):
    return jnp.asarray(2.0 ** (-8.0 * np.arange(1, n + 1) / n), dtype=jnp.float32)


def dsa_sparse_attention(q, k, v, q_idx, k_idx, w_idx):
    B, S = q.shape[0], q.shape[1]
    n_sel = min(TOPK_MAX, S // 4)
    nblk = S // SPARSE_Q_BLOCK
    slopes = alibi_slopes(H_A)
    key_chunk = jnp.arange(S) // CHUNK
    kif = k_idx.astype(jnp.float32)

    def to_blocks(a):
        return a.reshape((B, nblk, SPARSE_Q_BLOCK) + a.shape[2:]).swapaxes(0, 1)

    def block(args):
        i, qb, qib, wb = args
        t = i * SPARSE_Q_BLOCK + jnp.arange(SPARSE_Q_BLOCK)
        t_chunk = t // CHUNK
        admissible = key_chunk[None, :] <= t_chunk[:, None]
        logits = jnp.einsum('bqhd,bsd->bqhs', qib.astype(jnp.float32), kif)
        idx_score = jnp.einsum('bqh,bqhs->bqs', wb.astype(jnp.float32), jax.nn.relu(logits))
        idx_score = jnp.where(admissible[None], idx_score, -jnp.inf)
        _, sel = lax.top_k(idx_score, n_sel)
        valid = (sel // CHUNK) <= t_chunk[None, :, None]
        k_sel = jax.vmap(lambda kk, ii: kk[ii])(k, sel)
        v_sel = jax.vmap(lambda vv, ii: vv[ii])(v, sel)
        s = jnp.einsum('bqhd,bqkhd->bhqk', qb.astype(jnp.float32),
                       k_sel.astype(jnp.float32)) * (D_A ** -0.5)
        dist = jnp.abs(t[None, :, None] - sel).astype(jnp.float32)
        s = s - slopes[None, :, None, None] * dist[:, None]
        s = jnp.where(valid[:, None], s, MASK_VALUE)
        p = jax.nn.softmax(s, axis=-1)
        o = jnp.einsum('bhqk,bqkhd->bqhd', p, v_sel.astype(jnp.float32))
        return o.astype(q.dtype)

    out = lax.map(block, (jnp.arange(nblk), to_blocks(q), to_blocks(q_idx), to_blocks(w_idx)))
    return out.swapaxes(0, 1).reshape(B, S, H_A * D_A)


def differential_attention(q, k, v, lam, subln_g, lambda_init):
    B, S = q.shape[0], q.shape[1]
    nblk = S // DENSE_Q_BLOCK
    slopes = alibi_slopes(H_B)
    key_pos = jnp.arange(S)
    key_chunk = key_pos // CHUNK
    kf = k.astype(jnp.float32)
    vf = v.astype(jnp.float32)

    def block(args):
        i, qb = args
        t = i * DENSE_Q_BLOCK + jnp.arange(DENSE_Q_BLOCK)
        s = jnp.einsum('bqhmd,bshmd->bhmqs', qb.astype(jnp.float32), kf) * (D_B ** -0.5)
        dist = jnp.abs(t[:, None] - key_pos[None, :]).astype(jnp.float32)
        bias = -slopes[:, None, None, None] * dist[None, None]
        allowed = key_chunk[None, :] <= (t // CHUNK)[:, None]
        s = jnp.where(allowed, s + bias, MASK_VALUE)
        p = jax.nn.softmax(s, axis=-1)
        a = p[:, :, 0] - lam * p[:, :, 1]
        o = jnp.einsum('bhqs,bshe->bqhe', a, vf)
        o = rms_norm(o, subln_g) * (1.0 - lambda_init)
        return o.astype(q.dtype)

    qblk = q.reshape((B, nblk, DENSE_Q_BLOCK) + q.shape[2:]).swapaxes(0, 1)
    out = lax.map(block, (jnp.arange(nblk), qblk))
    return out.swapaxes(0, 1).reshape(B, S, H_B * 2 * D_B)


def setup_inputs(seed: int = 0) -> dict:
    key = jax.random.key(seed)
    ks = jax.random.split(key, 24)
    f32 = jnp.float32

    def w(k, shape, fan_in, gain=1.0):
        return jax.random.normal(k, shape, f32) * (gain * fan_in ** -0.5)

    def g(k, n):
        return 1.0 + 0.02 * jax.random.normal(k, (DEPTH, n), f32)

    return {
        'x': jax.random.normal(ks[0], (BATCH, SEQ, D_MODEL), f32),
        'c': jax.random.normal(ks[1], (BATCH, D_MODEL), f32),
        'w_ada': w(ks[2], (DEPTH, D_MODEL, 6 * D_MODEL), D_MODEL, 0.5),
        'b_ada': 0.02 * jax.random.normal(ks[3], (DEPTH, 6 * D_MODEL), f32),
        'norm1_g': g(ks[4], D_MODEL),
        'w_in': w(ks[5], (DEPTH, D_MODEL, W_IN), D_MODEL),
        'qn_a': g(ks[6], D_A),
        'kn_a': g(ks[7], D_A),
        'qn_b': g(ks[8], D_B),
        'kn_b': g(ks[9], D_B),
        'lam_q1': 0.1 * jax.random.normal(ks[10], (DEPTH, D_B), f32),
        'lam_k1': 0.1 * jax.random.normal(ks[11], (DEPTH, D_B), f32),
        'lam_q2': 0.1 * jax.random.normal(ks[12], (DEPTH, D_B), f32),
        'lam_k2': 0.1 * jax.random.normal(ks[13], (DEPTH, D_B), f32),
        'subln_g': g(ks[14], 2 * D_B),
        'w_up_a': w(ks[15], (DEPTH, H_A * D_A, D_MODEL), H_A * D_A),
        'w_up_b': w(ks[16], (DEPTH, H_B * 2 * D_B, D_MODEL), H_B * 2 * D_B),
        'w_o': w(ks[17], (DEPTH, D_MODEL, D_MODEL), D_MODEL),
        'norm2_g': g(ks[18], D_MODEL),
        'w_ff1': w(ks[19], (DEPTH, D_MODEL, D_FF), D_MODEL),
        'w_ff3': w(ks[20], (DEPTH, D_MODEL, D_FF), D_MODEL),
        'w_ff2': w(ks[21], (DEPTH, D_FF, D_MODEL), D_FF),
    }


def reference(x, c, w_ada, b_ada, norm1_g, w_in, qn_a, kn_a, qn_b, kn_b,
              lam_q1, lam_k1, lam_q2, lam_k2, subln_g, w_up_a, w_up_b, w_o,
              norm2_g, w_ff1, w_ff3, w_ff2):
    B, S = x.shape[0], x.shape[1]
    cs = jax.nn.silu(c)
    for l in range(DEPTH):
        lambda_init = 0.8 - 0.6 * math.exp(-0.3 * l)
        mod = cs @ w_ada[l] + b_ada[l]
        shift1, scale1, gate1, shift2, scale2, gate2 = jnp.split(mod, 6, axis=-1)

        h = rms_norm(x, norm1_g[l]) * (1.0 + scale1[:, None]) + shift1[:, None]
        proj = h @ w_in[l]
        qa, ka, va, qb, kb, vb, qi, ki, wi, gates = jnp.split(proj, SPLITS, axis=-1)

        qa = rms_norm(qa.reshape(B, S, H_A, D_A), qn_a[l])
        ka = rms_norm(ka.reshape(B, S, H_A, D_A), kn_a[l])
        va = va.reshape(B, S, H_A, D_A)
        y_a = dsa_sparse_attention(qa, ka, va, qi.reshape(B, S, H_I, D_I), ki, wi)

        qb = rms_norm(qb.reshape(B, S, H_B, 2, D_B), qn_b[l])
        kb = rms_norm(kb.reshape(B, S, H_B, 2, D_B), kn_b[l])
        vb = vb.reshape(B, S, H_B, 2 * D_B)
        lam = (jnp.exp(jnp.sum(lam_q1[l].astype(jnp.float32) * lam_k1[l].astype(jnp.float32)))
               - jnp.exp(jnp.sum(lam_q2[l].astype(jnp.float32) * lam_k2[l].astype(jnp.float32)))
               + lambda_init)
        y_b = differential_attention(qb, kb, vb, lam, subln_g[l], lambda_init)

        g_a, g_b = jnp.split(jax.nn.sigmoid(gates), 2, axis=-1)
        merged = g_a * (y_a @ w_up_a[l]) + g_b * (y_b @ w_up_b[l])
        x = x + gate1[:, None] * (merged @ w_o[l])

        h2 = rms_norm(x, norm2_g[l]) * (1.0 + scale2[:, None]) + shift2[:, None]
        ff = (jax.nn.silu(h2 @ w_ff1[l]) * (h2 @ w_ff3[l])) @ w_ff2[l]
        x = x + gate2[:, None] * ff
    return x
```

```python
import functools

import numpy as np
import jax
import jax.numpy as jnp
from jax import lax
from jax.experimental import pallas as pl
from jax.experimental.pallas import tpu as pltpu

F32 = jnp.float32
BF16 = jnp.bfloat16
I32 = jnp.int32

D_MODEL = 1024
CHUNK = 64
H_A, D_A = 8, 64
H_I, D_I = 4, 64
TOPK_MAX = 256
H_B, D_B = 4, 64
D_FF = 2816
RMS_EPS = 1e-6
MASK_VALUE = -1e30
LAMBDA_INIT = 0.2

W_A = H_A * D_A
W_B = H_B * 2 * D_B
W_QKV = 3 * W_A + 3 * W_B
W_IDX = H_I * D_I + D_I + H_I
W_IDX_PAD = 384
W_G = 2 * D_MODEL

TQ = 256
TM_PROJ = 256
TM_FFN = 256
CNT_ROWS = 256
VMEM_LIMIT = 56 * 1024 * 1024

NT_DIMS = (((1,), (1,)), ((), ()))


def _dot(a, b):
    return jnp.dot(a, b, preferred_element_type=F32)


def _dot_nt(a, b):
    return lax.dot_general(a, b, NT_DIMS, preferred_element_type=F32)


def _split(x):
    hi = x.astype(BF16)
    lo = (x - hi.astype(F32)).astype(BF16)
    return hi, lo


def _dot3(a, b_hi, b_lo):
    a_hi, a_lo = _split(a)
    return _dot(a_hi, b_hi) + _dot(a_hi, b_lo) + _dot(a_lo, b_hi)


def _const_spec(shape):
    nd = len(shape)
    return pl.BlockSpec(shape, lambda *_: (0,) * nd, pipeline_mode=pl.Buffered(1))


def _mod_kernel(c_ref, w_ref, b_ref, o_ref):
    c = c_ref[...]
    cs = c * jax.nn.sigmoid(c)
    w_hi, w_lo = _split(w_ref[...])
    o_ref[...] = _dot3(cs, w_hi, w_lo) + b_ref[...]


def _modulation(c, w_ada, b_ada):
    B = c.shape[0]
    n = w_ada.shape[1]
    tn = 1024
    return pl.pallas_call(
        _mod_kernel,
        grid=(n // tn,),
        in_specs=[pl.BlockSpec((B, D_MODEL), lambda j: (0, 0)),
                  pl.BlockSpec((D_MODEL, tn), lambda j: (0, j)),
                  pl.BlockSpec((1, tn), lambda j: (0, j))],
        out_specs=pl.BlockSpec((B, tn), lambda j: (0, j)),
        out_shape=jax.ShapeDtypeStruct((B, n), F32),
        compiler_params=pltpu.CompilerParams(vmem_limit_bytes=VMEM_LIMIT),
        name="adaln_mod",
    )(c, w_ada, b_ada.reshape(1, n))


def _proj_kernel(x_ref, mod_ref, g1_ref, wqkv_ref, wih_ref, wil_ref, wg_ref, grp_ref, gains_ref,
                 pqh_ref, pql_ref, pkh_ref, pkl_ref,
                 qa_ref, ka_ref, va_ref, qb_ref, kb_ref, vb_ref, qcat_ref, kcat_ref, wi_ref,
                 gates_ref):
    x = x_ref[...]
    ms = jnp.mean(x * x, axis=-1, keepdims=True)
    h = x * lax.rsqrt(ms + RMS_EPS) * g1_ref[...]
    h = h * (1.0 + mod_ref[1:2, :]) + mod_ref[0:1, :]
    h_hi, h_lo = _split(h)

    grp = grp_ref[...]

    def head_norm(z, gain):
        sq_hi, sq_lo = _split(z * z)
        m = (_dot(sq_hi, grp) + _dot(sq_lo, grp)) * (1.0 / D_A)
        return (z * lax.rsqrt(m + RMS_EPS) * gain).astype(BF16)

    qkv = _dot(h_hi, wqkv_ref[...])
    qa_ref[...] = head_norm(qkv[:, 0:W_A], gains_ref[0:1, :])
    ka_ref[...] = head_norm(qkv[:, W_A:2 * W_A], gains_ref[1:2, :])
    va_ref[...] = qkv[:, 2 * W_A:3 * W_A].astype(BF16)
    off = 3 * W_A
    qb_ref[...] = head_norm(qkv[:, off:off + W_B], gains_ref[2:3, :])
    kb_ref[...] = head_norm(qkv[:, off + W_B:off + 2 * W_B], gains_ref[3:4, :])
    vb_ref[...] = qkv[:, off + 2 * W_B:off + 3 * W_B].astype(BF16)

    pidx = (_dot(h_hi, wih_ref[...]) + _dot(h_hi, wil_ref[...]) + _dot(h_lo, wih_ref[...]))
    p_hi, p_lo = _split(pidx)
    qcat_ref[...] = (_dot(p_hi, pqh_ref[...]) + _dot(p_lo, pql_ref[...])).astype(BF16)
    kcat_ref[...] = (_dot(p_hi, pkh_ref[...]) + _dot(p_lo, pkl_ref[...])).astype(BF16)
    wi_ref[...] = pidx[:, 256:384]

    gates_ref[...] = jax.nn.sigmoid(_dot(h_hi, wg_ref[...])).astype(BF16)


def _placement_matrices():
    pqh = np.zeros((W_IDX_PAD, H_I * 256), np.float32)
    pql = np.zeros((W_IDX_PAD, H_I * 256), np.float32)
    for h in range(H_I):
        for d in range(D_I):
            pqh[h * D_I + d, h * 256 + d] = 1.0
            pql[h * D_I + d, h * 256 + 64 + d] = 1.0
            pqh[h * D_I + d, h * 256 + 128 + d] = 1.0
    pkh = np.zeros((W_IDX_PAD, 256), np.float32)
    pkl = np.zeros((W_IDX_PAD, 256), np.float32)
    for d in range(D_I):
        pkh[H_I * D_I + d, d] = 1.0
        pkh[H_I * D_I + d, 64 + d] = 1.0
        pkl[H_I * D_I + d, 128 + d] = 1.0
    return [jnp.asarray(m, BF16) for m in (pqh, pql, pkh, pkl)]


def _projection(x2, mod3, norm1_g, w_in, qn_a, kn_a, qn_b, kn_b, S):
    N = x2.shape[0]
    tm = TM_PROJ
    per_batch = S // tm
    w_qkv = w_in[:, :W_QKV].astype(BF16)
    w_idx = jnp.pad(w_in[:, W_QKV:W_QKV + W_IDX], ((0, 0), (0, W_IDX_PAD - W_IDX)))
    w_idx_hi = w_idx.astype(BF16)
    w_idx_lo = (w_idx - w_idx_hi.astype(F32)).astype(BF16)
    w_g = w_in[:, W_QKV + W_IDX:].astype(BF16)
    grp = jnp.asarray(np.kron(np.eye(W_A // D_A), np.ones((D_A, D_A))), BF16)
    gains = jnp.stack([jnp.tile(qn_a, H_A) * (D_A ** -0.5), jnp.tile(kn_a, H_A),
                       jnp.tile(qn_b, 2 * H_B) * (D_B ** -0.5), jnp.tile(kn_b, 2 * H_B)])
    pqh, pql, pkh, pkl = _placement_matrices()

    def row_spec(w):
        return pl.BlockSpec((tm, w), lambda t: (t, 0))

    out_widths = [W_A, W_A, W_A, W_B, W_B, W_B, H_I * 256, 256, 128, W_G]
    out_dtypes = [BF16] * 8 + [F32, BF16]
    return pl.pallas_call(
        _proj_kernel,
        grid=(N // tm,),
        in_specs=[row_spec(D_MODEL),
                  pl.BlockSpec((None, 6, D_MODEL), lambda t: (t // per_batch, 0, 0)),
                  _const_spec((1, D_MODEL)),
                  _const_spec((D_MODEL, W_QKV)),
                  _const_spec((D_MODEL, W_IDX_PAD)),
                  _const_spec((D_MODEL, W_IDX_PAD)),
                  _const_spec((D_MODEL, W_G)),
                  _const_spec((W_A, W_A)),
                  _const_spec((4, W_A)),
                  _const_spec((W_IDX_PAD, H_I * 256)),
                  _const_spec((W_IDX_PAD, H_I * 256)),
                  _const_spec((W_IDX_PAD, 256)),
                  _const_spec((W_IDX_PAD, 256))],
        out_specs=[row_spec(w) for w in out_widths],
        out_shape=[jax.ShapeDtypeStruct((N, w), dt) for w, dt in zip(out_widths, out_dtypes)],
        compiler_params=pltpu.CompilerParams(dimension_semantics=("parallel",),
                                             vmem_limit_bytes=VMEM_LIMIT),
        name="in_proj",
    )(x2, mod3, norm1_g.reshape(1, D_MODEL), w_qkv, w_idx_hi, w_idx_lo, w_g, grp, gains,
      pqh, pql, pkh, pkl)


def _count(ref, n_rows, pred):
    def body(j, acc):
        blk = ref[pl.ds(pl.multiple_of(j * CNT_ROWS, CNT_ROWS), CNT_ROWS), :]
        m = jnp.where(pred(blk), 1.0, 0.0)
        return acc + jnp.sum(m.reshape(CNT_ROWS // 8, 8, TQ), axis=0)
    acc = lax.fori_loop(0, n_rows // CNT_ROWS, body, jnp.zeros((8, TQ), F32))
    return jnp.sum(acc, axis=0, keepdims=True)


def _select_bias(qb, qcat_ref, kcat_ref, wt_ref, key_ref, eqi_ref, bias_ref):
    L = (qb + 1) * TQ
    kcat = kcat_ref[...]
    score = jnp.zeros((L, TQ), F32)
    for h in range(H_I):
        logits = _dot_nt(kcat, qcat_ref[:, h * 256:(h + 1) * 256])
        score = score + wt_ref[h:h + 1, :] * jnp.maximum(logits, 0.0)
    s_idx = lax.broadcasted_iota(I32, (L, TQ), 0)
    t_idx = lax.broadcasted_iota(I32, (L, TQ), 1) + qb * TQ
    admissible = (s_idx // CHUNK) <= (t_idx // CHUNK)

    if L <= TOPK_MAX:
        selected = admissible
    else:
        score = jnp.where(admissible, score, -jnp.inf)
        score = jnp.where(score == 0.0, 0.0, score)
        bits = pltpu.bitcast(score, I32)
        key_ref[0:L, :] = jnp.where(bits < 0, bits ^ jnp.int32(0x7FFFFFFF), bits)
        k_sel = float(TOPK_MAX)
        sign = jnp.int32(-2 ** 31)

        def thr_body(it, t_u):
            cand_u = t_u | lax.shift_left(jnp.int32(1), 31 - it)
            cand_s = cand_u ^ sign
            cnt = _count(key_ref, L, lambda blk: blk >= cand_s)
            return jnp.where(cnt >= k_sel, cand_u, t_u)
        t_u = lax.fori_loop(0, 32, thr_body, jnp.zeros((1, TQ), I32))
        thr = t_u ^ sign

        need = k_sel - _count(key_ref, L, lambda blk: blk > thr)
        big = jnp.int32(1 << 30)
        eqi_ref[0:L, :] = jnp.where(key_ref[0:L, :] == thr, s_idx, big)
        n_bits = max(1, int(np.ceil(np.log2(L))))

        def idx_body(it, j):
            cand = j | lax.shift_left(jnp.int32(1), n_bits - 1 - it)
            cnt = _count(eqi_ref, L, lambda blk: blk < cand)
            return jnp.where(cnt < need, cand, j)
        j_sel = lax.fori_loop(0, n_bits, idx_body, jnp.zeros((1, TQ), I32))
        selected = (key_ref[0:L, :] > thr) | (eqi_ref[0:L, :] <= j_sel)

    bias_t = jnp.where(selected, 0.0, MASK_VALUE)
    for j in range(L // TQ):
        bias_ref[:, j * TQ:(j + 1) * TQ] = bias_t[j * TQ:(j + 1) * TQ, :].T


def _dsa_kernel(qb, qcat_ref, kcat_ref, wt_ref, q_ref, k_ref, v_ref, o_ref,
                key_ref, eqi_ref, bias_ref):
    L = (qb + 1) * TQ
    pair = pl.program_id(1)

    @pl.when(pair == 0)
    def _():
        _select_bias(qb, qcat_ref, kcat_ref, wt_ref, key_ref, eqi_ref, bias_ref)

    t_idx = lax.broadcasted_iota(I32, (TQ, L), 0) + qb * TQ
    s_idx = lax.broadcasted_iota(I32, (TQ, L), 1)
    dist = jnp.abs(t_idx - s_idx).astype(F32)
    bias = bias_ref[:, 0:L]
    lane = lax.broadcasted_iota(I32, (TQ, 2 * D_A), 1)
    q2 = q_ref[...].astype(F32)
    k2 = k_ref[...]
    v2 = v_ref[...]
    outs = []
    for e in range(2):
        keep = (lane < D_A) if e == 0 else (lane >= D_A)
        qm = (q2 * jnp.where(keep, 1.0, 0.0)).astype(BF16)
        slope = jnp.exp2(-jnp.full((1, 1), 2 * pair + e + 1, I32).astype(F32))
        s = _dot_nt(qm, k2) - slope * dist + bias
        m = jnp.max(s, axis=-1, keepdims=True)
        p = jnp.exp(s - m)
        l = jnp.sum(p, axis=-1, keepdims=True)
        outs.append(_dot(p.astype(BF16), v2) / l)
    o_ref[...] = jnp.where(lane < D_A, outs[0], outs[1]).astype(BF16)


def _dsa_block(qb, qcat, kcat, wt, qa, ka, va):
    B, S = qa.shape[0], qa.shape[1]
    L = (qb + 1) * TQ
    return pl.pallas_call(
        functools.partial(_dsa_kernel, qb),
        grid=(B, H_A // 2),
        in_specs=[pl.BlockSpec((None, TQ, H_I * 256), lambda b, p: (b, qb, 0)),
                  pl.BlockSpec((None, L, 256), lambda b, p: (b, 0, 0)),
                  pl.BlockSpec((None, H_I, TQ), lambda b, p: (b, 0, qb)),
                  pl.BlockSpec((None, TQ, 2 * D_A), lambda b, p: (b, qb, p)),
                  pl.BlockSpec((None, L, 2 * D_A), lambda b, p: (b, 0, p)),
                  pl.BlockSpec((None, L, 2 * D_A), lambda b, p: (b, 0, p))],
        out_specs=pl.BlockSpec((None, TQ, 2 * D_A), lambda b, p: (b, 0, p)),
        out_shape=jax.ShapeDtypeStruct((B, TQ, W_A), BF16),
        scratch_shapes=[pltpu.VMEM((S, TQ), I32), pltpu.VMEM((S, TQ), I32),
                        pltpu.VMEM((TQ, S), F32)],
        compiler_params=pltpu.CompilerParams(dimension_semantics=("parallel", "arbitrary"),
                                             vmem_limit_bytes=VMEM_LIMIT),
        name=f"dsa_attn_q{qb}",
    )(qcat, kcat, wt, qa, ka, va)


def _diff_kernel(qb, lam_ref, sg_ref, q_ref, k_ref, v_ref, o_ref):
    L = (qb + 1) * TQ
    head = pl.program_id(1)
    lam = (jnp.exp(jnp.sum(lam_ref[0:1, :] * lam_ref[1:2, :], axis=-1, keepdims=True))
           - jnp.exp(jnp.sum(lam_ref[2:3, :] * lam_ref[3:4, :], axis=-1, keepdims=True))
           + LAMBDA_INIT)
    t_idx = lax.broadcasted_iota(I32, (TQ, L), 0) + qb * TQ
    s_idx = lax.broadcasted_iota(I32, (TQ, L), 1)
    dist = jnp.abs(t_idx - s_idx).astype(F32)
    allowed = (s_idx // CHUNK) <= (t_idx // CHUNK)
    slope = jnp.exp2(-jnp.full((1, 1), 2 * head + 2, I32).astype(F32))
    bias = jnp.where(allowed, -slope * dist, MASK_VALUE)
    lane = lax.broadcasted_iota(I32, (TQ, 2 * D_B), 1)
    q2 = q_ref[...].astype(F32)
    k2 = k_ref[...]
    probs = []
    for e in range(2):
        keep = (lane < D_B) if e == 0 else (lane >= D_B)
        qm = (q2 * jnp.where(keep, 1.0, 0.0)).astype(BF16)
        s = _dot_nt(qm, k2) + bias
        m = jnp.max(s, axis=-1, keepdims=True)
        p = jnp.exp(s - m)
        probs.append(p * (1.0 / jnp.sum(p, axis=-1, keepdims=True)))
    a = probs[0] - lam * probs[1]
    o = _dot(a.astype(BF16), v_ref[...])
    y = o * lax.rsqrt(jnp.mean(o * o, axis=-1, keepdims=True) + RMS_EPS)
    o_ref[...] = (y * sg_ref[...] * (1.0 - LAMBDA_INIT)).astype(BF16)


def _diff_block(qb, lam4, subln_g, qbn, kbn, vb):
    B, S = qbn.shape[0], qbn.shape[1]
    L = (qb + 1) * TQ
    return pl.pallas_call(
        functools.partial(_diff_kernel, qb),
        grid=(B, H_B),
        in_specs=[pl.BlockSpec((4, D_B), lambda b, h: (0, 0)),
                  pl.BlockSpec((1, 2 * D_B), lambda b, h: (0, 0)),
                  pl.BlockSpec((None, TQ, 2 * D_B), lambda b, h: (b, qb, h)),
                  pl.BlockSpec((None, L, 2 * D_B), lambda b, h: (b, 0, h)),
                  pl.BlockSpec((None, L, 2 * D_B), lambda b, h: (b, 0, h))],
        out_specs=pl.BlockSpec((None, TQ, 2 * D_B), lambda b, h: (b, 0, h)),
        out_shape=jax.ShapeDtypeStruct((B, TQ, W_B), BF16),
        compiler_params=pltpu.CompilerParams(dimension_semantics=("parallel", "parallel"),
                                             vmem_limit_bytes=VMEM_LIMIT),
        name=f"diff_attn_q{qb}",
    )(lam4, subln_g.reshape(1, 2 * D_B), qbn, kbn, vb)


def _ffn_kernel(x_ref, mod_ref, ya_ref, yb_ref, gates_ref, wua_ref, wub_ref, wo_ref, g2_ref,
                w1_ref, w3_ref, w2_ref, o_ref):
    gates = gates_ref[...].astype(F32)
    merged = (gates[:, :D_MODEL] * _dot(ya_ref[...], wua_ref[...])
              + gates[:, D_MODEL:] * _dot(yb_ref[...], wub_ref[...]))
    x1 = x_ref[...] + mod_ref[2:3, :] * _dot(merged.astype(BF16), wo_ref[...])
    ms = jnp.mean(x1 * x1, axis=-1, keepdims=True)
    h2 = x1 * lax.rsqrt(ms + RMS_EPS) * g2_ref[...]
    h2 = (h2 * (1.0 + mod_ref[4:5, :]) + mod_ref[3:4, :]).astype(BF16)
    u = _dot(h2, w1_ref[...])
    act = (u * jax.nn.sigmoid(u) * _dot(h2, w3_ref[...])).astype(BF16)
    o_ref[...] = x1 + mod_ref[5:6, :] * _dot(act, w2_ref[...])


def _merge_ffn(x2, mod3, ya, yb, gates, w_up_a, w_up_b, w_o, norm2_g, w_ff1, w_ff3, w_ff2, S):
    N = x2.shape[0]
    tm = TM_FFN
    per_batch = S // tm

    def row_spec(w):
        return pl.BlockSpec((tm, w), lambda t: (t, 0))

    return pl.pallas_call(
        _ffn_kernel,
        grid=(N // tm,),
        in_specs=[row_spec(D_MODEL),
                  pl.BlockSpec((None, 6, D_MODEL), lambda t: (t // per_batch, 0, 0)),
                  row_spec(W_A), row_spec(W_B), row_spec(W_G),
                  _const_spec((W_A, D_MODEL)), _const_spec((W_B, D_MODEL)),
                  _const_spec((D_MODEL, D_MODEL)), _const_spec((1, D_MODEL)),
                  _const_spec((D_MODEL, D_FF)), _const_spec((D_MODEL, D_FF)),
                  _const_spec((D_FF, D_MODEL))],
        out_specs=row_spec(D_MODEL),
        out_shape=jax.ShapeDtypeStruct((N, D_MODEL), F32),
        compiler_params=pltpu.CompilerParams(dimension_semantics=("parallel",),
                                             vmem_limit_bytes=VMEM_LIMIT),
        name="merge_ffn",
    )(x2, mod3, ya, yb, gates, w_up_a.astype(BF16), w_up_b.astype(BF16), w_o.astype(BF16),
      norm2_g.reshape(1, D_MODEL), w_ff1.astype(BF16), w_ff3.astype(BF16), w_ff2.astype(BF16))


def kernel(x, c, w_ada, b_ada, norm1_g, w_in, qn_a, kn_a, qn_b, kn_b, lam_q1, lam_k1, lam_q2,
           lam_k2, subln_g, w_up_a, w_up_b, w_o, norm2_g, w_ff1, w_ff3, w_ff2):
    B, S, D = x.shape
    assert D == D_MODEL and S % TQ == 0 and w_ada.shape[0] == 1
    N = B * S
    x2 = x.reshape(N, D)
    mod3 = _modulation(c, w_ada[0], b_ada[0]).reshape(B, 6, D)

    (qa, ka, va, qbn, kbn, vb, qcat, kcat, wi, gates) = _projection(
        x2, mod3, norm1_g[0], w_in[0], qn_a[0], kn_a[0], qn_b[0], kn_b[0], S)
    r3 = lambda a: a.reshape(B, S, a.shape[-1])
    qa, ka, va, qbn, kbn, vb, qcat, kcat = map(r3, (qa, ka, va, qbn, kbn, vb, qcat, kcat))
    wt = wi[:, D_I:D_I + H_I].reshape(B, S, H_I).transpose(0, 2, 1)
    lam4 = jnp.stack([lam_q1[0], lam_k1[0], lam_q2[0], lam_k2[0]])

    ya_blocks, yb_blocks = [], []
    for qb in range(S // TQ):
        ya_blocks.append(_dsa_block(qb, qcat, kcat, wt, qa, ka, va))
        yb_blocks.append(_diff_block(qb, lam4, subln_g[0], qbn, kbn, vb))
    ya = jnp.concatenate(ya_blocks, axis=1).reshape(N, W_A)
    yb = jnp.concatenate(yb_blocks, axis=1).reshape(N, W_B)

    out = _merge_ffn(x2, mod3, ya, yb, gates, w_up_a[0], w_up_b[0], w_o[0], norm2_g[0],
                     w_ff1[0], w_ff3[0], w_ff2[0], S)
    return out.reshape(B, S, D)
```

```python
import functools

import numpy as np
import jax
import jax.numpy as jnp
from jax import lax
from jax.experimental import pallas as pl
from jax.experimental.pallas import tpu as pltpu

F32 = jnp.float32
BF16 = jnp.bfloat16
I32 = jnp.int32

D_MODEL = 1024
CHUNK = 64
H_A, D_A = 8, 64
H_I, D_I = 4, 64
TOPK_MAX = 256
H_B, D_B = 4, 64
D_FF = 2816
RMS_EPS = 1e-6
MASK_VALUE = -1e30
LAMBDA_INIT = 0.2

W_A = H_A * D_A
W_B = H_B * 2 * D_B
W_QKV = 3 * W_A + 3 * W_B
W_IDX = H_I * D_I + D_I + H_I
W_IDX_PAD = 384
W_G = 2 * D_MODEL

TQ = 256
TM_PROJ = 256
TM_FFN = 256
CNT_ROWS = 256
VMEM_LIMIT = 56 * 1024 * 1024

NT_DIMS = (((1,), (1,)), ((), ()))


def _dot(a, b):
    return jnp.dot(a, b, preferred_element_type=F32)


def _dot_nt(a, b):
    return lax.dot_general(a, b, NT_DIMS, preferred_element_type=F32)


def _split(x):
    hi = x.astype(BF16)
    lo = (x - hi.astype(F32)).astype(BF16)
    return hi, lo


def _dot3(a, b_hi, b_lo):
    a_hi, a_lo = _split(a)
    return _dot(a_hi, b_hi) + _dot(a_hi, b_lo) + _dot(a_lo, b_hi)


def _const_spec(shape):
    nd = len(shape)
    return pl.BlockSpec(shape, lambda *_: (0,) * nd, pipeline_mode=pl.Buffered(1))


def _mod_kernel(c_ref, w_ref, b_ref, o_ref):
    c = c_ref[...]
    cs = c * jax.nn.sigmoid(c)
    w_hi, w_lo = _split(w_ref[...])
    o_ref[...] = _dot3(cs, w_hi, w_lo) + b_ref[...]


def _modulation(c, w_ada, b_ada):
    B = c.shape[0]
    n = w_ada.shape[1]
    tn = 1024
    return pl.pallas_call(
        _mod_kernel,
        grid=(n // tn,),
        in_specs=[pl.BlockSpec((B, D_MODEL), lambda j: (0, 0)),
                  pl.BlockSpec((D_MODEL, tn), lambda j: (0, j)),
                  pl.BlockSpec((1, tn), lambda j: (0, j))],
        out_specs=pl.BlockSpec((B, tn), lambda j: (0, j)),
        out_shape=jax.ShapeDtypeStruct((B, n), F32),
        compiler_params=pltpu.CompilerParams(vmem_limit_bytes=VMEM_LIMIT),
        name="adaln_mod",
    )(c, w_ada, b_ada.reshape(1, n))


def _proj_kernel(x_ref, mod_ref, g1_ref, wqkv_ref, wih_ref, wil_ref, wg_ref, grp_ref, gains_ref,
                 pqh_ref, pql_ref, pkh_ref, pkl_ref,
                 qa_ref, ka_ref, va_ref, qb_ref, kb_ref, vb_ref, qcat_ref, kcat_ref, wi_ref,
                 gates_ref):
    x = x_ref[...]
    ms = jnp.mean(x * x, axis=-1, keepdims=True)
    h = x * lax.rsqrt(ms + RMS_EPS) * g1_ref[...]
    h = h * (1.0 + mod_ref[1:2, :]) + mod_ref[0:1, :]
    h_hi, h_lo = _split(h)

    grp = grp_ref[...]

    def head_norm(z, gain):
        sq_hi, sq_lo = _split(z * z)
        m = (_dot(sq_hi, grp) + _dot(sq_lo, grp)) * (1.0 / D_A)
        return (z * lax.rsqrt(m + RMS_EPS) * gain).astype(BF16)

    qkv = _dot(h_hi, wqkv_ref[...])
    qa_ref[...] = head_norm(qkv[:, 0:W_A], gains_ref[0:1, :])
    ka_ref[...] = head_norm(qkv[:, W_A:2 * W_A], gains_ref[1:2, :])
    va_ref[...] = qkv[:, 2 * W_A:3 * W_A].astype(BF16)
    off = 3 * W_A
    qb_ref[...] = head_norm(qkv[:, off:off + W_B], gains_ref[2:3, :])
    kb_ref[...] = head_norm(qkv[:, off + W_B:off + 2 * W_B], gains_ref[3:4, :])
    vb_ref[...] = qkv[:, off + 2 * W_B:off + 3 * W_B].astype(BF16)

    pidx = (_dot(h_hi, wih_ref[...]) + _dot(h_hi, wil_ref[...]) + _dot(h_lo, wih_ref[...]))
    p_hi, p_lo = _split(pidx)
    qcat_ref[...] = (_dot(p_hi, pqh_ref[...]) + _dot(p_lo, pql_ref[...])).astype(BF16)
    kcat_ref[...] = (_dot(p_hi, pkh_ref[...]) + _dot(p_lo, pkl_ref[...])).astype(BF16)
    wi_ref[...] = pidx[:, 256:384]

    gates_ref[...] = jax.nn.sigmoid(_dot(h_hi, wg_ref[...])).astype(BF16)


def _placement_matrices():
    pqh = np.zeros((W_IDX_PAD, H_I * 256), np.float32)
    pql = np.zeros((W_IDX_PAD, H_I * 256), np.float32)
    for h in range(H_I):
        for d in range(D_I):
            pqh[h * D_I + d, h * 256 + d] = 1.0
            pql[h * D_I + d, h * 256 + 64 + d] = 1.0
            pqh[h * D_I + d, h * 256 + 128 + d] = 1.0
    pkh = np.zeros((W_IDX_PAD, 256), np.float32)
    pkl = np.zeros((W_IDX_PAD, 256), np.float32)
    for d in range(D_I):
        pkh[H_I * D_I + d, d] = 1.0
        pkh[H_I * D_I + d, 64 + d] = 1.0
        pkl[H_I * D_I + d, 128 + d] = 1.0
    return [jnp.asarray(m, BF16) for m in (pqh, pql, pkh, pkl)]


def _projection(x2, mod3, norm1_g, w_in, qn_a, kn_a, qn_b, kn_b, S):
    N = x2.shape[0]
    tm = TM_PROJ
    per_batch = S // tm
    w_qkv = w_in[:, :W_QKV].astype(BF16)
    w_idx = jnp.pad(w_in[:, W_QKV:W_QKV + W_IDX], ((0, 0), (0, W_IDX_PAD - W_IDX)))
    w_idx_hi = w_idx.astype(BF16)
    w_idx_lo = (w_idx - w_idx_hi.astype(F32)).astype(BF16)
    w_g = w_in[:, W_QKV + W_IDX:].astype(BF16)
    grp = jnp.asarray(np.kron(np.eye(W_A // D_A), np.ones((D_A, D_A))), BF16)
    gains = jnp.stack([jnp.tile(qn_a, H_A) * (D_A ** -0.5), jnp.tile(kn_a, H_A),
                       jnp.tile(qn_b, 2 * H_B) * (D_B ** -0.5), jnp.tile(kn_b, 2 * H_B)])
    pqh, pql, pkh, pkl = _placement_matrices()

    def row_spec(w):
        return pl.BlockSpec((tm, w), lambda t: (t, 0))

    out_widths = [W_A, W_A, W_A, W_B, W_B, W_B, H_I * 256, 256, 128, W_G]
    out_dtypes = [BF16] * 8 + [F32, BF16]
    return pl.pallas_call(
        _proj_kernel,
        grid=(N // tm,),
        in_specs=[row_spec(D_MODEL),
                  pl.BlockSpec((None, 6, D_MODEL), lambda t: (t // per_batch, 0, 0)),
                  _const_spec((1, D_MODEL)),
                  _const_spec((D_MODEL, W_QKV)),
                  _const_spec((D_MODEL, W_IDX_PAD)),
                  _const_spec((D_MODEL, W_IDX_PAD)),
                  _const_spec((D_MODEL, W_G)),
                  _const_spec((W_A, W_A)),
                  _const_spec((4, W_A)),
                  _const_spec((W_IDX_PAD, H_I * 256)),
                  _const_spec((W_IDX_PAD, H_I * 256)),
                  _const_spec((W_IDX_PAD, 256)),
                  _const_spec((W_IDX_PAD, 256))],
        out_specs=[row_spec(w) for w in out_widths],
        out_shape=[jax.ShapeDtypeStruct((N, w), dt) for w, dt in zip(out_widths, out_dtypes)],
        compiler_params=pltpu.CompilerParams(dimension_semantics=("parallel",),
                                             vmem_limit_bytes=VMEM_LIMIT),
        name="in_proj",
    )(x2, mod3, norm1_g.reshape(1, D_MODEL), w_qkv, w_idx_hi, w_idx_lo, w_g, grp, gains,
      pqh, pql, pkh, pkl)


def _count_ge(ref, n_rows, cand):
    def body(j, acc):
        blk = ref[pl.ds(pl.multiple_of(j * CNT_ROWS, CNT_ROWS), CNT_ROWS), :]
        m = jnp.where(blk >= cand, 1.0, 0.0)
        parts = [m[r * 8:(r + 1) * 8, :] for r in range(CNT_ROWS // 8)]
        while len(parts) > 1:
            parts = [a + b for a, b in zip(parts[0::2], parts[1::2])]
        return acc + parts[0]
    acc = lax.fori_loop(0, n_rows // CNT_ROWS, body, jnp.zeros((8, TQ), F32))
    return jnp.sum(acc, axis=0, keepdims=True)


def _rank_to_f32(rank):
    key = rank ^ jnp.int32(-2 ** 31)
    return pltpu.bitcast(jnp.where(key < 0, key ^ jnp.int32(0x7FFFFFFF), key), F32)


def _store_bias(bias_ref, bias_t, L):
    for j in range(L // TQ):
        bias_ref[:, j * TQ:(j + 1) * TQ] = bias_t[j * TQ:(j + 1) * TQ, :].T


def _select_bias(qb, qcat_ref, kcat_ref, wt_ref, score_ref, eqi_ref, bias_ref):
    L = (qb + 1) * TQ
    kcat = kcat_ref[...]
    score = jnp.zeros((L, TQ), F32)
    for h in range(H_I):
        logits = _dot_nt(kcat, qcat_ref[:, h * 256:(h + 1) * 256])
        score = score + wt_ref[h:h + 1, :] * jnp.maximum(logits, 0.0)
    s_idx = lax.broadcasted_iota(I32, (L, TQ), 0)
    t_idx = lax.broadcasted_iota(I32, (L, TQ), 1) + qb * TQ
    admissible = (s_idx // CHUNK) <= (t_idx // CHUNK)

    if L <= TOPK_MAX:
        _store_bias(bias_ref, jnp.where(admissible, 0.0, MASK_VALUE), L)
        return

    score_ref[0:L, :] = jnp.where(admissible, score, -jnp.inf)
    k_sel = float(TOPK_MAX)

    def thr_cond(st):
        it, _, cnt = st
        return jnp.logical_and(it < 16, jnp.max(cnt) > k_sel)

    def thr_body(st):
        it, rank, cnt = st
        for half in range(2):
            cand = rank | lax.shift_left(jnp.int32(1), 31 - (2 * it + half))
            c = _count_ge(score_ref, L, _rank_to_f32(cand))
            take = c >= k_sel
            rank = jnp.where(take, cand, rank)
            cnt = jnp.where(take, c, cnt)
        return it + 1, rank, cnt

    _, rank, cnt = lax.while_loop(
        thr_cond, thr_body,
        (jnp.int32(0), jnp.zeros((1, TQ), I32), jnp.full((1, TQ), float(L), F32)))
    thr = _rank_to_f32(rank)
    tied = jnp.max(cnt) > k_sel

    @pl.when(jnp.logical_not(tied))
    def _():
        _store_bias(bias_ref, jnp.where(score_ref[0:L, :] >= thr, 0.0, MASK_VALUE), L)

    @pl.when(tied)
    def _():
        above = _rank_to_f32(rank + 1)
        need = k_sel - _count_ge(score_ref, L, above)
        big = jnp.int32(1 << 30)
        sc = score_ref[0:L, :]
        eqi_ref[0:L, :] = jnp.where(sc >= above, big, jnp.where(sc >= thr, s_idx, big))
        n_bits = max(1, int(np.ceil(np.log2(L))))

        def idx_body(it, j):
            cand = j | lax.shift_left(jnp.int32(1), n_bits - 1 - it)
            cnt_lt = float(L) - _count_ge(eqi_ref, L, cand)
            return jnp.where(cnt_lt < need, cand, j)
        j_sel = lax.fori_loop(0, n_bits, idx_body, jnp.zeros((1, TQ), I32))
        sel = jnp.where(score_ref[0:L, :] >= above, 0.0,
                        jnp.where(eqi_ref[0:L, :] <= j_sel, 0.0, MASK_VALUE))
        _store_bias(bias_ref, sel, L)


def _dsa_kernel(qb, qcat_ref, kcat_ref, wt_ref, q_ref, k_ref, v_ref, o_ref,
                key_ref, eqi_ref, bias_ref):
    L = (qb + 1) * TQ
    pair = pl.program_id(1)

    @pl.when(pair == 0)
    def _():
        _select_bias(qb, qcat_ref, kcat_ref, wt_ref, key_ref, eqi_ref, bias_ref)

    t_idx = lax.broadcasted_iota(I32, (TQ, L), 0) + qb * TQ
    s_idx = lax.broadcasted_iota(I32, (TQ, L), 1)
    dist = jnp.abs(t_idx - s_idx).astype(F32)
    bias = bias_ref[:, 0:L]
    lane = lax.broadcasted_iota(I32, (TQ, 2 * D_A), 1)
    q2 = q_ref[...].astype(F32)
    k2 = k_ref[...]
    v2 = v_ref[...]
    outs = []
    for e in range(2):
        keep = (lane < D_A) if e == 0 else (lane >= D_A)
        qm = (q2 * jnp.where(keep, 1.0, 0.0)).astype(BF16)
        slope = jnp.exp2(-jnp.full((1, 1), 2 * pair + e + 1, I32).astype(F32))
        s = _dot_nt(qm, k2) - slope * dist + bias
        m = jnp.max(s, axis=-1, keepdims=True)
        p = jnp.exp(s - m)
        l = jnp.sum(p, axis=-1, keepdims=True)
        outs.append(_dot(p.astype(BF16), v2) / l)
    o_ref[...] = jnp.where(lane < D_A, outs[0], outs[1]).astype(BF16)


def _dsa_block(qb, qcat, kcat, wt, qa, ka, va):
    B, S = qa.shape[0], qa.shape[1]
    L = (qb + 1) * TQ
    return pl.pallas_call(
        functools.partial(_dsa_kernel, qb),
        grid=(B, H_A // 2),
        in_specs=[pl.BlockSpec((None, TQ, H_I * 256), lambda b, p: (b, qb, 0)),
                  pl.BlockSpec((None, L, 256), lambda b, p: (b, 0, 0)),
                  pl.BlockSpec((None, H_I, TQ), lambda b, p: (b, 0, qb)),
                  pl.BlockSpec((None, TQ, 2 * D_A), lambda b, p: (b, qb, p)),
                  pl.BlockSpec((None, L, 2 * D_A), lambda b, p: (b, 0, p)),
                  pl.BlockSpec((None, L, 2 * D_A), lambda b, p: (b, 0, p))],
        out_specs=pl.BlockSpec((None, TQ, 2 * D_A), lambda b, p: (b, 0, p)),
        out_shape=jax.ShapeDtypeStruct((B, TQ, W_A), BF16),
        scratch_shapes=[pltpu.VMEM((S, TQ), F32), pltpu.VMEM((S, TQ), I32),
                        pltpu.VMEM((TQ, S), F32)],
        compiler_params=pltpu.CompilerParams(dimension_semantics=("parallel", "arbitrary"),
                                             vmem_limit_bytes=VMEM_LIMIT),
        name=f"dsa_attn_q{qb}",
    )(qcat, kcat, wt, qa, ka, va)


def _diff_kernel(qb, lam_ref, sg_ref, q_ref, k_ref, v_ref, o_ref):
    L = (qb + 1) * TQ
    head = pl.program_id(1)
    lam = (jnp.exp(jnp.sum(lam_ref[0:1, :] * lam_ref[1:2, :], axis=-1, keepdims=True))
           - jnp.exp(jnp.sum(lam_ref[2:3, :] * lam_ref[3:4, :], axis=-1, keepdims=True))
           + LAMBDA_INIT)
    t_idx = lax.broadcasted_iota(I32, (TQ, L), 0) + qb * TQ
    s_idx = lax.broadcasted_iota(I32, (TQ, L), 1)
    dist = jnp.abs(t_idx - s_idx).astype(F32)
    allowed = (s_idx // CHUNK) <= (t_idx // CHUNK)
    slope = jnp.exp2(-jnp.full((1, 1), 2 * head + 2, I32).astype(F32))
    bias = jnp.where(allowed, -slope * dist, MASK_VALUE)
    lane = lax.broadcasted_iota(I32, (TQ, 2 * D_B), 1)
    q2 = q_ref[...].astype(F32)
    k2 = k_ref[...]
    probs = []
    for e in range(2):
        keep = (lane < D_B) if e == 0 else (lane >= D_B)
        qm = (q2 * jnp.where(keep, 1.0, 0.0)).astype(BF16)
        s = _dot_nt(qm, k2) + bias
        m = jnp.max(s, axis=-1, keepdims=True)
        p = jnp.exp(s - m)
        probs.append(p * (1.0 / jnp.sum(p, axis=-1, keepdims=True)))
    a = probs[0] - lam * probs[1]
    o = _dot(a.astype(BF16), v_ref[...])
    y = o * lax.rsqrt(jnp.mean(o * o, axis=-1, keepdims=True) + RMS_EPS)
    o_ref[...] = (y * sg_ref[...] * (1.0 - LAMBDA_INIT)).astype(BF16)


def _diff_block(qb, lam4, subln_g, qbn, kbn, vb):
    B, S = qbn.shape[0], qbn.shape[1]
    L = (qb + 1) * TQ
    return pl.pallas_call(
        functools.partial(_diff_kernel, qb),
        grid=(B, H_B),
        in_specs=[pl.BlockSpec((4, D_B), lambda b, h: (0, 0)),
                  pl.BlockSpec((1, 2 * D_B), lambda b, h: (0, 0)),
                  pl.BlockSpec((None, TQ, 2 * D_B), lambda b, h: (b, qb, h)),
                  pl.BlockSpec((None, L, 2 * D_B), lambda b, h: (b, 0, h)),
                  pl.BlockSpec((None, L, 2 * D_B), lambda b, h: (b, 0, h))],
        out_specs=pl.BlockSpec((None, TQ, 2 * D_B), lambda b, h: (b, 0, h)),
        out_shape=jax.ShapeDtypeStruct((B, TQ, W_B), BF16),
        compiler_params=pltpu.CompilerParams(dimension_semantics=("parallel", "parallel"),
                                             vmem_limit_bytes=VMEM_LIMIT),
        name=f"diff_attn_q{qb}",
    )(lam4, subln_g.reshape(1, 2 * D_B), qbn, kbn, vb)


def _ffn_kernel(x_ref, mod_ref, ya_ref, yb_ref, gates_ref, wua_ref, wub_ref, wo_ref, g2_ref,
                w1_ref, w3_ref, w2_ref, o_ref):
    gates = gates_ref[...].astype(F32)
    merged = (gates[:, :D_MODEL] * _dot(ya_ref[...], wua_ref[...])
              + gates[:, D_MODEL:] * _dot(yb_ref[...], wub_ref[...]))
    x1 = x_ref[...] + mod_ref[2:3, :] * _dot(merged.astype(BF16), wo_ref[...])
    ms = jnp.mean(x1 * x1, axis=-1, keepdims=True)
    h2 = x1 * lax.rsqrt(ms + RMS_EPS) * g2_ref[...]
    h2 = (h2 * (1.0 + mod_ref[4:5, :]) + mod_ref[3:4, :]).astype(BF16)
    u = _dot(h2, w1_ref[...])
    act = (u * jax.nn.sigmoid(u) * _dot(h2, w3_ref[...])).astype(BF16)
    o_ref[...] = x1 + mod_ref[5:6, :] * _dot(act, w2_ref[...])


def _merge_ffn(x2, mod3, ya, yb, gates, w_up_a, w_up_b, w_o, norm2_g, w_ff1, w_ff3, w_ff2, S):
    N = x2.shape[0]
    tm = TM_FFN
    per_batch = S // tm

    def row_spec(w):
        return pl.BlockSpec((tm, w), lambda t: (t, 0))

    return pl.pallas_call(
        _ffn_kernel,
        grid=(N // tm,),
        in_specs=[row_spec(D_MODEL),
                  pl.BlockSpec((None, 6, D_MODEL), lambda t: (t // per_batch, 0, 0)),
                  row_spec(W_A), row_spec(W_B), row_spec(W_G),
                  _const_spec((W_A, D_MODEL)), _const_spec((W_B, D_MODEL)),
                  _const_spec((D_MODEL, D_MODEL)), _const_spec((1, D_MODEL)),
                  _const_spec((D_MODEL, D_FF)), _const_spec((D_MODEL, D_FF)),
                  _const_spec((D_FF, D_MODEL))],
        out_specs=row_spec(D_MODEL),
        out_shape=jax.ShapeDtypeStruct((N, D_MODEL), F32),
        compiler_params=pltpu.CompilerParams(dimension_semantics=("parallel",),
                                             vmem_limit_bytes=VMEM_LIMIT),
        name="merge_ffn",
    )(x2, mod3, ya, yb, gates, w_up_a.astype(BF16), w_up_b.astype(BF16), w_o.astype(BF16),
      norm2_g.reshape(1, D_MODEL), w_ff1.astype(BF16), w_ff3.astype(BF16), w_ff2.astype(BF16))


def kernel(x, c, w_ada, b_ada, norm1_g, w_in, qn_a, kn_a, qn_b, kn_b, lam_q1, lam_k1, lam_q2,
           lam_k2, subln_g, w_up_a, w_up_b, w_o, norm2_g, w_ff1, w_ff3, w_ff2):
    B, S, D = x.shape
    assert D == D_MODEL and S % TQ == 0 and w_ada.shape[0] == 1
    N = B * S
    x2 = x.reshape(N, D)
    mod3 = _modulation(c, w_ada[0], b_ada[0]).reshape(B, 6, D)

    (qa, ka, va, qbn, kbn, vb, qcat, kcat, wi, gates) = _projection(
        x2, mod3, norm1_g[0], w_in[0], qn_a[0], kn_a[0], qn_b[0], kn_b[0], S)
    r3 = lambda a: a.reshape(B, S, a.shape[-1])
    qa, ka, va, qbn, kbn, vb, qcat, kcat = map(r3, (qa, ka, va, qbn, kbn, vb, qcat, kcat))
    wt = wi[:, D_I:D_I + H_I].reshape(B, S, H_I).transpose(0, 2, 1)
    lam4 = jnp.stack([lam_q1[0], lam_k1[0], lam_q2[0], lam_k2[0]])

    ya_blocks, yb_blocks = [], []
    for qb in range(S // TQ):
        ya_blocks.append(_dsa_block(qb, qcat, kcat, wt, qa, ka, va))
        yb_blocks.append(_diff_block(qb, lam4, subln_g[0], qbn, kbn, vb))
    ya = jnp.concatenate(ya_blocks, axis=1).reshape(N, W_A)
    yb = jnp.concatenate(yb_blocks, axis=1).reshape(N, W_B)

    out = _merge_ffn(x2, mod3, ya, yb, gates, w_up_a[0], w_up_b[0], w_o[0], norm2_g[0],
                     w_ff1[0], w_ff3[0], w_ff2[0], S)
    return out.reshape(B, S, D)
```

```python
import functools

import numpy as np
import jax
import jax.numpy as jnp
from jax import lax
from jax.experimental import pallas as pl
from jax.experimental.pallas import tpu as pltpu

F32 = jnp.float32
BF16 = jnp.bfloat16
I32 = jnp.int32

D_MODEL = 1024
CHUNK = 64
H_A, D_A = 8, 64
H_I, D_I = 4, 64
TOPK_MAX = 256
H_B, D_B = 4, 64
D_FF = 2816
RMS_EPS = 1e-6
MASK_VALUE = -1e30
LAMBDA_INIT = 0.2
LOG2E = 1.4426950408889634

W_A = H_A * D_A
W_B = H_B * 2 * D_B
W_QKV = 3 * W_A + 3 * W_B
W_IDX = H_I * D_I + D_I + H_I
W_IDX_PAD = 384
W_G = 2 * D_MODEL

TQ = 256
KT = TQ
TM_PROJ = KT
TM_FFN = 256
CNT_ROWS = 256
VMEM_LIMIT = 56 * 1024 * 1024

NT_DIMS = (((1,), (1,)), ((), ()))


def _dot(a, b):
    return jnp.dot(a, b, preferred_element_type=F32)


def _dot_nt(a, b):
    return lax.dot_general(a, b, NT_DIMS, preferred_element_type=F32)


def _split(x):
    hi = x.astype(BF16)
    lo = (x - hi.astype(F32)).astype(BF16)
    return hi, lo


def _dot3(a, b_hi, b_lo):
    a_hi, a_lo = _split(a)
    return _dot(a_hi, b_hi) + _dot(a_hi, b_lo) + _dot(a_lo, b_hi)


def _const_spec(shape):
    nd = len(shape)
    return pl.BlockSpec(shape, lambda *_: (0,) * nd, pipeline_mode=pl.Buffered(1))


def _mod_kernel(c_ref, w_ref, b_ref, o_ref):
    c = c_ref[...]
    cs = c * jax.nn.sigmoid(c)
    w_hi, w_lo = _split(w_ref[...])
    o_ref[...] = _dot3(cs, w_hi, w_lo) + b_ref[...]


def _modulation(c, w_ada, b_ada):
    B = c.shape[0]
    n = w_ada.shape[1]
    tn = 1024
    return pl.pallas_call(
        _mod_kernel,
        grid=(n // tn,),
        in_specs=[pl.BlockSpec((B, D_MODEL), lambda j: (0, 0)),
                  pl.BlockSpec((D_MODEL, tn), lambda j: (0, j)),
                  pl.BlockSpec((1, tn), lambda j: (0, j))],
        out_specs=pl.BlockSpec((B, tn), lambda j: (0, j)),
        out_shape=jax.ShapeDtypeStruct((B, n), F32),
        compiler_params=pltpu.CompilerParams(vmem_limit_bytes=VMEM_LIMIT),
        name="adaln_mod",
    )(c, w_ada, b_ada.reshape(1, n))


def _proj_kernel(x_ref, mod_ref, g1_ref, wqk_ref, wvt_ref, wih_ref, wil_ref, wg_ref, grp_ref,
                 gains_ref,
                 qa_ref, ka_ref, qb_ref, kb_ref, vat_ref, vbt_ref, qcat_ref, kcat_ref, wi_ref,
                 gates_ref):
    x = x_ref[...]
    ms = jnp.mean(x * x, axis=-1, keepdims=True)
    h = x * lax.rsqrt(ms + RMS_EPS) * g1_ref[...]
    h = h * (1.0 + mod_ref[1:2, :]) + mod_ref[0:1, :]
    h_hi, h_lo = _split(h)

    grp = grp_ref[...]

    def head_norm(z, gain):
        m = _dot((z * z).astype(BF16), grp) * (1.0 / D_A)
        return (z * lax.rsqrt(m + RMS_EPS) * gain).astype(BF16)

    qk = _dot(h_hi, wqk_ref[...])
    qa_ref[...] = head_norm(qk[:, 0:W_A], gains_ref[0:1, :])
    ka_ref[...] = head_norm(qk[:, W_A:2 * W_A], gains_ref[1:2, :])
    qb_ref[...] = head_norm(qk[:, 2 * W_A:2 * W_A + W_B], gains_ref[2:3, :])
    kb_ref[...] = head_norm(qk[:, 2 * W_A + W_B:], gains_ref[3:4, :])
    vt = _dot_nt(wvt_ref[...], h_hi)
    vat_ref[...] = vt[0:W_A, :].astype(BF16)
    vbt_ref[...] = vt[W_A:, :].astype(BF16)

    pidx = (_dot(h_hi, wih_ref[...]) + _dot(h_hi, wil_ref[...]) + _dot(h_lo, wih_ref[...]))
    low_half = lax.broadcasted_iota(I32, (x.shape[0], 128), 1) < D_I
    for g in range(H_I // 2):
        pair = pidx[:, g * 128:(g + 1) * 128]
        hi = pair.astype(BF16).astype(F32)
        lo = pair - hi
        hi_sw = pltpu.roll(hi, D_I, 1)
        lo_sw = pltpu.roll(lo, D_I, 1)
        base = 2 * g * 256
        qcat_ref[:, base:base + 128] = jnp.where(low_half, hi, lo_sw).astype(BF16)
        qcat_ref[:, base + 128:base + 256] = jnp.where(low_half, hi, 0.0).astype(BF16)
        qcat_ref[:, base + 256:base + 384] = jnp.where(low_half, hi_sw, lo).astype(BF16)
        qcat_ref[:, base + 384:base + 512] = jnp.where(low_half, hi_sw, 0.0).astype(BF16)
    tail = pidx[:, 256:384]
    hi = tail.astype(BF16).astype(F32)
    lo = tail - hi
    kcat_ref[:, 0:128] = jnp.where(low_half, hi, pltpu.roll(hi, D_I, 1)).astype(BF16)
    kcat_ref[:, 128:256] = jnp.where(low_half, lo, 0.0).astype(BF16)
    wi_ref[...] = tail

    gates_ref[...] = jax.nn.sigmoid(_dot(h_hi, wg_ref[...])).astype(BF16)


def _projection(x2, mod3, norm1_g, w_in, qn_a, kn_a, qn_b, kn_b, S):
    N = x2.shape[0]
    tm = TM_PROJ
    per_batch = S // tm
    w_qk = jnp.concatenate([w_in[:, 0:2 * W_A], w_in[:, 3 * W_A:3 * W_A + 2 * W_B]],
                           axis=1).astype(BF16)
    w_vt = jnp.concatenate([w_in[:, 2 * W_A:3 * W_A], w_in[:, 3 * W_A + 2 * W_B:W_QKV]],
                           axis=1).T.astype(BF16)
    w_idx = jnp.pad(w_in[:, W_QKV:W_QKV + W_IDX], ((0, 0), (0, W_IDX_PAD - W_IDX)))
    w_idx_hi = w_idx.astype(BF16)
    w_idx_lo = (w_idx - w_idx_hi.astype(F32)).astype(BF16)
    w_g = w_in[:, W_QKV + W_IDX:].astype(BF16)
    grp = jnp.asarray(np.kron(np.eye(W_A // D_A), np.ones((D_A, D_A))), BF16)
    gains = jnp.stack([jnp.tile(qn_a, H_A) * (LOG2E * D_A ** -0.5), jnp.tile(kn_a, H_A),
                       jnp.tile(qn_b, 2 * H_B) * (LOG2E * D_B ** -0.5), jnp.tile(kn_b, 2 * H_B)])

    def row_spec(w):
        return pl.BlockSpec((tm, w), lambda t: (t, 0))

    def rows(w, dt):
        return row_spec(w), jax.ShapeDtypeStruct((N, w), dt)

    def transposed(w):
        return (pl.BlockSpec((None, w, tm), lambda t: (t, 0, 0)),
                jax.ShapeDtypeStruct((N // tm, w, tm), BF16))

    outs = [rows(W_A, BF16), rows(W_A, BF16), rows(W_B, BF16), rows(W_B, BF16),
            transposed(W_A), transposed(W_B),
            rows(H_I * 256, BF16), rows(256, BF16), rows(128, F32), rows(W_G, BF16)]
    return pl.pallas_call(
        _proj_kernel,
        grid=(N // tm,),
        in_specs=[row_spec(D_MODEL),
                  pl.BlockSpec((None, 6, D_MODEL), lambda t: (t // per_batch, 0, 0)),
                  _const_spec((1, D_MODEL)),
                  _const_spec((D_MODEL, 2 * W_A + 2 * W_B)),
                  _const_spec((W_A + W_B, D_MODEL)),
                  _const_spec((D_MODEL, W_IDX_PAD)),
                  _const_spec((D_MODEL, W_IDX_PAD)),
                  _const_spec((D_MODEL, W_G)),
                  _const_spec((W_A, W_A)),
                  _const_spec((4, W_A))],
        out_specs=[o[0] for o in outs],
        out_shape=[o[1] for o in outs],
        compiler_params=pltpu.CompilerParams(dimension_semantics=("parallel",),
                                             vmem_limit_bytes=VMEM_LIMIT),
        name="in_proj",
    )(x2, mod3, norm1_g.reshape(1, D_MODEL), w_qk, w_vt, w_idx_hi, w_idx_lo, w_g, grp, gains)


def _count_ge(ref, n_rows, cand):
    def body(j, acc):
        blk = ref[pl.ds(pl.multiple_of(j * CNT_ROWS, CNT_ROWS), CNT_ROWS), :]
        m = jnp.where(blk >= cand, 1.0, 0.0)
        parts = [m[r * 8:(r + 1) * 8, :] for r in range(CNT_ROWS // 8)]
        while len(parts) > 1:
            parts = [a + b for a, b in zip(parts[0::2], parts[1::2])]
        return acc + parts[0]
    acc = lax.fori_loop(0, n_rows // CNT_ROWS, body, jnp.zeros((8, TQ), F32), unroll=2)
    return jnp.sum(acc, axis=0, keepdims=True)


def _rank_to_f32(rank):
    key = rank ^ jnp.int32(-2 ** 31)
    return pltpu.bitcast(jnp.where(key < 0, key ^ jnp.int32(0x7FFFFFFF), key), F32)


def _store_bias(bias_ref, bias_t, L):
    bias_ref[0:L, :] = bias_t


def _select_bias(qb, qcat_ref, kcat_ref, wt_ref, score_ref, eqi_ref, bias_ref):
    L = (qb + 1) * TQ
    kcat = kcat_ref[...]
    score = jnp.zeros((L, TQ), F32)
    for h in range(H_I):
        logits = _dot_nt(kcat, qcat_ref[:, h * 256:(h + 1) * 256])
        score = score + wt_ref[h:h + 1, :] * jnp.maximum(logits, 0.0)
    s_idx = lax.broadcasted_iota(I32, (L, TQ), 0)
    t_idx = lax.broadcasted_iota(I32, (L, TQ), 1) + qb * TQ
    admissible = (s_idx // CHUNK) <= (t_idx // CHUNK)

    if L <= TOPK_MAX:
        _store_bias(bias_ref, jnp.where(admissible, 0.0, MASK_VALUE), L)
        return

    score_ref[0:L, :] = jnp.where(admissible, score, -jnp.inf)
    k_sel = float(TOPK_MAX)

    def thr_body(it, st):
        rank, cnt = st
        cand = rank | lax.shift_left(jnp.int32(1), 31 - it)
        c = _count_ge(score_ref, L, _rank_to_f32(cand))
        take = c >= k_sel
        return jnp.where(take, cand, rank), jnp.where(take, c, cnt)

    rank, cnt = lax.fori_loop(
        0, 32, thr_body, (jnp.zeros((1, TQ), I32), jnp.full((1, TQ), float(L), F32)))
    thr = _rank_to_f32(rank)
    tied = jnp.max(cnt) > k_sel

    @pl.when(jnp.logical_not(tied))
    def _():
        _store_bias(bias_ref, jnp.where(score_ref[0:L, :] >= thr, 0.0, MASK_VALUE), L)

    @pl.when(tied)
    def _():
        above = _rank_to_f32(rank + 1)
        need = k_sel - _count_ge(score_ref, L, above)
        big = jnp.int32(1 << 30)
        sc = score_ref[0:L, :]
        eqi_ref[0:L, :] = jnp.where(sc >= above, big, jnp.where(sc >= thr, s_idx, big))
        n_bits = max(1, int(np.ceil(np.log2(L))))

        def idx_body(it, j):
            cand = j | lax.shift_left(jnp.int32(1), n_bits - 1 - it)
            cnt_lt = float(L) - _count_ge(eqi_ref, L, cand)
            return jnp.where(cnt_lt < need, cand, j)
        j_sel = lax.fori_loop(0, n_bits, idx_body, jnp.zeros((1, TQ), I32))
        sel = jnp.where(score_ref[0:L, :] >= above, 0.0,
                        jnp.where(eqi_ref[0:L, :] <= j_sel, 0.0, MASK_VALUE))
        _store_bias(bias_ref, sel, L)


def _tile_offsets():
    return (lax.broadcasted_iota(I32, (KT, TQ), 1)
            - lax.broadcasted_iota(I32, (KT, TQ), 0)).astype(F32)


def _masked_halves(q_ref, n_groups):
    lane = lax.broadcasted_iota(I32, (TQ, 128), 1)
    low = jnp.where(lane < 64, 1.0, 0.0)
    high = 1.0 - low
    out = []
    for g in range(n_groups):
        q2 = q_ref[:, g * 128:(g + 1) * 128].astype(F32)
        out += [(q2 * low).astype(BF16), (q2 * high).astype(BF16)]
    return out


def _online_steps(scores, values, carry):
    ss = [f() for f in scores]
    mid = []
    for s, (m, l, _) in zip(ss, carry):
        m_new = jnp.maximum(m, jnp.max(s, axis=0, keepdims=True))
        alpha = jnp.exp2(m - m_new)
        p = jnp.exp2(s - m_new)
        mid.append((m_new, alpha * l + jnp.sum(p, axis=0, keepdims=True), alpha, p.astype(BF16)))
    return tuple((m, l, alpha * acc + _dot(v(), p))
                 for (m, l, alpha, p), v, (_, _, acc) in zip(mid, values, carry))


def _softmax_init(rows):
    return (jnp.full((1, TQ), -jnp.inf, F32), jnp.zeros((1, TQ), F32), jnp.zeros((rows, TQ), F32))


def _key_rows(j):
    return pl.ds(pl.multiple_of(j * KT, KT), KT)


def _dsa_kernel(qb, qcat_ref, kcat_ref, wt_ref, q_ref, k_ref, vt_ref, o_ref,
                score_ref, eqi_ref, bias_ref):
    _select_bias(qb, qcat_ref, kcat_ref, wt_ref, score_ref, eqi_ref, bias_ref)

    qms = _masked_halves(q_ref, H_A // 2)
    offs = _tile_offsets()
    slopes = [LOG2E * 2.0 ** -(h + 1) for h in range(H_A)]

    def heads(j, k_rows, bias_t, dist, carry):
        def score(h):
            k_t = k_ref[k_rows, (h // 2) * 128:(h // 2 + 1) * 128]
            return lambda: _dot_nt(k_t, qms[h]) + (bias_t - slopes[h] * dist)

        def value(h):
            return lambda: vt_ref[j, h * D_A:(h + 1) * D_A, :]
        return _online_steps([score(h) for h in range(H_A)], [value(h) for h in range(H_A)],
                             carry)

    def off_diagonal(j, carry):
        dist = offs + ((qb - j) * KT).astype(F32)
        return heads(j, _key_rows(j), bias_ref[_key_rows(j), :], dist, carry)

    carry = lax.fori_loop(0, qb, off_diagonal, tuple(_softmax_init(D_A) for _ in range(H_A)))
    diag = slice(qb * KT, (qb + 1) * KT)
    carry = heads(qb, diag, bias_ref[diag, :], jnp.abs(offs), carry)
    out_t = jnp.concatenate([acc / l for _, l, acc in carry], axis=0)
    o_ref[...] = out_t.T.astype(BF16)


def _dsa_block(qb, qcat, kcat, wt, qa, ka, vat):
    B, S = qa.shape[0], qa.shape[1]
    L = (qb + 1) * TQ
    return pl.pallas_call(
        functools.partial(_dsa_kernel, qb),
        grid=(B,),
        in_specs=[pl.BlockSpec((None, TQ, H_I * 256), lambda b: (b, qb, 0)),
                  pl.BlockSpec((None, L, 256), lambda b: (b, 0, 0)),
                  pl.BlockSpec((None, H_I, TQ), lambda b: (b, 0, qb)),
                  pl.BlockSpec((None, TQ, W_A), lambda b: (b, qb, 0)),
                  pl.BlockSpec((None, L, W_A), lambda b: (b, 0, 0)),
                  pl.BlockSpec((S // KT, W_A, KT), lambda b: (b, 0, 0))],
        out_specs=pl.BlockSpec((None, TQ, W_A), lambda b: (b, 0, 0)),
        out_shape=jax.ShapeDtypeStruct((B, TQ, W_A), BF16),
        scratch_shapes=[pltpu.VMEM((S, TQ), F32), pltpu.VMEM((S, TQ), I32),
                        pltpu.VMEM((S, TQ), F32)],
        compiler_params=pltpu.CompilerParams(dimension_semantics=("parallel",),
                                             vmem_limit_bytes=VMEM_LIMIT),
        name=f"dsa_attn_q{qb}",
    )(qcat, kcat, wt, qa, ka, vat)


def _diff_kernel(qb, lam_ref, sg_ref, q_ref, k_ref, vt_ref, o_ref):
    lam = (jnp.exp(jnp.sum(lam_ref[0:1, :] * lam_ref[1:2, :], axis=-1, keepdims=True))
           - jnp.exp(jnp.sum(lam_ref[2:3, :] * lam_ref[3:4, :], axis=-1, keepdims=True))
           + LAMBDA_INIT)
    qms = _masked_halves(q_ref, H_B)
    offs = _tile_offsets()
    slopes = [LOG2E * 2.0 ** -(2 * (h + 1)) for h in range(H_B)]

    def heads(j, k_rows, biases, carry):
        def score(i):
            k_t = k_ref[k_rows, (i // 2) * 128:(i // 2 + 1) * 128]
            return lambda: _dot_nt(k_t, qms[i]) + biases[i // 2]

        def value(i):
            return lambda: vt_ref[j, (i // 2) * 128:(i // 2 + 1) * 128, :]
        n = 2 * H_B
        return _online_steps([score(i) for i in range(n)], [value(i) for i in range(n)], carry)

    def off_diagonal(j, carry):
        dist = offs + ((qb - j) * KT).astype(F32)
        return heads(j, _key_rows(j), [-sl * dist for sl in slopes], carry)

    carry = lax.fori_loop(0, qb, off_diagonal,
                          tuple(_softmax_init(2 * D_B) for _ in range(2 * H_B)))
    key_chunk = lax.broadcasted_iota(I32, (KT, TQ), 0) // CHUNK
    query_chunk = lax.broadcasted_iota(I32, (KT, TQ), 1) // CHUNK
    allowed = key_chunk <= query_chunk
    dist = jnp.abs(offs)
    carry = heads(qb, slice(qb * KT, (qb + 1) * KT),
                  [jnp.where(allowed, -sl * dist, MASK_VALUE) for sl in slopes], carry)
    ys = []
    for h in range(H_B):
        (_, l1, acc1), (_, l2, acc2) = carry[2 * h], carry[2 * h + 1]
        o = acc1 / l1 - lam * (acc2 / l2)
        ys.append(o * lax.rsqrt(jnp.mean(o * o, axis=0, keepdims=True) + RMS_EPS))
    y = jnp.concatenate(ys, axis=0).T
    o_ref[...] = (y * sg_ref[...] * (1.0 - LAMBDA_INIT)).astype(BF16)


def _diff_block(qb, lam4, subln_g, qbn, kbn, vbt):
    B, S = qbn.shape[0], qbn.shape[1]
    L = (qb + 1) * TQ
    return pl.pallas_call(
        functools.partial(_diff_kernel, qb),
        grid=(B,),
        in_specs=[pl.BlockSpec((4, D_B), lambda b: (0, 0)),
                  pl.BlockSpec((1, W_B), lambda b: (0, 0)),
                  pl.BlockSpec((None, TQ, W_B), lambda b: (b, qb, 0)),
                  pl.BlockSpec((None, L, W_B), lambda b: (b, 0, 0)),
                  pl.BlockSpec((S // KT, W_B, KT), lambda b: (b, 0, 0))],
        out_specs=pl.BlockSpec((None, TQ, W_B), lambda b: (b, 0, 0)),
        out_shape=jax.ShapeDtypeStruct((B, TQ, W_B), BF16),
        compiler_params=pltpu.CompilerParams(dimension_semantics=("parallel",),
                                             vmem_limit_bytes=VMEM_LIMIT),
        name=f"diff_attn_q{qb}",
    )(lam4, jnp.tile(subln_g, H_B).reshape(1, W_B), qbn, kbn, vbt)


def _ffn_kernel(x_ref, mod_ref, ya_ref, yb_ref, gates_ref, wua_ref, wub_ref, wo_ref, g2_ref,
                w1_ref, w3_ref, w2_ref, o_ref):
    gates = gates_ref[...].astype(F32)
    merged = (gates[:, :D_MODEL] * _dot(ya_ref[...], wua_ref[...])
              + gates[:, D_MODEL:] * _dot(yb_ref[...], wub_ref[...]))
    x1 = x_ref[...] + mod_ref[2:3, :] * _dot(merged.astype(BF16), wo_ref[...])
    ms = jnp.mean(x1 * x1, axis=-1, keepdims=True)
    h2 = x1 * lax.rsqrt(ms + RMS_EPS) * g2_ref[...]
    h2 = (h2 * (1.0 + mod_ref[4:5, :]) + mod_ref[3:4, :]).astype(BF16)
    u = _dot(h2, w1_ref[...])
    act = (u * jax.nn.sigmoid(u) * _dot(h2, w3_ref[...])).astype(BF16)
    o_ref[...] = x1 + mod_ref[5:6, :] * _dot(act, w2_ref[...])


def _merge_ffn(x2, mod3, ya, yb, gates, w_up_a, w_up_b, w_o, norm2_g, w_ff1, w_ff3, w_ff2, S):
    N = x2.shape[0]
    tm = TM_FFN
    per_batch = S // tm

    def row_spec(w):
        return pl.BlockSpec((tm, w), lambda t: (t, 0))

    return pl.pallas_call(
        _ffn_kernel,
        grid=(N // tm,),
        in_specs=[row_spec(D_MODEL),
                  pl.BlockSpec((None, 6, D_MODEL), lambda t: (t // per_batch, 0, 0)),
                  row_spec(W_A), row_spec(W_B), row_spec(W_G),
                  _const_spec((W_A, D_MODEL)), _const_spec((W_B, D_MODEL)),
                  _const_spec((D_MODEL, D_MODEL)), _const_spec((1, D_MODEL)),
                  _const_spec((D_MODEL, D_FF)), _const_spec((D_MODEL, D_FF)),
                  _const_spec((D_FF, D_MODEL))],
        out_specs=row_spec(D_MODEL),
        out_shape=jax.ShapeDtypeStruct((N, D_MODEL), F32),
        compiler_params=pltpu.CompilerParams(dimension_semantics=("parallel",),
                                             vmem_limit_bytes=VMEM_LIMIT),
        name="merge_ffn",
    )(x2, mod3, ya, yb, gates, w_up_a.astype(BF16), w_up_b.astype(BF16), w_o.astype(BF16),
      norm2_g.reshape(1, D_MODEL), w_ff1.astype(BF16), w_ff3.astype(BF16), w_ff2.astype(BF16))


def kernel(x, c, w_ada, b_ada, norm1_g, w_in, qn_a, kn_a, qn_b, kn_b, lam_q1, lam_k1, lam_q2,
           lam_k2, subln_g, w_up_a, w_up_b, w_o, norm2_g, w_ff1, w_ff3, w_ff2):
    B, S, D = x.shape
    assert D == D_MODEL and S % TQ == 0 and w_ada.shape[0] == 1
    N = B * S
    x2 = x.reshape(N, D)
    mod3 = _modulation(c, w_ada[0], b_ada[0]).reshape(B, 6, D)

    (qa, ka, qbn, kbn, vat, vbt, qcat, kcat, wi, gates) = _projection(
        x2, mod3, norm1_g[0], w_in[0], qn_a[0], kn_a[0], qn_b[0], kn_b[0], S)
    r3 = lambda a: a.reshape(B, S, a.shape[-1])
    qa, ka, qbn, kbn, qcat, kcat = map(r3, (qa, ka, qbn, kbn, qcat, kcat))
    wt = wi[:, D_I:D_I + H_I].reshape(B, S, H_I).transpose(0, 2, 1)
    lam4 = jnp.stack([lam_q1[0], lam_k1[0], lam_q2[0], lam_k2[0]])

    ya_blocks, yb_blocks = [], []
    for qb in range(S // TQ):
        ya_blocks.append(_dsa_block(qb, qcat, kcat, wt, qa, ka, vat))
        yb_blocks.append(_diff_block(qb, lam4, subln_g[0], qbn, kbn, vbt))
    ya = jnp.concatenate(ya_blocks, axis=1).reshape(N, W_A)
    yb = jnp.concatenate(yb_blocks, axis=1).reshape(N, W_B)

    out = _merge_ffn(x2, mod3, ya, yb, gates, w_up_a[0], w_up_b[0], w_o[0], norm2_g[0],
                     w_ff1[0], w_ff3[0], w_ff2[0], S)
    return out.reshape(B, S, D)
```

```python
import functools

import numpy as np
import jax
import jax.numpy as jnp
from jax import lax
from jax.experimental import pallas as pl
from jax.experimental.pallas import tpu as pltpu

F32 = jnp.float32
BF16 = jnp.bfloat16
I32 = jnp.int32

D_MODEL = 1024
CHUNK = 64
H_A, D_A = 8, 64
H_I, D_I = 4, 64
TOPK_MAX = 256
H_B, D_B = 4, 64
D_FF = 2816
RMS_EPS = 1e-6
MASK_VALUE = -1e30
LAMBDA_INIT = 0.2
LOG2E = 1.4426950408889634

W_A = H_A * D_A
W_B = H_B * 2 * D_B
W_QKV = 3 * W_A + 3 * W_B
W_IDX = H_I * D_I + D_I + H_I
W_IDX_PAD = 384
W_G = 2 * D_MODEL

TQ = 256
KT = TQ
TM_PROJ = KT
TM_FFN = 256
CNT_ROWS = 256
STAGE_GROUP = 8
VMEM_LIMIT = 56 * 1024 * 1024

NT_DIMS = (((1,), (1,)), ((), ()))


def _dot(a, b):
    return jnp.dot(a, b, preferred_element_type=F32)


def _dot_nt(a, b):
    return lax.dot_general(a, b, NT_DIMS, preferred_element_type=F32)


def _split(x):
    hi = x.astype(BF16)
    lo = (x - hi.astype(F32)).astype(BF16)
    return hi, lo


def _dot3(a, b_hi, b_lo):
    a_hi, a_lo = _split(a)
    return _dot(a_hi, b_hi) + _dot(a_hi, b_lo) + _dot(a_lo, b_hi)


def _const_spec(shape):
    nd = len(shape)
    return pl.BlockSpec(shape, lambda *_: (0,) * nd, pipeline_mode=pl.Buffered(1))


def _mod_kernel(c_ref, w_ref, b_ref, o_ref):
    c = c_ref[...]
    cs = c * jax.nn.sigmoid(c)
    w_hi, w_lo = _split(w_ref[...])
    o_ref[...] = _dot3(cs, w_hi, w_lo) + b_ref[...]


def _modulation(c, w_ada, b_ada):
    B = c.shape[0]
    n = w_ada.shape[1]
    tn = 1024
    return pl.pallas_call(
        _mod_kernel,
        grid=(n // tn,),
        in_specs=[pl.BlockSpec((B, D_MODEL), lambda j: (0, 0)),
                  pl.BlockSpec((D_MODEL, tn), lambda j: (0, j)),
                  pl.BlockSpec((1, tn), lambda j: (0, j))],
        out_specs=pl.BlockSpec((B, tn), lambda j: (0, j)),
        out_shape=jax.ShapeDtypeStruct((B, n), F32),
        compiler_params=pltpu.CompilerParams(vmem_limit_bytes=VMEM_LIMIT),
        name="adaln_mod",
    )(c, w_ada, b_ada.reshape(1, n))


def _proj_kernel(x_ref, mod_ref, g1_ref, wqk_ref, wvt_ref, wih_ref, wil_ref, wg_ref, grp_ref,
                 gains_ref,
                 qa_ref, ka_ref, qb_ref, kb_ref, vat_ref, vbt_ref, qcat_ref, kcat_ref, wi_ref,
                 gates_ref):
    x = x_ref[...]
    ms = jnp.mean(x * x, axis=-1, keepdims=True)
    h = x * lax.rsqrt(ms + RMS_EPS) * g1_ref[...]
    h = h * (1.0 + mod_ref[1:2, :]) + mod_ref[0:1, :]
    h_hi, h_lo = _split(h)

    grp = grp_ref[...]

    def head_norm(z, gain):
        m = _dot((z * z).astype(BF16), grp) * (1.0 / D_A)
        return (z * lax.rsqrt(m + RMS_EPS) * gain).astype(BF16)

    qk = _dot(h_hi, wqk_ref[...])
    qa_ref[...] = head_norm(qk[:, 0:W_A], gains_ref[0:1, :])
    ka_ref[...] = head_norm(qk[:, W_A:2 * W_A], gains_ref[1:2, :])
    qb_ref[...] = head_norm(qk[:, 2 * W_A:2 * W_A + W_B], gains_ref[2:3, :])
    kb_ref[...] = head_norm(qk[:, 2 * W_A + W_B:], gains_ref[3:4, :])
    vt = _dot_nt(wvt_ref[...], h_hi)
    vat_ref[...] = vt[0:W_A, :].astype(BF16)
    vbt_ref[...] = vt[W_A:, :].astype(BF16)

    pidx = (_dot(h_hi, wih_ref[...]) + _dot(h_hi, wil_ref[...]) + _dot(h_lo, wih_ref[...]))
    low_half = lax.broadcasted_iota(I32, (x.shape[0], 128), 1) < D_I
    for g in range(H_I // 2):
        pair = pidx[:, g * 128:(g + 1) * 128]
        hi = pair.astype(BF16).astype(F32)
        lo = pair - hi
        hi_sw = pltpu.roll(hi, D_I, 1)
        lo_sw = pltpu.roll(lo, D_I, 1)
        base = 2 * g * 256
        qcat_ref[:, base:base + 128] = jnp.where(low_half, hi, lo_sw).astype(BF16)
        qcat_ref[:, base + 128:base + 256] = jnp.where(low_half, hi, 0.0).astype(BF16)
        qcat_ref[:, base + 256:base + 384] = jnp.where(low_half, hi_sw, lo).astype(BF16)
        qcat_ref[:, base + 384:base + 512] = jnp.where(low_half, hi_sw, 0.0).astype(BF16)
    tail = pidx[:, 256:384]
    hi = tail.astype(BF16).astype(F32)
    lo = tail - hi
    kcat_ref[:, 0:128] = jnp.where(low_half, hi, pltpu.roll(hi, D_I, 1)).astype(BF16)
    kcat_ref[:, 128:256] = jnp.where(low_half, lo, 0.0).astype(BF16)
    wi_ref[...] = tail

    gates_ref[...] = jax.nn.sigmoid(_dot(h_hi, wg_ref[...])).astype(BF16)


def _projection(x2, mod3, norm1_g, w_in, qn_a, kn_a, qn_b, kn_b, S):
    N = x2.shape[0]
    tm = TM_PROJ
    per_batch = S // tm
    w_qk = jnp.concatenate([w_in[:, 0:2 * W_A], w_in[:, 3 * W_A:3 * W_A + 2 * W_B]],
                           axis=1).astype(BF16)
    w_vt = jnp.concatenate([w_in[:, 2 * W_A:3 * W_A], w_in[:, 3 * W_A + 2 * W_B:W_QKV]],
                           axis=1).T.astype(BF16)
    w_idx = jnp.pad(w_in[:, W_QKV:W_QKV + W_IDX], ((0, 0), (0, W_IDX_PAD - W_IDX)))
    w_idx_hi = w_idx.astype(BF16)
    w_idx_lo = (w_idx - w_idx_hi.astype(F32)).astype(BF16)
    w_g = w_in[:, W_QKV + W_IDX:].astype(BF16)
    grp = jnp.asarray(np.kron(np.eye(W_A // D_A), np.ones((D_A, D_A))), BF16)
    gains = jnp.stack([jnp.tile(qn_a, H_A) * (LOG2E * D_A ** -0.5), jnp.tile(kn_a, H_A),
                       jnp.tile(qn_b, 2 * H_B) * (LOG2E * D_B ** -0.5), jnp.tile(kn_b, 2 * H_B)])

    def row_spec(w):
        return pl.BlockSpec((tm, w), lambda t: (t, 0))

    def rows(w, dt):
        return row_spec(w), jax.ShapeDtypeStruct((N, w), dt)

    def transposed(w):
        return (pl.BlockSpec((None, w, tm), lambda t: (t, 0, 0)),
                jax.ShapeDtypeStruct((N // tm, w, tm), BF16))

    outs = [rows(W_A, BF16), rows(W_A, BF16), rows(W_B, BF16), rows(W_B, BF16),
            transposed(W_A), transposed(W_B),
            rows(H_I * 256, BF16), rows(256, BF16), rows(128, F32), rows(W_G, BF16)]
    return pl.pallas_call(
        _proj_kernel,
        grid=(N // tm,),
        in_specs=[row_spec(D_MODEL),
                  pl.BlockSpec((None, 6, D_MODEL), lambda t: (t // per_batch, 0, 0)),
                  _const_spec((1, D_MODEL)),
                  _const_spec((D_MODEL, 2 * W_A + 2 * W_B)),
                  _const_spec((W_A + W_B, D_MODEL)),
                  _const_spec((D_MODEL, W_IDX_PAD)),
                  _const_spec((D_MODEL, W_IDX_PAD)),
                  _const_spec((D_MODEL, W_G)),
                  _const_spec((W_A, W_A)),
                  _const_spec((4, W_A))],
        out_specs=[o[0] for o in outs],
        out_shape=[o[1] for o in outs],
        compiler_params=pltpu.CompilerParams(dimension_semantics=("parallel",),
                                             vmem_limit_bytes=VMEM_LIMIT),
        name="in_proj",
    )(x2, mod3, norm1_g.reshape(1, D_MODEL), w_qk, w_vt, w_idx_hi, w_idx_lo, w_g, grp, gains)


def _count_ge(ref, n_rows, cand):
    def body(j, acc):
        blk = ref[pl.ds(pl.multiple_of(j * CNT_ROWS, CNT_ROWS), CNT_ROWS), :]
        m = jnp.where(blk >= cand, 1.0, 0.0)
        parts = [m[r * 8:(r + 1) * 8, :] for r in range(CNT_ROWS // 8)]
        while len(parts) > 1:
            parts = [a + b for a, b in zip(parts[0::2], parts[1::2])]
        return acc + parts[0]
    acc = lax.fori_loop(0, n_rows // CNT_ROWS, body, jnp.zeros((8, TQ), F32), unroll=2)
    return jnp.sum(acc, axis=0, keepdims=True)


def _count_ge_bf16(ref, n_rows, cand):
    rows = 16
    one, zero = jnp.ones((), BF16), jnp.zeros((), BF16)

    def body(j, acc):
        blk = ref[pl.ds(pl.multiple_of(j * CNT_ROWS, CNT_ROWS), CNT_ROWS), :]
        m = jnp.where(blk >= cand, one, zero)
        parts = [m[r * rows:(r + 1) * rows, :] for r in range(CNT_ROWS // rows)]
        while len(parts) > 1:
            parts = [a + b for a, b in zip(parts[0::2], parts[1::2])]
        return acc + parts[0]
    assert n_rows // rows <= 256
    acc = lax.fori_loop(0, n_rows // CNT_ROWS, body, jnp.zeros((rows, TQ), BF16), unroll=2)
    return jnp.sum(acc.astype(F32), axis=0, keepdims=True)


def _floor_to_bf16(x):
    bits = pltpu.bitcast(x, I32)
    bits = jnp.where(bits < 0, bits + jnp.int32(0xFFFF), bits)
    return pltpu.bitcast(bits & jnp.int32(-65536), F32).astype(BF16)


def _rank_to_f32(rank):
    key = rank ^ jnp.int32(-2 ** 31)
    return pltpu.bitcast(jnp.where(key < 0, key ^ jnp.int32(0x7FFFFFFF), key), F32)


def _store_bias(bias_ref, bias_t, L):
    bias_ref[0:L, :] = bias_t


def _select_bias(qb, qcat_ref, kcat_ref, wt_ref, score_ref, coarse_ref, eqi_ref, bias_ref):
    L = (qb + 1) * TQ
    kcat = kcat_ref[...]
    score = jnp.zeros((L, TQ), F32)
    for h in range(H_I):
        logits = _dot_nt(kcat, qcat_ref[:, h * 256:(h + 1) * 256])
        score = score + wt_ref[h:h + 1, :] * jnp.maximum(logits, 0.0)
    s_idx = lax.broadcasted_iota(I32, (L, TQ), 0)
    t_idx = lax.broadcasted_iota(I32, (L, TQ), 1) + qb * TQ
    admissible = (s_idx // CHUNK) <= (t_idx // CHUNK)

    if L <= TOPK_MAX:
        _store_bias(bias_ref, jnp.where(admissible, 0.0, MASK_VALUE), L)
        return

    score = jnp.where(admissible, score, -jnp.inf)
    score_ref[0:L, :] = score
    coarse_ref[0:L, :] = _floor_to_bf16(score)
    k_sel = float(TOPK_MAX)

    def thr_body(count, it, st):
        rank, cnt = st
        cand = rank | lax.shift_left(jnp.int32(1), 31 - it)
        c = count(_rank_to_f32(cand))
        take = c >= k_sel
        return jnp.where(take, cand, rank), jnp.where(take, c, cnt)

    st = (jnp.zeros((1, TQ), I32), jnp.full((1, TQ), float(L), F32))
    st = lax.fori_loop(0, 16, functools.partial(
        thr_body, lambda c: _count_ge_bf16(coarse_ref, L, c.astype(BF16))), st)
    rank, cnt = lax.fori_loop(16, 32, functools.partial(
        thr_body, lambda c: _count_ge(score_ref, L, c)), st)
    thr = _rank_to_f32(rank)
    tied = jnp.max(cnt) > k_sel

    @pl.when(jnp.logical_not(tied))
    def _():
        _store_bias(bias_ref, jnp.where(score_ref[0:L, :] >= thr, 0.0, MASK_VALUE), L)

    @pl.when(tied)
    def _():
        above = _rank_to_f32(rank + 1)
        need = k_sel - _count_ge(score_ref, L, above)
        big = jnp.int32(1 << 30)
        sc = score_ref[0:L, :]
        eqi_ref[0:L, :] = jnp.where(sc >= above, big, jnp.where(sc >= thr, s_idx, big))
        n_bits = max(1, int(np.ceil(np.log2(L))))

        def idx_body(it, j):
            cand = j | lax.shift_left(jnp.int32(1), n_bits - 1 - it)
            cnt_lt = float(L) - _count_ge(eqi_ref, L, cand)
            return jnp.where(cnt_lt < need, cand, j)
        j_sel = lax.fori_loop(0, n_bits, idx_body, jnp.zeros((1, TQ), I32))
        sel = jnp.where(score_ref[0:L, :] >= above, 0.0,
                        jnp.where(eqi_ref[0:L, :] <= j_sel, 0.0, MASK_VALUE))
        _store_bias(bias_ref, sel, L)


def _tile_offsets():
    return (lax.broadcasted_iota(I32, (KT, TQ), 1)
            - lax.broadcasted_iota(I32, (KT, TQ), 0)).astype(F32)


def _masked_halves(q_ref, n_groups):
    lane = lax.broadcasted_iota(I32, (TQ, 128), 1)
    low = jnp.where(lane < 64, 1.0, 0.0)
    high = 1.0 - low
    out = []
    for g in range(n_groups):
        q2 = q_ref[:, g * 128:(g + 1) * 128].astype(F32)
        out += [(q2 * low).astype(BF16), (q2 * high).astype(BF16)]
    return out


def _online_steps(scores, values, carry):
    out = []
    for g in range(0, len(scores), STAGE_GROUP):
        grp = slice(g, g + STAGE_GROUP)
        ss = [f() for f in scores[grp]]
        mid = []
        for s, (m, l, _) in zip(ss, carry[grp]):
            m_new = jnp.maximum(m, jnp.max(s, axis=0, keepdims=True))
            alpha = jnp.exp2(m - m_new)
            p = jnp.exp2(s - m_new)
            mid.append((m_new, alpha * l + jnp.sum(p, axis=0, keepdims=True), alpha,
                        p.astype(BF16)))
        out += [(m, l, alpha * acc + _dot(v(), p))
                for (m, l, alpha, p), v, (_, _, acc) in zip(mid, values[grp], carry[grp])]
    return tuple(out)


def _softmax_init(rows):
    return (jnp.full((1, TQ), -jnp.inf, F32), jnp.zeros((1, TQ), F32), jnp.zeros((rows, TQ), F32))


def _key_rows(j):
    return pl.ds(pl.multiple_of(j * KT, KT), KT)


def _dsa_kernel(qb, qcat_ref, kcat_ref, wt_ref, q_ref, k_ref, vt_ref, o_ref,
                score_ref, coarse_ref, eqi_ref, bias_ref):
    _select_bias(qb, qcat_ref, kcat_ref, wt_ref, score_ref, coarse_ref, eqi_ref, bias_ref)

    qms = _masked_halves(q_ref, H_A // 2)
    offs = _tile_offsets()
    slopes = [LOG2E * 2.0 ** -(h + 1) for h in range(H_A)]

    def heads(j, k_rows, bias_t, dist, carry):
        def score(h):
            k_t = k_ref[k_rows, (h // 2) * 128:(h // 2 + 1) * 128]
            return lambda: _dot_nt(k_t, qms[h]) + (bias_t - slopes[h] * dist)

        def value(h):
            return lambda: vt_ref[j, h * D_A:(h + 1) * D_A, :]
        return _online_steps([score(h) for h in range(H_A)], [value(h) for h in range(H_A)],
                             carry)

    def off_diagonal(j, carry):
        dist = offs + jnp.asarray((qb - j) * KT).astype(F32)
        return heads(j, _key_rows(j), bias_ref[_key_rows(j), :], dist, carry)

    carry = lax.fori_loop(0, qb, off_diagonal, tuple(_softmax_init(D_A) for _ in range(H_A)))
    diag = slice(qb * KT, (qb + 1) * KT)
    carry = heads(qb, diag, bias_ref[diag, :], jnp.abs(offs), carry)
    out_t = jnp.concatenate([acc / l for _, l, acc in carry], axis=0)
    o_ref[...] = out_t.T.astype(BF16)


def _dsa_block(qb, qcat, kcat, wt, qa, ka, vat):
    B, S = qa.shape[0], qa.shape[1]
    L = (qb + 1) * TQ
    return pl.pallas_call(
        functools.partial(_dsa_kernel, qb),
        grid=(B,),
        in_specs=[pl.BlockSpec((None, TQ, H_I * 256), lambda b: (b, qb, 0)),
                  pl.BlockSpec((None, L, 256), lambda b: (b, 0, 0)),
                  pl.BlockSpec((None, H_I, TQ), lambda b: (b, 0, qb)),
                  pl.BlockSpec((None, TQ, W_A), lambda b: (b, qb, 0)),
                  pl.BlockSpec((None, L, W_A), lambda b: (b, 0, 0)),
                  pl.BlockSpec((S // KT, W_A, KT), lambda b: (b, 0, 0))],
        out_specs=pl.BlockSpec((None, TQ, W_A), lambda b: (b, qb, 0)),
        out_shape=jax.ShapeDtypeStruct(qa.shape, BF16),
        input_output_aliases={3: 0},
        scratch_shapes=[pltpu.VMEM((S, TQ), F32), pltpu.VMEM((S, TQ), BF16),
                        pltpu.VMEM((S, TQ), I32), pltpu.VMEM((S, TQ), F32)],
        compiler_params=pltpu.CompilerParams(dimension_semantics=("parallel",),
                                             vmem_limit_bytes=VMEM_LIMIT),
        name=f"dsa_attn_q{qb}",
    )(qcat, kcat, wt, qa, ka, vat)


def _diff_kernel(qb, lam_ref, sg_ref, q_ref, k_ref, vt_ref, o_ref):
    lam = (jnp.exp(jnp.sum(lam_ref[0:1, :] * lam_ref[1:2, :], axis=-1, keepdims=True))
           - jnp.exp(jnp.sum(lam_ref[2:3, :] * lam_ref[3:4, :], axis=-1, keepdims=True))
           + LAMBDA_INIT)
    qms = _masked_halves(q_ref, H_B)
    offs = _tile_offsets()
    slopes = [LOG2E * 2.0 ** -(2 * (h + 1)) for h in range(H_B)]

    def heads(j, k_rows, biases, carry):
        def score(i):
            k_t = k_ref[k_rows, (i // 2) * 128:(i // 2 + 1) * 128]
            return lambda: _dot_nt(k_t, qms[i]) + biases[i // 2]

        def value(i):
            return lambda: vt_ref[j, (i // 2) * 128:(i // 2 + 1) * 128, :]
        n = 2 * H_B
        return _online_steps([score(i) for i in range(n)], [value(i) for i in range(n)], carry)

    def off_diagonal(j, carry):
        dist = offs + jnp.asarray((qb - j) * KT).astype(F32)
        return heads(j, _key_rows(j), [-sl * dist for sl in slopes], carry)

    carry = lax.fori_loop(0, qb, off_diagonal,
                          tuple(_softmax_init(2 * D_B) for _ in range(2 * H_B)))
    key_chunk = lax.broadcasted_iota(I32, (KT, TQ), 0) // CHUNK
    query_chunk = lax.broadcasted_iota(I32, (KT, TQ), 1) // CHUNK
    allowed = key_chunk <= query_chunk
    dist = jnp.abs(offs)
    carry = heads(qb, slice(qb * KT, (qb + 1) * KT),
                  [jnp.where(allowed, -sl * dist, MASK_VALUE) for sl in slopes], carry)
    ys = []
    for h in range(H_B):
        (_, l1, acc1), (_, l2, acc2) = carry[2 * h], carry[2 * h + 1]
        o = acc1 / l1 - lam * (acc2 / l2)
        ys.append(o * lax.rsqrt(jnp.mean(o * o, axis=0, keepdims=True) + RMS_EPS))
    y = jnp.concatenate(ys, axis=0).T
    o_ref[...] = (y * sg_ref[...] * (1.0 - LAMBDA_INIT)).astype(BF16)


def _diff_block(qb, lam4, subln_g, qbn, kbn, vbt):
    B, S = qbn.shape[0], qbn.shape[1]
    L = (qb + 1) * TQ
    return pl.pallas_call(
        functools.partial(_diff_kernel, qb),
        grid=(B,),
        in_specs=[pl.BlockSpec((4, D_B), lambda b: (0, 0)),
                  pl.BlockSpec((1, W_B), lambda b: (0, 0)),
                  pl.BlockSpec((None, TQ, W_B), lambda b: (b, qb, 0)),
                  pl.BlockSpec((None, L, W_B), lambda b: (b, 0, 0)),
                  pl.BlockSpec((S // KT, W_B, KT), lambda b: (b, 0, 0))],
        out_specs=pl.BlockSpec((None, TQ, W_B), lambda b: (b, qb, 0)),
        out_shape=jax.ShapeDtypeStruct(qbn.shape, BF16),
        input_output_aliases={2: 0},
        compiler_params=pltpu.CompilerParams(dimension_semantics=("parallel",),
                                             vmem_limit_bytes=VMEM_LIMIT),
        name=f"diff_attn_q{qb}",
    )(lam4, jnp.tile(subln_g, H_B).reshape(1, W_B), qbn, kbn, vbt)


def _ffn_kernel(x_ref, mod_ref, ya_ref, yb_ref, gates_ref, wua_ref, wub_ref, wo_ref, g2_ref,
                w1_ref, w3_ref, w2_ref, o_ref):
    gates = gates_ref[...].astype(F32)
    merged = (gates[:, :D_MODEL] * _dot(ya_ref[...], wua_ref[...])
              + gates[:, D_MODEL:] * _dot(yb_ref[...], wub_ref[...]))
    x1 = x_ref[...] + mod_ref[2:3, :] * _dot(merged.astype(BF16), wo_ref[...])
    ms = jnp.mean(x1 * x1, axis=-1, keepdims=True)
    h2 = x1 * lax.rsqrt(ms + RMS_EPS) * g2_ref[...]
    h2 = (h2 * (1.0 + mod_ref[4:5, :]) + mod_ref[3:4, :]).astype(BF16)
    u = _dot(h2, w1_ref[...])
    act = (u * jax.nn.sigmoid(u) * _dot(h2, w3_ref[...])).astype(BF16)
    o_ref[...] = x1 + mod_ref[5:6, :] * _dot(act, w2_ref[...])


def _merge_ffn(x2, mod3, ya, yb, gates, w_up_a, w_up_b, w_o, norm2_g, w_ff1, w_ff3, w_ff2, S):
    N = x2.shape[0]
    tm = TM_FFN
    per_batch = S // tm

    def row_spec(w):
        return pl.BlockSpec((tm, w), lambda t: (t, 0))

    return pl.pallas_call(
        _ffn_kernel,
        grid=(N // tm,),
        in_specs=[row_spec(D_MODEL),
                  pl.BlockSpec((None, 6, D_MODEL), lambda t: (t // per_batch, 0, 0)),
                  row_spec(W_A), row_spec(W_B), row_spec(W_G),
                  _const_spec((W_A, D_MODEL)), _const_spec((W_B, D_MODEL)),
                  _const_spec((D_MODEL, D_MODEL)), _const_spec((1, D_MODEL)),
                  _const_spec((D_MODEL, D_FF)), _const_spec((D_MODEL, D_FF)),
                  _const_spec((D_FF, D_MODEL))],
        out_specs=row_spec(D_MODEL),
        out_shape=jax.ShapeDtypeStruct((N, D_MODEL), F32),
        compiler_params=pltpu.CompilerParams(dimension_semantics=("parallel",),
                                             vmem_limit_bytes=VMEM_LIMIT),
        name="merge_ffn",
    )(x2, mod3, ya, yb, gates, w_up_a.astype(BF16), w_up_b.astype(BF16), w_o.astype(BF16),
      norm2_g.reshape(1, D_MODEL), w_ff1.astype(BF16), w_ff3.astype(BF16), w_ff2.astype(BF16))


def kernel(x, c, w_ada, b_ada, norm1_g, w_in, qn_a, kn_a, qn_b, kn_b, lam_q1, lam_k1, lam_q2,
           lam_k2, subln_g, w_up_a, w_up_b, w_o, norm2_g, w_ff1, w_ff3, w_ff2):
    B, S, D = x.shape
    assert D == D_MODEL and S % TQ == 0 and w_ada.shape[0] == 1
    N = B * S
    x2 = x.reshape(N, D)
    mod3 = _modulation(c, w_ada[0], b_ada[0]).reshape(B, 6, D)

    (qa, ka, qbn, kbn, vat, vbt, qcat, kcat, wi, gates) = _projection(
        x2, mod3, norm1_g[0], w_in[0], qn_a[0], kn_a[0], qn_b[0], kn_b[0], S)
    r3 = lambda a: a.reshape(B, S, a.shape[-1])
    qa, ka, qbn, kbn, qcat, kcat = map(r3, (qa, ka, qbn, kbn, qcat, kcat))
    wt = wi[:, D_I:D_I + H_I].reshape(B, S, H_I).transpose(0, 2, 1)
    lam4 = jnp.stack([lam_q1[0], lam_k1[0], lam_q2[0], lam_k2[0]])

    ya, yb = qa, qbn
    for qb in range(S // TQ):
        ya = _dsa_block(qb, qcat, kcat, wt, ya, ka, vat)
        yb = _diff_block(qb, lam4, subln_g[0], yb, kbn, vbt)
    ya = ya.reshape(N, W_A)
    yb = yb.reshape(N, W_B)

    out = _merge_ffn(x2, mod3, ya, yb, gates, w_up_a[0], w_up_b[0], w_o[0], norm2_g[0],
                     w_ff1[0], w_ff3[0], w_ff2[0], S)
    return out.reshape(B, S, D)
```

```python
import functools

import numpy as np
import jax
import jax.numpy as jnp
from jax import lax
from jax.experimental import pallas as pl
from jax.experimental.pallas import tpu as pltpu

F32 = jnp.float32
BF16 = jnp.bfloat16
I32 = jnp.int32

D_MODEL = 1024
CHUNK = 64
H_A, D_A = 8, 64
H_I, D_I = 4, 64
TOPK_MAX = 256
H_B, D_B = 4, 64
D_FF = 2816
RMS_EPS = 1e-6
MASK_VALUE = -1e30
LAMBDA_INIT = 0.2
LOG2E = 1.4426950408889634

W_A = H_A * D_A
W_B = H_B * 2 * D_B
W_QKV = 3 * W_A + 3 * W_B
W_IDX = H_I * D_I + D_I + H_I
W_IDX_PAD = 384
W_G = 2 * D_MODEL

TQ = 256
KT = TQ
TM_PROJ = KT
TM_FFN = 256
CNT_ROWS = 256
STAGE_GROUP = 8
VMEM_LIMIT = 56 * 1024 * 1024

NT_DIMS = (((1,), (1,)), ((), ()))


def _dot(a, b):
    return jnp.dot(a, b, preferred_element_type=F32)


def _dot_nt(a, b):
    return lax.dot_general(a, b, NT_DIMS, preferred_element_type=F32)


def _split(x):
    hi = x.astype(BF16)
    lo = (x - hi.astype(F32)).astype(BF16)
    return hi, lo


def _dot3(a, b_hi, b_lo):
    a_hi, a_lo = _split(a)
    return _dot(a_hi, b_hi) + _dot(a_hi, b_lo) + _dot(a_lo, b_hi)


def _const_spec(shape):
    nd = len(shape)
    return pl.BlockSpec(shape, lambda *_: (0,) * nd, pipeline_mode=pl.Buffered(1))


def _mod_kernel(c_ref, w_ref, b_ref, o_ref):
    c = c_ref[...]
    cs = c * jax.nn.sigmoid(c)
    w_hi, w_lo = _split(w_ref[...])
    o_ref[...] = _dot3(cs, w_hi, w_lo) + b_ref[...]


def _modulation(c, w_ada, b_ada):
    B = c.shape[0]
    n = w_ada.shape[1]
    tn = 1024
    return pl.pallas_call(
        _mod_kernel,
        grid=(n // tn,),
        in_specs=[pl.BlockSpec((B, D_MODEL), lambda j: (0, 0)),
                  pl.BlockSpec((D_MODEL, tn), lambda j: (0, j)),
                  pl.BlockSpec((1, tn), lambda j: (0, j))],
        out_specs=pl.BlockSpec((B, tn), lambda j: (0, j)),
        out_shape=jax.ShapeDtypeStruct((B, n), F32),
        compiler_params=pltpu.CompilerParams(vmem_limit_bytes=VMEM_LIMIT),
        name="adaln_mod",
    )(c, w_ada, b_ada.reshape(1, n))


def _proj_kernel(x_ref, mod_ref, g1_ref, wqk_ref, wvt_ref, wih_ref, wil_ref, wg_ref, grp_ref,
                 gains_ref,
                 qa_ref, ka_ref, qb_ref, kb_ref, vat_ref, vbt_ref, qcat_ref, kcat_ref, wi_ref,
                 gates_ref):
    x = x_ref[...]
    ms = jnp.mean(x * x, axis=-1, keepdims=True)
    h = x * lax.rsqrt(ms + RMS_EPS) * g1_ref[...]
    h = h * (1.0 + mod_ref[1:2, :]) + mod_ref[0:1, :]
    h_hi, h_lo = _split(h)

    grp = grp_ref[...]

    def head_norm(z, gain):
        m = _dot((z * z).astype(BF16), grp) * (1.0 / D_A)
        return (z * lax.rsqrt(m + RMS_EPS) * gain).astype(BF16)

    qk = _dot(h_hi, wqk_ref[...])
    qa_ref[...] = head_norm(qk[:, 0:W_A], gains_ref[0:1, :])
    ka_ref[...] = head_norm(qk[:, W_A:2 * W_A], gains_ref[1:2, :])
    qb_ref[...] = head_norm(qk[:, 2 * W_A:2 * W_A + W_B], gains_ref[2:3, :])
    kb_ref[...] = head_norm(qk[:, 2 * W_A + W_B:], gains_ref[3:4, :])
    vt = _dot_nt(wvt_ref[...], h_hi)
    vat_ref[...] = vt[0:W_A, :].astype(BF16)
    vbt_ref[...] = vt[W_A:, :].astype(BF16)

    pidx = (_dot(h_hi, wih_ref[...]) + _dot(h_hi, wil_ref[...]) + _dot(h_lo, wih_ref[...]))
    low_half = lax.broadcasted_iota(I32, (x.shape[0], 128), 1) < D_I
    for g in range(H_I // 2):
        pair = pidx[:, g * 128:(g + 1) * 128]
        hi = pair.astype(BF16).astype(F32)
        lo = pair - hi
        hi_sw = pltpu.roll(hi, D_I, 1)
        lo_sw = pltpu.roll(lo, D_I, 1)
        base = 2 * g * 256
        qcat_ref[:, base:base + 128] = jnp.where(low_half, hi, lo_sw).astype(BF16)
        qcat_ref[:, base + 128:base + 256] = jnp.where(low_half, hi, 0.0).astype(BF16)
        qcat_ref[:, base + 256:base + 384] = jnp.where(low_half, hi_sw, lo).astype(BF16)
        qcat_ref[:, base + 384:base + 512] = jnp.where(low_half, hi_sw, 0.0).astype(BF16)
    tail = pidx[:, 256:384]
    hi = tail.astype(BF16).astype(F32)
    lo = tail - hi
    kcat_ref[:, 0:128] = jnp.where(low_half, hi, pltpu.roll(hi, D_I, 1)).astype(BF16)
    kcat_ref[:, 128:256] = jnp.where(low_half, lo, 0.0).astype(BF16)
    wi_ref[...] = tail

    gates_ref[...] = jax.nn.sigmoid(_dot(h_hi, wg_ref[...])).astype(BF16)


def _projection(x2, mod3, norm1_g, w_in, qn_a, kn_a, qn_b, kn_b, S):
    N = x2.shape[0]
    tm = TM_PROJ
    per_batch = S // tm
    w_qk = jnp.concatenate([w_in[:, 0:2 * W_A], w_in[:, 3 * W_A:3 * W_A + 2 * W_B]],
                           axis=1).astype(BF16)
    w_vt = jnp.concatenate([w_in[:, 2 * W_A:3 * W_A], w_in[:, 3 * W_A + 2 * W_B:W_QKV]],
                           axis=1).T.astype(BF16)
    w_idx = jnp.pad(w_in[:, W_QKV:W_QKV + W_IDX], ((0, 0), (0, W_IDX_PAD - W_IDX)))
    w_idx_hi = w_idx.astype(BF16)
    w_idx_lo = (w_idx - w_idx_hi.astype(F32)).astype(BF16)
    w_g = w_in[:, W_QKV + W_IDX:].astype(BF16)
    grp = jnp.asarray(np.kron(np.eye(W_A // D_A), np.ones((D_A, D_A))), BF16)
    gains = jnp.stack([jnp.tile(qn_a, H_A) * (LOG2E * D_A ** -0.5), jnp.tile(kn_a, H_A),
                       jnp.tile(qn_b, 2 * H_B) * (LOG2E * D_B ** -0.5), jnp.tile(kn_b, 2 * H_B)])

    def row_spec(w):
        return pl.BlockSpec((tm, w), lambda t: (t, 0))

    def rows(w, dt):
        return row_spec(w), jax.ShapeDtypeStruct((N, w), dt)

    def transposed(w):
        return (pl.BlockSpec((None, w, tm), lambda t: (t, 0, 0)),
                jax.ShapeDtypeStruct((N // tm, w, tm), BF16))

    outs = [rows(W_A, BF16), rows(W_A, BF16), rows(W_B, BF16), rows(W_B, BF16),
            transposed(W_A), transposed(W_B),
            rows(H_I * 256, BF16), rows(256, BF16), rows(128, F32), rows(W_G, BF16)]
    return pl.pallas_call(
        _proj_kernel,
        grid=(N // tm,),
        in_specs=[row_spec(D_MODEL),
                  pl.BlockSpec((None, 6, D_MODEL), lambda t: (t // per_batch, 0, 0)),
                  _const_spec((1, D_MODEL)),
                  _const_spec((D_MODEL, 2 * W_A + 2 * W_B)),
                  _const_spec((W_A + W_B, D_MODEL)),
                  _const_spec((D_MODEL, W_IDX_PAD)),
                  _const_spec((D_MODEL, W_IDX_PAD)),
                  _const_spec((D_MODEL, W_G)),
                  _const_spec((W_A, W_A)),
                  _const_spec((4, W_A))],
        out_specs=[o[0] for o in outs],
        out_shape=[o[1] for o in outs],
        compiler_params=pltpu.CompilerParams(dimension_semantics=("parallel",),
                                             vmem_limit_bytes=VMEM_LIMIT),
        name="in_proj",
    )(x2, mod3, norm1_g.reshape(1, D_MODEL), w_qk, w_vt, w_idx_hi, w_idx_lo, w_g, grp, gains)


def _count_ge(ref, n_rows, cand):
    def body(j, acc):
        blk = ref[pl.ds(pl.multiple_of(j * CNT_ROWS, CNT_ROWS), CNT_ROWS), :]
        m = jnp.where(blk >= cand, 1.0, 0.0)
        parts = [m[r * 8:(r + 1) * 8, :] for r in range(CNT_ROWS // 8)]
        while len(parts) > 1:
            parts = [a + b for a, b in zip(parts[0::2], parts[1::2])]
        return acc + parts[0]
    acc = lax.fori_loop(0, n_rows // CNT_ROWS, body, jnp.zeros((8, TQ), F32), unroll=2)
    return jnp.sum(acc, axis=0, keepdims=True)


def _count_ge_bf16(ref, n_rows, cand):
    rows = 16
    one, zero = jnp.ones((), BF16), jnp.zeros((), BF16)

    def body(j, acc):
        blk = ref[pl.ds(pl.multiple_of(j * CNT_ROWS, CNT_ROWS), CNT_ROWS), :]
        m = jnp.where(blk >= cand, one, zero)
        parts = [m[r * rows:(r + 1) * rows, :] for r in range(CNT_ROWS // rows)]
        while len(parts) > 1:
            parts = [a + b for a, b in zip(parts[0::2], parts[1::2])]
        return acc + parts[0]
    assert n_rows // rows <= 256
    acc = lax.fori_loop(0, n_rows // CNT_ROWS, body, jnp.zeros((rows, TQ), BF16), unroll=2)
    return jnp.sum(acc.astype(F32), axis=0, keepdims=True)


def _rank_to_f32(rank):
    key = rank ^ jnp.int32(-2 ** 31)
    return pltpu.bitcast(jnp.where(key < 0, key ^ jnp.int32(0x7FFFFFFF), key), F32)


def _store_bias(bias_ref, bias_t, L):
    bias_ref[0:L, :] = bias_t


def _select_bias(qb, qcat_ref, kcat_ref, wt_ref, score_ref, coarse_ref, eqi_ref, bias_ref):
    L = (qb + 1) * TQ
    kcat = kcat_ref[...]
    score = jnp.zeros((L, TQ), F32)
    for h in range(H_I):
        logits = _dot_nt(kcat, qcat_ref[:, h * 256:(h + 1) * 256])
        score = score + wt_ref[h:h + 1, :] * jnp.maximum(logits, 0.0)
    s_idx = lax.broadcasted_iota(I32, (L, TQ), 0)
    t_idx = lax.broadcasted_iota(I32, (L, TQ), 1) + qb * TQ
    admissible = (s_idx // CHUNK) <= (t_idx // CHUNK)

    if L <= TOPK_MAX:
        _store_bias(bias_ref, jnp.where(admissible, 0.0, MASK_VALUE), L)
        return

    score = jnp.where(admissible, score, -jnp.inf)
    score_ref[0:L, :] = score
    coarse_ref[0:L, :] = score.astype(BF16)
    k_sel = float(TOPK_MAX)

    def coarse_body(it, rank):
        cand = rank | lax.shift_left(jnp.int32(1), 31 - it)
        c = _count_ge_bf16(coarse_ref, L, _rank_to_f32(cand).astype(BF16))
        return jnp.where(c >= k_sel, cand, rank)
    base = lax.fori_loop(0, 16, coarse_body, jnp.zeros((1, TQ), I32)) - jnp.int32(1 << 16)

    def fine_body(it, st):
        off, cnt = st
        cand = off | lax.shift_left(jnp.int32(1), 16 - it)
        c = _count_ge(score_ref, L, _rank_to_f32(base + cand))
        take = c >= k_sel
        return jnp.where(take, cand, off), jnp.where(take, c, cnt)
    off, cnt = lax.fori_loop(
        0, 17, fine_body, (jnp.zeros((1, TQ), I32), jnp.full((1, TQ), float(L), F32)))
    rank = base + off
    thr = _rank_to_f32(rank)
    tied = jnp.max(cnt) > k_sel

    @pl.when(jnp.logical_not(tied))
    def _():
        _store_bias(bias_ref, jnp.where(score_ref[0:L, :] >= thr, 0.0, MASK_VALUE), L)

    @pl.when(tied)
    def _():
        above = _rank_to_f32(rank + 1)
        need = k_sel - _count_ge(score_ref, L, above)
        big = jnp.int32(1 << 30)
        sc = score_ref[0:L, :]
        eqi_ref[0:L, :] = jnp.where(sc >= above, big, jnp.where(sc >= thr, s_idx, big))
        n_bits = max(1, int(np.ceil(np.log2(L))))

        def idx_body(it, j):
            cand = j | lax.shift_left(jnp.int32(1), n_bits - 1 - it)
            cnt_lt = float(L) - _count_ge(eqi_ref, L, cand)
            return jnp.where(cnt_lt < need, cand, j)
        j_sel = lax.fori_loop(0, n_bits, idx_body, jnp.zeros((1, TQ), I32))
        sel = jnp.where(score_ref[0:L, :] >= above, 0.0,
                        jnp.where(eqi_ref[0:L, :] <= j_sel, 0.0, MASK_VALUE))
        _store_bias(bias_ref, sel, L)


def _tile_offsets():
    return (lax.broadcasted_iota(I32, (KT, TQ), 1)
            - lax.broadcasted_iota(I32, (KT, TQ), 0)).astype(F32)


def _masked_halves(q_ref, n_groups):
    lane = lax.broadcasted_iota(I32, (TQ, 128), 1)
    low = jnp.where(lane < 64, 1.0, 0.0)
    high = 1.0 - low
    out = []
    for g in range(n_groups):
        q2 = q_ref[:, g * 128:(g + 1) * 128].astype(F32)
        out += [(q2 * low).astype(BF16), (q2 * high).astype(BF16)]
    return out


def _online_steps(scores, values, carry):
    out = []
    for g in range(0, len(scores), STAGE_GROUP):
        grp = slice(g, g + STAGE_GROUP)
        ss = [f() for f in scores[grp]]
        mid = []
        for s, (m, l, _) in zip(ss, carry[grp]):
            m_new = jnp.maximum(m, jnp.max(s, axis=0, keepdims=True))
            alpha = jnp.exp2(m - m_new)
            p = jnp.exp2(s - m_new)
            mid.append((m_new, alpha * l + jnp.sum(p, axis=0, keepdims=True), alpha,
                        p.astype(BF16)))
        out += [(m, l, alpha * acc + _dot(v(), p))
                for (m, l, alpha, p), v, (_, _, acc) in zip(mid, values[grp], carry[grp])]
    return tuple(out)


def _softmax_init(rows):
    return (jnp.full((1, TQ), -jnp.inf, F32), jnp.zeros((1, TQ), F32), jnp.zeros((rows, TQ), F32))


def _key_rows(j):
    return pl.ds(pl.multiple_of(j * KT, KT), KT)


def _dsa_kernel(qb, qcat_ref, kcat_ref, wt_ref, q_ref, k_ref, vt_ref, o_ref,
                score_ref, coarse_ref, eqi_ref, bias_ref):
    _select_bias(qb, qcat_ref, kcat_ref, wt_ref, score_ref, coarse_ref, eqi_ref, bias_ref)

    qms = _masked_halves(q_ref, H_A // 2)
    offs = _tile_offsets()
    slopes = [LOG2E * 2.0 ** -(h + 1) for h in range(H_A)]

    def heads(j, k_rows, bias_t, dist, carry):
        def score(h):
            k_t = k_ref[k_rows, (h // 2) * 128:(h // 2 + 1) * 128]
            return lambda: _dot_nt(k_t, qms[h]) + (bias_t - slopes[h] * dist)

        def value(h):
            return lambda: vt_ref[j, h * D_A:(h + 1) * D_A, :]
        return _online_steps([score(h) for h in range(H_A)], [value(h) for h in range(H_A)],
                             carry)

    def off_diagonal(j, carry):
        dist = offs + jnp.asarray((qb - j) * KT).astype(F32)
        return heads(j, _key_rows(j), bias_ref[_key_rows(j), :], dist, carry)

    carry = lax.fori_loop(0, qb, off_diagonal, tuple(_softmax_init(D_A) for _ in range(H_A)))
    diag = slice(qb * KT, (qb + 1) * KT)
    carry = heads(qb, diag, bias_ref[diag, :], jnp.abs(offs), carry)
    out_t = jnp.concatenate([acc / l for _, l, acc in carry], axis=0)
    o_ref[...] = out_t.T.astype(BF16)


def _dsa_block(qb, qcat, kcat, wt, qa, ka, vat):
    B, S = qa.shape[0], qa.shape[1]
    L = (qb + 1) * TQ
    return pl.pallas_call(
        functools.partial(_dsa_kernel, qb),
        grid=(B,),
        in_specs=[pl.BlockSpec((None, TQ, H_I * 256), lambda b: (b, qb, 0)),
                  pl.BlockSpec((None, L, 256), lambda b: (b, 0, 0)),
                  pl.BlockSpec((None, H_I, TQ), lambda b: (b, 0, qb)),
                  pl.BlockSpec((None, TQ, W_A), lambda b: (b, qb, 0)),
                  pl.BlockSpec((None, L, W_A), lambda b: (b, 0, 0)),
                  pl.BlockSpec((S // KT, W_A, KT), lambda b: (b, 0, 0))],
        out_specs=pl.BlockSpec((None, TQ, W_A), lambda b: (b, qb, 0)),
        out_shape=jax.ShapeDtypeStruct(qa.shape, BF16),
        input_output_aliases={3: 0},
        scratch_shapes=[pltpu.VMEM((S, TQ), F32), pltpu.VMEM((S, TQ), BF16),
                        pltpu.VMEM((S, TQ), I32), pltpu.VMEM((S, TQ), F32)],
        compiler_params=pltpu.CompilerParams(dimension_semantics=("parallel",),
                                             vmem_limit_bytes=VMEM_LIMIT),
        name=f"dsa_attn_q{qb}",
    )(qcat, kcat, wt, qa, ka, vat)


def _diff_kernel(qb, lam_ref, sg_ref, q_ref, k_ref, vt_ref, o_ref):
    lam = (jnp.exp(jnp.sum(lam_ref[0:1, :] * lam_ref[1:2, :], axis=-1, keepdims=True))
           - jnp.exp(jnp.sum(lam_ref[2:3, :] * lam_ref[3:4, :], axis=-1, keepdims=True))
           + LAMBDA_INIT)
    qms = _masked_halves(q_ref, H_B)
    offs = _tile_offsets()
    slopes = [LOG2E * 2.0 ** -(2 * (h + 1)) for h in range(H_B)]

    def heads(j, k_rows, biases, carry):
        def score(i):
            k_t = k_ref[k_rows, (i // 2) * 128:(i // 2 + 1) * 128]
            return lambda: _dot_nt(k_t, qms[i]) + biases[i // 2]

        def value(i):
            return lambda: vt_ref[j, (i // 2) * 128:(i // 2 + 1) * 128, :]
        n = 2 * H_B
        return _online_steps([score(i) for i in range(n)], [value(i) for i in range(n)], carry)

    def off_diagonal(j, carry):
        dist = offs + jnp.asarray((qb - j) * KT).astype(F32)
        return heads(j, _key_rows(j), [-sl * dist for sl in slopes], carry)

    carry = lax.fori_loop(0, qb, off_diagonal,
                          tuple(_softmax_init(2 * D_B) for _ in range(2 * H_B)))
    key_chunk = lax.broadcasted_iota(I32, (KT, TQ), 0) // CHUNK
    query_chunk = lax.broadcasted_iota(I32, (KT, TQ), 1) // CHUNK
    allowed = key_chunk <= query_chunk
    dist = jnp.abs(offs)
    carry = heads(qb, slice(qb * KT, (qb + 1) * KT),
                  [jnp.where(allowed, -sl * dist, MASK_VALUE) for sl in slopes], carry)
    ys = []
    for h in range(H_B):
        (_, l1, acc1), (_, l2, acc2) = carry[2 * h], carry[2 * h + 1]
        o = acc1 / l1 - lam * (acc2 / l2)
        ys.append(o * lax.rsqrt(jnp.mean(o * o, axis=0, keepdims=True) + RMS_EPS))
    y = jnp.concatenate(ys, axis=0).T
    o_ref[...] = (y * sg_ref[...] * (1.0 - LAMBDA_INIT)).astype(BF16)


def _diff_block(qb, lam4, subln_g, qbn, kbn, vbt):
    B, S = qbn.shape[0], qbn.shape[1]
    L = (qb + 1) * TQ
    return pl.pallas_call(
        functools.partial(_diff_kernel, qb),
        grid=(B,),
        in_specs=[pl.BlockSpec((4, D_B), lambda b: (0, 0)),
                  pl.BlockSpec((1, W_B), lambda b: (0, 0)),
                  pl.BlockSpec((None, TQ, W_B), lambda b: (b, qb, 0)),
                  pl.BlockSpec((None, L, W_B), lambda b: (b, 0, 0)),
                  pl.BlockSpec((S // KT, W_B, KT), lambda b: (b, 0, 0))],
        out_specs=pl.BlockSpec((None, TQ, W_B), lambda b: (b, qb, 0)),
        out_shape=jax.ShapeDtypeStruct(qbn.shape, BF16),
        input_output_aliases={2: 0},
        compiler_params=pltpu.CompilerParams(dimension_semantics=("parallel",),
                                             vmem_limit_bytes=VMEM_LIMIT),
        name=f"diff_attn_q{qb}",
    )(lam4, jnp.tile(subln_g, H_B).reshape(1, W_B), qbn, kbn, vbt)


def _ffn_kernel(x_ref, mod_ref, ya_ref, yb_ref, gates_ref, wua_ref, wub_ref, wo_ref, g2_ref,
                w1_ref, w3_ref, w2_ref, o_ref):
    gates = gates_ref[...].astype(F32)
    merged = (gates[:, :D_MODEL] * _dot(ya_ref[...], wua_ref[...])
              + gates[:, D_MODEL:] * _dot(yb_ref[...], wub_ref[...]))
    x1 = x_ref[...] + mod_ref[2:3, :] * _dot(merged.astype(BF16), wo_ref[...])
    ms = jnp.mean(x1 * x1, axis=-1, keepdims=True)
    h2 = x1 * lax.rsqrt(ms + RMS_EPS) * g2_ref[...]
    h2 = (h2 * (1.0 + mod_ref[4:5, :]) + mod_ref[3:4, :]).astype(BF16)
    u = _dot(h2, w1_ref[...])
    act = (u * jax.nn.sigmoid(u) * _dot(h2, w3_ref[...])).astype(BF16)
    o_ref[...] = x1 + mod_ref[5:6, :] * _dot(act, w2_ref[...])


def _merge_ffn(x2, mod3, ya, yb, gates, w_up_a, w_up_b, w_o, norm2_g, w_ff1, w_ff3, w_ff2, S):
    N = x2.shape[0]
    tm = TM_FFN
    per_batch = S // tm

    def row_spec(w):
        return pl.BlockSpec((tm, w), lambda t: (t, 0))

    return pl.pallas_call(
        _ffn_kernel,
        grid=(N // tm,),
        in_specs=[row_spec(D_MODEL),
                  pl.BlockSpec((None, 6, D_MODEL), lambda t: (t // per_batch, 0, 0)),
                  row_spec(W_A), row_spec(W_B), row_spec(W_G),
                  _const_spec((W_A, D_MODEL)), _const_spec((W_B, D_MODEL)),
                  _const_spec((D_MODEL, D_MODEL)), _const_spec((1, D_MODEL)),
                  _const_spec((D_MODEL, D_FF)), _const_spec((D_MODEL, D_FF)),
                  _const_spec((D_FF, D_MODEL))],
        out_specs=row_spec(D_MODEL),
        out_shape=jax.ShapeDtypeStruct((N, D_MODEL), F32),
        compiler_params=pltpu.CompilerParams(dimension_semantics=("parallel",),
                                             vmem_limit_bytes=VMEM_LIMIT),
        name="merge_ffn",
    )(x2, mod3, ya, yb, gates, w_up_a.astype(BF16), w_up_b.astype(BF16), w_o.astype(BF16),
      norm2_g.reshape(1, D_MODEL), w_ff1.astype(BF16), w_ff3.astype(BF16), w_ff2.astype(BF16))


def kernel(x, c, w_ada, b_ada, norm1_g, w_in, qn_a, kn_a, qn_b, kn_b, lam_q1, lam_k1, lam_q2,
           lam_k2, subln_g, w_up_a, w_up_b, w_o, norm2_g, w_ff1, w_ff3, w_ff2):
    B, S, D = x.shape
    assert D == D_MODEL and S % TQ == 0 and w_ada.shape[0] == 1
    N = B * S
    x2 = x.reshape(N, D)
    mod3 = _modulation(c, w_ada[0], b_ada[0]).reshape(B, 6, D)

    (qa, ka, qbn, kbn, vat, vbt, qcat, kcat, wi, gates) = _projection(
        x2, mod3, norm1_g[0], w_in[0], qn_a[0], kn_a[0], qn_b[0], kn_b[0], S)
    r3 = lambda a: a.reshape(B, S, a.shape[-1])
    qa, ka, qbn, kbn, qcat, kcat = map(r3, (qa, ka, qbn, kbn, qcat, kcat))
    wt = wi[:, D_I:D_I + H_I].reshape(B, S, H_I).transpose(0, 2, 1)
    lam4 = jnp.stack([lam_q1[0], lam_k1[0], lam_q2[0], lam_k2[0]])

    ya, yb = qa, qbn
    for qb in range(S // TQ):
        ya = _dsa_block(qb, qcat, kcat, wt, ya, ka, vat)
        yb = _diff_block(qb, lam4, subln_g[0], yb, kbn, vbt)
    ya = ya.reshape(N, W_A)
    yb = yb.reshape(N, W_B)

    out = _merge_ffn(x2, mod3, ya, yb, gates, w_up_a[0], w_up_b[0], w_o[0], norm2_g[0],
                     w_ff1[0], w_ff3[0], w_ff2[0], S)
    return out.reshape(B, S, D)
```

```python
import functools

import numpy as np
import jax
import jax.numpy as jnp
from jax import lax
from jax.experimental import pallas as pl
from jax.experimental.pallas import tpu as pltpu

F32 = jnp.float32
BF16 = jnp.bfloat16
I32 = jnp.int32

D_MODEL = 1024
CHUNK = 64
H_A, D_A = 8, 64
H_I, D_I = 4, 64
TOPK_MAX = 256
H_B, D_B = 4, 64
D_FF = 2816
RMS_EPS = 1e-6
MASK_VALUE = -1e30
LAMBDA_INIT = 0.2
LOG2E = 1.4426950408889634
SLOPES_A = [2.0 ** (-8.0 * (h + 1) / H_A) for h in range(H_A)]
SLOPES_B = [2.0 ** (-8.0 * (h + 1) / H_B) for h in range(H_B)]

W_A = H_A * D_A
W_B = H_B * 2 * D_B
W_QKV = 3 * W_A + 3 * W_B
W_IDX = H_I * D_I + D_I + H_I
W_IDX_PAD = 384
W_G = 2 * D_MODEL

TQ = 256
KT = TQ
TM_PROJ = 2 * KT
TM_FFN = 512
CNT_ROWS = 256
STAGE_GROUP = 8
VMEM_LIMIT = 56 * 1024 * 1024

NT_DIMS = (((1,), (1,)), ((), ()))


def _dot(a, b):
    return jnp.dot(a, b, preferred_element_type=F32)


def _dot_nt(a, b):
    return lax.dot_general(a, b, NT_DIMS, preferred_element_type=F32)


def _split(x):
    hi = x.astype(BF16)
    lo = (x - hi.astype(F32)).astype(BF16)
    return hi, lo


def _dot3(a, b_hi, b_lo):
    a_hi, a_lo = _split(a)
    return _dot(a_hi, b_hi) + _dot(a_hi, b_lo) + _dot(a_lo, b_hi)


def _const_spec(shape):
    nd = len(shape)
    return pl.BlockSpec(shape, lambda *_: (0,) * nd, pipeline_mode=pl.Buffered(1))


def _mod_kernel(c_ref, w_ref, b_ref, o_ref):
    c = c_ref[...]
    cs = c * jax.nn.sigmoid(c)
    w_hi, w_lo = _split(w_ref[...])
    o_ref[...] = _dot3(cs, w_hi, w_lo) + b_ref[...]


def _modulation(c, w_ada, b_ada):
    B = c.shape[0]
    n = w_ada.shape[1]
    tn = 1024
    return pl.pallas_call(
        _mod_kernel,
        grid=(n // tn,),
        in_specs=[pl.BlockSpec((B, D_MODEL), lambda j: (0, 0)),
                  pl.BlockSpec((D_MODEL, tn), lambda j: (0, j)),
                  pl.BlockSpec((1, tn), lambda j: (0, j))],
        out_specs=pl.BlockSpec((B, tn), lambda j: (0, j)),
        out_shape=jax.ShapeDtypeStruct((B, n), F32),
        compiler_params=pltpu.CompilerParams(vmem_limit_bytes=VMEM_LIMIT),
        name="adaln_mod",
    )(c, w_ada, b_ada.reshape(1, n))


def _proj_kernel(x_ref, mod_ref, g1_ref, wqk_ref, wvt_ref, wih_ref, wil_ref, wg_ref, grp_ref,
                 gains_ref,
                 qa_ref, ka_ref, qb_ref, kb_ref, vat_ref, vbt_ref, qcat_ref, kcat_ref, wi_ref,
                 gates_ref):
    x = x_ref[...]
    ms = jnp.mean(x * x, axis=-1, keepdims=True)
    h = x * lax.rsqrt(ms + RMS_EPS) * g1_ref[...]
    h = h * (1.0 + mod_ref[1:2, :]) + mod_ref[0:1, :]
    h_hi, h_lo = _split(h)

    grp = grp_ref[...]

    def head_norm(z, gain):
        m = _dot((z * z).astype(BF16), grp) * (1.0 / D_A)
        return (z * lax.rsqrt(m + RMS_EPS) * gain).astype(BF16)

    qk = _dot(h_hi, wqk_ref[...])
    qa_ref[...] = head_norm(qk[:, 0:W_A], gains_ref[0:1, :])
    ka_ref[...] = head_norm(qk[:, W_A:2 * W_A], gains_ref[1:2, :])
    qb_ref[...] = head_norm(qk[:, 2 * W_A:2 * W_A + W_B], gains_ref[2:3, :])
    kb_ref[...] = head_norm(qk[:, 2 * W_A + W_B:], gains_ref[3:4, :])
    vt = _dot_nt(wvt_ref[...], h_hi)
    for s in range(x.shape[0] // KT):
        vat_ref[s] = vt[0:W_A, s * KT:(s + 1) * KT].astype(BF16)
        vbt_ref[s] = vt[W_A:, s * KT:(s + 1) * KT].astype(BF16)

    pidx = (_dot(h_hi, wih_ref[...]) + _dot(h_hi, wil_ref[...]) + _dot(h_lo, wih_ref[...]))
    low_half = lax.broadcasted_iota(I32, (x.shape[0], 128), 1) < D_I
    for g in range(H_I // 2):
        pair = pidx[:, g * 128:(g + 1) * 128]
        hi = pair.astype(BF16).astype(F32)
        lo = pair - hi
        hi_sw = pltpu.roll(hi, D_I, 1)
        lo_sw = pltpu.roll(lo, D_I, 1)
        base = 2 * g * 256
        qcat_ref[:, base:base + 128] = jnp.where(low_half, hi, lo_sw).astype(BF16)
        qcat_ref[:, base + 128:base + 256] = jnp.where(low_half, hi, 0.0).astype(BF16)
        qcat_ref[:, base + 256:base + 384] = jnp.where(low_half, hi_sw, lo).astype(BF16)
        qcat_ref[:, base + 384:base + 512] = jnp.where(low_half, hi_sw, 0.0).astype(BF16)
    tail = pidx[:, 256:384]
    hi = tail.astype(BF16).astype(F32)
    lo = tail - hi
    kcat_ref[:, 0:128] = jnp.where(low_half, hi, pltpu.roll(hi, D_I, 1)).astype(BF16)
    kcat_ref[:, 128:256] = jnp.where(low_half, lo, 0.0).astype(BF16)
    wi_ref[...] = tail

    gates_ref[...] = jax.nn.sigmoid(_dot(h_hi, wg_ref[...])).astype(BF16)


def _projection(x2, mod3, norm1_g, w_in, qn_a, kn_a, qn_b, kn_b, S):
    N = x2.shape[0]
    tm = TM_PROJ
    per_batch = S // tm
    w_qk = jnp.concatenate([w_in[:, 0:2 * W_A], w_in[:, 3 * W_A:3 * W_A + 2 * W_B]],
                           axis=1).astype(BF16)
    w_vt = jnp.concatenate([w_in[:, 2 * W_A:3 * W_A], w_in[:, 3 * W_A + 2 * W_B:W_QKV]],
                           axis=1).T.astype(BF16)
    w_idx = jnp.pad(w_in[:, W_QKV:W_QKV + W_IDX], ((0, 0), (0, W_IDX_PAD - W_IDX)))
    w_idx_hi = w_idx.astype(BF16)
    w_idx_lo = (w_idx - w_idx_hi.astype(F32)).astype(BF16)
    w_g = w_in[:, W_QKV + W_IDX:].astype(BF16)
    grp = jnp.asarray(np.kron(np.eye(W_A // D_A), np.ones((D_A, D_A))), BF16)
    gains = jnp.stack([jnp.tile(qn_a, H_A) * (LOG2E * D_A ** -0.5), jnp.tile(kn_a, H_A),
                       jnp.tile(qn_b, 2 * H_B) * (LOG2E * D_B ** -0.5), jnp.tile(kn_b, 2 * H_B)])

    def row_spec(w):
        return pl.BlockSpec((tm, w), lambda t: (t, 0))

    def rows(w, dt):
        return row_spec(w), jax.ShapeDtypeStruct((N, w), dt)

    def transposed(w):
        return (pl.BlockSpec((tm // KT, w, KT), lambda t: (t, 0, 0)),
                jax.ShapeDtypeStruct((N // KT, w, KT), BF16))

    outs = [rows(W_A, BF16), rows(W_A, BF16), rows(W_B, BF16), rows(W_B, BF16),
            transposed(W_A), transposed(W_B),
            rows(H_I * 256, BF16), rows(256, BF16), rows(128, F32), rows(W_G, BF16)]
    return pl.pallas_call(
        _proj_kernel,
        grid=(N // tm,),
        in_specs=[row_spec(D_MODEL),
                  pl.BlockSpec((None, 6, D_MODEL), lambda t: (t // per_batch, 0, 0)),
                  _const_spec((1, D_MODEL)),
                  _const_spec((D_MODEL, 2 * W_A + 2 * W_B)),
                  _const_spec((W_A + W_B, D_MODEL)),
                  _const_spec((D_MODEL, W_IDX_PAD)),
                  _const_spec((D_MODEL, W_IDX_PAD)),
                  _const_spec((D_MODEL, W_G)),
                  _const_spec((W_A, W_A)),
                  _const_spec((4, W_A))],
        out_specs=[o[0] for o in outs],
        out_shape=[o[1] for o in outs],
        compiler_params=pltpu.CompilerParams(dimension_semantics=("parallel",),
                                             vmem_limit_bytes=VMEM_LIMIT),
        name="in_proj",
    )(x2, mod3, norm1_g.reshape(1, D_MODEL), w_qk, w_vt, w_idx_hi, w_idx_lo, w_g, grp, gains)


def _count_ge(ref, n_rows, cand):
    def body(j, acc):
        blk = ref[pl.ds(pl.multiple_of(j * CNT_ROWS, CNT_ROWS), CNT_ROWS), :]
        m = jnp.where(blk >= cand, 1.0, 0.0)
        parts = [m[r * 8:(r + 1) * 8, :] for r in range(CNT_ROWS // 8)]
        while len(parts) > 1:
            parts = [a + b for a, b in zip(parts[0::2], parts[1::2])]
        return acc + parts[0]
    acc = lax.fori_loop(0, n_rows // CNT_ROWS, body, jnp.zeros((8, TQ), F32), unroll=2)
    return jnp.sum(acc, axis=0, keepdims=True)


def _count_ge_bf16(ref, n_rows, cand):
    rows = 16
    one, zero = jnp.ones((), BF16), jnp.zeros((), BF16)

    def body(j, acc):
        blk = ref[pl.ds(pl.multiple_of(j * CNT_ROWS, CNT_ROWS), CNT_ROWS), :]
        m = jnp.where(blk >= cand, one, zero)
        parts = [m[r * rows:(r + 1) * rows, :] for r in range(CNT_ROWS // rows)]
        while len(parts) > 1:
            parts = [a + b for a, b in zip(parts[0::2], parts[1::2])]
        return acc + parts[0]
    assert n_rows // rows <= 256
    acc = lax.fori_loop(0, n_rows // CNT_ROWS, body, jnp.zeros((rows, TQ), BF16), unroll=2)
    return jnp.sum(acc.astype(F32), axis=0, keepdims=True)


def _rank_to_f32(rank):
    key = rank ^ jnp.int32(-2 ** 31)
    return pltpu.bitcast(jnp.where(key < 0, key ^ jnp.int32(0x7FFFFFFF), key), F32)


def _store_bias(bias_ref, bias_t, L):
    bias_ref[0:L, :] = bias_t


def _select_bias(qb, qcat_ref, kcat_ref, wt_ref, score_ref, coarse_ref, eqi_ref, bias_ref):
    L = (qb + 1) * TQ
    admissible = ((lax.broadcasted_iota(I32, (KT, TQ), 0) // CHUNK)
                  <= (lax.broadcasted_iota(I32, (KT, TQ), 1) // CHUNK))
    if L <= TOPK_MAX:
        _store_bias(bias_ref, jnp.where(admissible, 0.0, MASK_VALUE), L)
        return

    for j in range(L // KT):
        rows = slice(j * KT, (j + 1) * KT)
        score = jnp.zeros((KT, TQ), F32)
        for h in range(H_I):
            logits = _dot_nt(kcat_ref[rows, :], qcat_ref[:, h * 256:(h + 1) * 256])
            score = score + wt_ref[h:h + 1, :] * jnp.maximum(logits, 0.0)
        if j == qb:
            score = jnp.where(admissible, score, -jnp.inf)
        score_ref[rows, :] = score
        coarse_ref[rows, :] = score.astype(BF16)
    k_sel = float(TOPK_MAX)

    def coarse_body(it, rank):
        cand = rank | lax.shift_left(jnp.int32(1), 31 - it)
        c = _count_ge_bf16(coarse_ref, L, _rank_to_f32(cand).astype(BF16))
        return jnp.where(c >= k_sel, cand, rank)
    base = lax.fori_loop(0, 16, coarse_body, jnp.zeros((1, TQ), I32)) - jnp.int32(1 << 16)

    def fine_body(it, st):
        off, cnt = st
        cand = off | lax.shift_left(jnp.int32(1), 16 - it)
        c = _count_ge(score_ref, L, _rank_to_f32(base + cand))
        take = c >= k_sel
        return jnp.where(take, cand, off), jnp.where(take, c, cnt)
    off, cnt = lax.fori_loop(
        0, 17, fine_body, (jnp.zeros((1, TQ), I32), jnp.full((1, TQ), float(L), F32)))
    rank = base + off
    thr = _rank_to_f32(rank)
    tied = jnp.max(cnt) > k_sel

    @pl.when(jnp.logical_not(tied))
    def _():
        _store_bias(bias_ref, jnp.where(score_ref[0:L, :] >= thr, 0.0, MASK_VALUE), L)

    @pl.when(tied)
    def _():
        above = _rank_to_f32(rank + 1)
        need = k_sel - _count_ge(score_ref, L, above)
        big = jnp.int32(1 << 30)
        sc = score_ref[0:L, :]
        s_idx = lax.broadcasted_iota(I32, (L, TQ), 0)
        eqi_ref[0:L, :] = jnp.where(sc >= above, big, jnp.where(sc >= thr, s_idx, big))
        n_bits = max(1, int(np.ceil(np.log2(L))))

        def idx_body(it, j):
            cand = j | lax.shift_left(jnp.int32(1), n_bits - 1 - it)
            cnt_lt = float(L) - _count_ge(eqi_ref, L, cand)
            return jnp.where(cnt_lt < need, cand, j)
        j_sel = lax.fori_loop(0, n_bits, idx_body, jnp.zeros((1, TQ), I32))
        sel = jnp.where(score_ref[0:L, :] >= above, 0.0,
                        jnp.where(eqi_ref[0:L, :] <= j_sel, 0.0, MASK_VALUE))
        _store_bias(bias_ref, sel, L)


def _tile_offsets():
    return (lax.broadcasted_iota(I32, (KT, TQ), 1)
            - lax.broadcasted_iota(I32, (KT, TQ), 0)).astype(F32)


def _masked_halves(q_ref, n_groups):
    lane = lax.broadcasted_iota(I32, (TQ, 128), 1)
    low = jnp.where(lane < 64, 1.0, 0.0)
    high = 1.0 - low
    out = []
    for g in range(n_groups):
        q2 = q_ref[:, g * 128:(g + 1) * 128].astype(F32)
        out += [(q2 * low).astype(BF16), (q2 * high).astype(BF16)]
    return out


def _online_steps(scores, values, carry):
    out = []
    for g in range(0, len(scores), STAGE_GROUP):
        grp = slice(g, g + STAGE_GROUP)
        ss = [f() for f in scores[grp]]
        for s, v, (m, l, acc) in zip(ss, values[grp], carry[grp]):
            m_new = jnp.maximum(m, jnp.max(s, axis=0, keepdims=True))
            alpha = jnp.exp2(m - m_new)
            p = jnp.exp2(s - m_new)
            out.append((m_new, alpha * l + jnp.sum(p, axis=0, keepdims=True),
                        alpha * acc + _dot(v(), p.astype(BF16))))
    return tuple(out)


def _softmax_init(rows):
    return (jnp.full((1, TQ), -jnp.inf, F32), jnp.zeros((1, TQ), F32), jnp.zeros((rows, TQ), F32))


def _key_rows(j):
    return pl.ds(pl.multiple_of(j * KT, KT), KT)


def _split3(x):
    a = x.astype(BF16)
    r = x - a.astype(F32)
    b = r.astype(BF16)
    return a, b, (r - b.astype(F32)).astype(BF16)


def _alibi_tables(S, slopes):
    pos = jnp.arange(S, dtype=I32)
    hi = (pos // 64 * 64).astype(F32)
    lo = (pos % 64).astype(F32)
    lam = jnp.full((S,), LOG2E, F32)
    k_cols = [*_split3(lam), *_split3(lam), *_split3(lam * hi), *_split3(lam * lo)]
    kaug = jnp.pad(jnp.stack(k_cols, axis=1), ((0, 0), (0, 128 - len(k_cols))))
    qaug = []
    for sl in slopes:
        q_cols = [-sl * hi] * 3 + [-sl * lo] * 3 + [jnp.full((S,), sl, F32)] * 6
        qaug.append(jnp.pad(jnp.stack(q_cols, axis=1), ((0, 0), (0, 128 - len(q_cols)))))
    return kaug, jnp.stack(qaug).astype(BF16)


def _dsa_kernel(qb, qcat_ref, kcat_ref, wt_ref, q_ref, k_ref, vt_ref, kaug_ref, qaug_ref, o_ref,
                score_ref, coarse_ref, eqi_ref, bias_ref):
    _select_bias(qb, qcat_ref, kcat_ref, wt_ref, score_ref, coarse_ref, eqi_ref, bias_ref)

    qms = _masked_halves(q_ref, H_A // 2)
    q_ext = [jnp.concatenate([qms[h], qaug_ref[h]], axis=1) for h in range(H_A)]

    def values(j):
        return [lambda h=h: vt_ref[j, h * D_A:(h + 1) * D_A, :] for h in range(H_A)]

    def off_diagonal(j, carry):
        rows = _key_rows(j)
        bias_t = bias_ref[rows, :]
        k_ext = [jnp.concatenate([k_ref[rows, g * 128:(g + 1) * 128], kaug_ref[rows, :]], axis=1)
                 for g in range(H_A // 2)]
        scores = [lambda h=h: _dot_nt(k_ext[h // 2], q_ext[h]) + bias_t for h in range(H_A)]
        return _online_steps(scores, values(j), carry)

    carry = lax.fori_loop(0, qb, off_diagonal, tuple(_softmax_init(D_A) for _ in range(H_A)))

    diag = slice(qb * KT, (qb + 1) * KT)
    bias_t = bias_ref[diag, :]
    dist = jnp.abs(_tile_offsets())
    scores = [lambda h=h: (_dot_nt(k_ref[diag, (h // 2) * 128:(h // 2 + 1) * 128], qms[h])
                           + (bias_t - (LOG2E * SLOPES_A[h]) * dist)) for h in range(H_A)]
    carry = _online_steps(scores, values(qb), carry)
    out_t = jnp.concatenate([acc / l for _, l, acc in carry], axis=0)
    o_ref[...] = out_t.T.astype(BF16)


def _dsa_block(qb, qcat, kcat, wt, qa, ka, vat, kaug, qaug):
    B, S = qa.shape[0], qa.shape[1]
    L = (qb + 1) * TQ
    return pl.pallas_call(
        functools.partial(_dsa_kernel, qb),
        grid=(B,),
        in_specs=[pl.BlockSpec((None, TQ, H_I * 256), lambda b: (b, qb, 0)),
                  pl.BlockSpec((None, L, 256), lambda b: (b, 0, 0)),
                  pl.BlockSpec((None, H_I, TQ), lambda b: (b, 0, qb)),
                  pl.BlockSpec((None, TQ, W_A), lambda b: (b, qb, 0)),
                  pl.BlockSpec((None, L, W_A), lambda b: (b, 0, 0)),
                  pl.BlockSpec((S // KT, W_A, KT), lambda b: (b, 0, 0)),
                  pl.BlockSpec((L, 128), lambda b: (0, 0)),
                  pl.BlockSpec((H_A, TQ, 128), lambda b: (0, qb, 0))],
        out_specs=pl.BlockSpec((None, TQ, W_A), lambda b: (b, qb, 0)),
        out_shape=jax.ShapeDtypeStruct(qa.shape, BF16),
        input_output_aliases={3: 0},
        scratch_shapes=[pltpu.VMEM((S, TQ), F32), pltpu.VMEM((S, TQ), BF16),
                        pltpu.VMEM((S, TQ), I32), pltpu.VMEM((S, TQ), F32)],
        compiler_params=pltpu.CompilerParams(dimension_semantics=("parallel",),
                                             vmem_limit_bytes=VMEM_LIMIT),
        name=f"dsa_attn_q{qb}",
    )(qcat, kcat, wt, qa, ka, vat, kaug, qaug)


def _diff_kernel(qb, lam_ref, sg_ref, q_ref, k_ref, vt_ref, kaug_ref, qaug_ref, o_ref):
    lam = (jnp.exp(jnp.sum(lam_ref[0:1, :] * lam_ref[1:2, :], axis=-1, keepdims=True))
           - jnp.exp(jnp.sum(lam_ref[2:3, :] * lam_ref[3:4, :], axis=-1, keepdims=True))
           + LAMBDA_INIT)
    n = 2 * H_B
    qms = _masked_halves(q_ref, H_B)
    q_ext = [jnp.concatenate([qms[i], qaug_ref[i // 2]], axis=1) for i in range(n)]

    def values(j):
        return [lambda i=i: vt_ref[j, (i // 2) * 128:(i // 2 + 1) * 128, :] for i in range(n)]

    def off_diagonal(j, carry):
        rows = _key_rows(j)
        k_ext = [jnp.concatenate([k_ref[rows, h * 128:(h + 1) * 128], kaug_ref[rows, :]], axis=1)
                 for h in range(H_B)]
        scores = [lambda i=i: _dot_nt(k_ext[i // 2], q_ext[i]) for i in range(n)]
        return _online_steps(scores, values(j), carry)

    carry = lax.fori_loop(0, qb, off_diagonal, tuple(_softmax_init(2 * D_B) for _ in range(n)))

    diag = slice(qb * KT, (qb + 1) * KT)
    key_chunk = lax.broadcasted_iota(I32, (KT, TQ), 0) // CHUNK
    query_chunk = lax.broadcasted_iota(I32, (KT, TQ), 1) // CHUNK
    allowed = key_chunk <= query_chunk
    dist = jnp.abs(_tile_offsets())
    biases = [jnp.where(allowed, -(LOG2E * sl) * dist, MASK_VALUE) for sl in SLOPES_B]
    scores = [lambda i=i: (_dot_nt(k_ref[diag, (i // 2) * 128:(i // 2 + 1) * 128], qms[i])
                           + biases[i // 2]) for i in range(n)]
    carry = _online_steps(scores, values(qb), carry)
    ys = []
    for h in range(H_B):
        (_, l1, acc1), (_, l2, acc2) = carry[2 * h], carry[2 * h + 1]
        o = acc1 / l1 - lam * (acc2 / l2)
        ys.append(o * lax.rsqrt(jnp.mean(o * o, axis=0, keepdims=True) + RMS_EPS))
    y = jnp.concatenate(ys, axis=0).T
    o_ref[...] = (y * sg_ref[...] * (1.0 - LAMBDA_INIT)).astype(BF16)


def _diff_block(qb, lam4, subln_g, qbn, kbn, vbt, kaug, qaug):
    B, S = qbn.shape[0], qbn.shape[1]
    L = (qb + 1) * TQ
    return pl.pallas_call(
        functools.partial(_diff_kernel, qb),
        grid=(B,),
        in_specs=[pl.BlockSpec((4, D_B), lambda b: (0, 0)),
                  pl.BlockSpec((1, W_B), lambda b: (0, 0)),
                  pl.BlockSpec((None, TQ, W_B), lambda b: (b, qb, 0)),
                  pl.BlockSpec((None, L, W_B), lambda b: (b, 0, 0)),
                  pl.BlockSpec((S // KT, W_B, KT), lambda b: (b, 0, 0)),
                  pl.BlockSpec((L, 128), lambda b: (0, 0)),
                  pl.BlockSpec((H_B, TQ, 128), lambda b: (0, qb, 0))],
        out_specs=pl.BlockSpec((None, TQ, W_B), lambda b: (b, qb, 0)),
        out_shape=jax.ShapeDtypeStruct(qbn.shape, BF16),
        input_output_aliases={2: 0},
        compiler_params=pltpu.CompilerParams(dimension_semantics=("parallel",),
                                             vmem_limit_bytes=VMEM_LIMIT),
        name=f"diff_attn_q{qb}",
    )(lam4, jnp.tile(subln_g, H_B).reshape(1, W_B), qbn, kbn, vbt, kaug, qaug)


def _ffn_kernel(x_ref, mod_ref, ya_ref, yb_ref, gates_ref, wua_ref, wub_ref, wo_ref, g2_ref,
                w1_ref, w3_ref, w2_ref, o_ref):
    gates = gates_ref[...].astype(F32)
    merged = (gates[:, :D_MODEL] * _dot(ya_ref[...], wua_ref[...])
              + gates[:, D_MODEL:] * _dot(yb_ref[...], wub_ref[...]))
    x1 = x_ref[...] + mod_ref[2:3, :] * _dot(merged.astype(BF16), wo_ref[...])
    ms = jnp.mean(x1 * x1, axis=-1, keepdims=True)
    h2 = x1 * lax.rsqrt(ms + RMS_EPS) * g2_ref[...]
    h2 = (h2 * (1.0 + mod_ref[4:5, :]) + mod_ref[3:4, :]).astype(BF16)
    u = _dot(h2, w1_ref[...])
    act = (u * jax.nn.sigmoid(u) * _dot(h2, w3_ref[...])).astype(BF16)
    o_ref[...] = x1 + mod_ref[5:6, :] * _dot(act, w2_ref[...])


def _merge_ffn(x2, mod3, ya, yb, gates, w_up_a, w_up_b, w_o, norm2_g, w_ff1, w_ff3, w_ff2, S):
    N = x2.shape[0]
    tm = TM_FFN
    per_batch = S // tm

    def row_spec(w):
        return pl.BlockSpec((tm, w), lambda t: (t, 0))

    return pl.pallas_call(
        _ffn_kernel,
        grid=(N // tm,),
        in_specs=[row_spec(D_MODEL),
                  pl.BlockSpec((None, 6, D_MODEL), lambda t: (t // per_batch, 0, 0)),
                  row_spec(W_A), row_spec(W_B), row_spec(W_G),
                  _const_spec((W_A, D_MODEL)), _const_spec((W_B, D_MODEL)),
                  _const_spec((D_MODEL, D_MODEL)), _const_spec((1, D_MODEL)),
                  _const_spec((D_MODEL, D_FF)), _const_spec((D_MODEL, D_FF)),
                  _const_spec((D_FF, D_MODEL))],
        out_specs=row_spec(D_MODEL),
        out_shape=jax.ShapeDtypeStruct((N, D_MODEL), F32),
        compiler_params=pltpu.CompilerParams(dimension_semantics=("parallel",),
                                             vmem_limit_bytes=VMEM_LIMIT),
        name="merge_ffn",
    )(x2, mod3, ya, yb, gates, w_up_a.astype(BF16), w_up_b.astype(BF16), w_o.astype(BF16),
      norm2_g.reshape(1, D_MODEL), w_ff1.astype(BF16), w_ff3.astype(BF16), w_ff2.astype(BF16))


def kernel(x, c, w_ada, b_ada, norm1_g, w_in, qn_a, kn_a, qn_b, kn_b, lam_q1, lam_k1, lam_q2,
           lam_k2, subln_g, w_up_a, w_up_b, w_o, norm2_g, w_ff1, w_ff3, w_ff2):
    B, S, D = x.shape
    assert D == D_MODEL and S % TQ == 0 and w_ada.shape[0] == 1
    N = B * S
    x2 = x.reshape(N, D)
    mod3 = _modulation(c, w_ada[0], b_ada[0]).reshape(B, 6, D)

    (qa, ka, qbn, kbn, vat, vbt, qcat, kcat, wi, gates) = _projection(
        x2, mod3, norm1_g[0], w_in[0], qn_a[0], kn_a[0], qn_b[0], kn_b[0], S)
    r3 = lambda a: a.reshape(B, S, a.shape[-1])
    qa, ka, qbn, kbn, qcat, kcat = map(r3, (qa, ka, qbn, kbn, qcat, kcat))
    wt = wi[:, D_I:D_I + H_I].reshape(B, S, H_I).transpose(0, 2, 1)
    lam4 = jnp.stack([lam_q1[0], lam_k1[0], lam_q2[0], lam_k2[0]])

    kaug, qaug_a = _alibi_tables(S, SLOPES_A)
    _, qaug_b = _alibi_tables(S, SLOPES_B)
    ya, yb = qa, qbn
    for qb in range(S // TQ):
        ya = _dsa_block(qb, qcat, kcat, wt, ya, ka, vat, kaug, qaug_a)
        yb = _diff_block(qb, lam4, subln_g[0], yb, kbn, vbt, kaug, qaug_b)
    ya = ya.reshape(N, W_A)
    yb = yb.reshape(N, W_B)

    out = _merge_ffn(x2, mod3, ya, yb, gates, w_up_a[0], w_up_b[0], w_o[0], norm2_g[0],
                     w_ff1[0], w_ff3[0], w_ff2[0], S)
    return out.reshape(B, S, D)
```

```python
import functools

import numpy as np
import jax
import jax.numpy as jnp
from jax import lax
from jax.experimental import pallas as pl
from jax.experimental.pallas import tpu as pltpu

F32 = jnp.float32
BF16 = jnp.bfloat16
I32 = jnp.int32

D_MODEL = 1024
CHUNK = 64
H_A, D_A = 8, 64
H_I, D_I = 4, 64
TOPK_MAX = 256
H_B, D_B = 4, 64
D_FF = 2816
RMS_EPS = 1e-6
MASK_VALUE = -1e30
LAMBDA_INIT = 0.2
LOG2E = 1.4426950408889634
SLOPES_A = [2.0 ** (-8.0 * (h + 1) / H_A) for h in range(H_A)]
SLOPES_B = [2.0 ** (-8.0 * (h + 1) / H_B) for h in range(H_B)]

W_A = H_A * D_A
W_B = H_B * 2 * D_B
W_QKV = 3 * W_A + 3 * W_B
W_IDX = H_I * D_I + D_I + H_I
W_IDX_PAD = 384
W_G = 2 * D_MODEL

TQ = 256
KT = TQ
TM_PROJ = 2 * KT
TM_FFN = 512
CNT_ROWS = 256
STAGE_GROUP = 8
VMEM_LIMIT = 56 * 1024 * 1024

NT_DIMS = (((1,), (1,)), ((), ()))


def _dot(a, b):
    return jnp.dot(a, b, preferred_element_type=F32)


def _dot_nt(a, b):
    return lax.dot_general(a, b, NT_DIMS, preferred_element_type=F32)


def _split(x):
    hi = x.astype(BF16)
    lo = (x - hi.astype(F32)).astype(BF16)
    return hi, lo


def _dot3(a, b_hi, b_lo):
    a_hi, a_lo = _split(a)
    return _dot(a_hi, b_hi) + _dot(a_hi, b_lo) + _dot(a_lo, b_hi)


def _const_spec(shape):
    nd = len(shape)
    return pl.BlockSpec(shape, lambda *_: (0,) * nd, pipeline_mode=pl.Buffered(1))


def _mod_kernel(c_ref, w_ref, b_ref, o_ref):
    c = c_ref[...]
    cs = c * jax.nn.sigmoid(c)
    w_hi, w_lo = _split(w_ref[...])
    o_ref[...] = _dot3(cs, w_hi, w_lo) + b_ref[...]


def _modulation(c, w_ada, b_ada):
    B = c.shape[0]
    n = w_ada.shape[1]
    tn = 1024
    return pl.pallas_call(
        _mod_kernel,
        grid=(n // tn,),
        in_specs=[pl.BlockSpec((B, D_MODEL), lambda j: (0, 0)),
                  pl.BlockSpec((D_MODEL, tn), lambda j: (0, j)),
                  pl.BlockSpec((1, tn), lambda j: (0, j))],
        out_specs=pl.BlockSpec((B, tn), lambda j: (0, j)),
        out_shape=jax.ShapeDtypeStruct((B, n), F32),
        compiler_params=pltpu.CompilerParams(vmem_limit_bytes=VMEM_LIMIT),
        name="adaln_mod",
    )(c, w_ada, b_ada.reshape(1, n))


def _proj_kernel(x_ref, mod_ref, g1_ref, wqk_ref, wvt_ref, wih_ref, wil_ref, wg_ref, grp_ref,
                 gains_ref,
                 qa_ref, ka_ref, qb_ref, kb_ref, vat_ref, vbt_ref, qcat_ref, kcat_ref, wi_ref,
                 gates_ref):
    x = x_ref[...]
    ms = jnp.mean(x * x, axis=-1, keepdims=True)
    h = x * lax.rsqrt(ms + RMS_EPS) * g1_ref[...]
    h = h * (1.0 + mod_ref[1:2, :]) + mod_ref[0:1, :]
    h_hi, h_lo = _split(h)

    grp = grp_ref[...]

    def head_norm(z, gain):
        m = _dot((z * z).astype(BF16), grp) * (1.0 / D_A)
        return (z * lax.rsqrt(m + RMS_EPS) * gain).astype(BF16)

    qk = _dot(h_hi, wqk_ref[...])
    qa_ref[...] = head_norm(qk[:, 0:W_A], gains_ref[0:1, :])
    ka_ref[...] = head_norm(qk[:, W_A:2 * W_A], gains_ref[1:2, :])
    qb_ref[...] = head_norm(qk[:, 2 * W_A:2 * W_A + W_B], gains_ref[2:3, :])
    kb_ref[...] = head_norm(qk[:, 2 * W_A + W_B:], gains_ref[3:4, :])
    vt = _dot_nt(wvt_ref[...], h_hi)
    for s in range(x.shape[0] // KT):
        vat_ref[s] = vt[0:W_A, s * KT:(s + 1) * KT].astype(BF16)
        vbt_ref[s] = vt[W_A:, s * KT:(s + 1) * KT].astype(BF16)

    pidx = (_dot(h_hi, wih_ref[...]) + _dot(h_hi, wil_ref[...]) + _dot(h_lo, wih_ref[...]))
    low_half = lax.broadcasted_iota(I32, (x.shape[0], 128), 1) < D_I
    for g in range(H_I // 2):
        pair = pidx[:, g * 128:(g + 1) * 128]
        hi = pair.astype(BF16).astype(F32)
        lo = pair - hi
        hi_sw = pltpu.roll(hi, D_I, 1)
        lo_sw = pltpu.roll(lo, D_I, 1)
        base = 2 * g * 256
        qcat_ref[:, base:base + 128] = jnp.where(low_half, hi, lo_sw).astype(BF16)
        qcat_ref[:, base + 128:base + 256] = jnp.where(low_half, hi, 0.0).astype(BF16)
        qcat_ref[:, base + 256:base + 384] = jnp.where(low_half, hi_sw, lo).astype(BF16)
        qcat_ref[:, base + 384:base + 512] = jnp.where(low_half, hi_sw, 0.0).astype(BF16)
    tail = pidx[:, 256:384]
    hi = tail.astype(BF16).astype(F32)
    lo = tail - hi
    kcat_ref[:, 0:128] = jnp.where(low_half, hi, pltpu.roll(hi, D_I, 1)).astype(BF16)
    kcat_ref[:, 128:256] = jnp.where(low_half, lo, 0.0).astype(BF16)
    wi_ref[...] = tail

    gates_ref[...] = jax.nn.sigmoid(_dot(h_hi, wg_ref[...])).astype(BF16)


def _projection(x2, mod3, norm1_g, w_in, qn_a, kn_a, qn_b, kn_b, S):
    N = x2.shape[0]
    tm = TM_PROJ
    per_batch = S // tm
    w_qk = jnp.concatenate([w_in[:, 0:2 * W_A], w_in[:, 3 * W_A:3 * W_A + 2 * W_B]],
                           axis=1).astype(BF16)
    w_vt = jnp.concatenate([w_in[:, 2 * W_A:3 * W_A], w_in[:, 3 * W_A + 2 * W_B:W_QKV]],
                           axis=1).T.astype(BF16)
    w_idx = jnp.pad(w_in[:, W_QKV:W_QKV + W_IDX], ((0, 0), (0, W_IDX_PAD - W_IDX)))
    w_idx_hi = w_idx.astype(BF16)
    w_idx_lo = (w_idx - w_idx_hi.astype(F32)).astype(BF16)
    w_g = w_in[:, W_QKV + W_IDX:].astype(BF16)
    grp = jnp.asarray(np.kron(np.eye(W_A // D_A), np.ones((D_A, D_A))), BF16)
    gains = jnp.stack([jnp.tile(qn_a, H_A) * (LOG2E * D_A ** -0.5), jnp.tile(kn_a, H_A),
                       jnp.tile(qn_b, 2 * H_B) * (LOG2E * D_B ** -0.5), jnp.tile(kn_b, 2 * H_B)])

    def row_spec(w):
        return pl.BlockSpec((tm, w), lambda t: (t, 0))

    def rows(w, dt):
        return row_spec(w), jax.ShapeDtypeStruct((N, w), dt)

    def transposed(w):
        return (pl.BlockSpec((tm // KT, w, KT), lambda t: (t, 0, 0)),
                jax.ShapeDtypeStruct((N // KT, w, KT), BF16))

    outs = [rows(W_A, BF16), rows(W_A, BF16), rows(W_B, BF16), rows(W_B, BF16),
            transposed(W_A), transposed(W_B),
            rows(H_I * 256, BF16), rows(256, BF16), rows(128, F32), rows(W_G, BF16)]
    return pl.pallas_call(
        _proj_kernel,
        grid=(N // tm,),
        in_specs=[row_spec(D_MODEL),
                  pl.BlockSpec((None, 6, D_MODEL), lambda t: (t // per_batch, 0, 0)),
                  _const_spec((1, D_MODEL)),
                  _const_spec((D_MODEL, 2 * W_A + 2 * W_B)),
                  _const_spec((W_A + W_B, D_MODEL)),
                  _const_spec((D_MODEL, W_IDX_PAD)),
                  _const_spec((D_MODEL, W_IDX_PAD)),
                  _const_spec((D_MODEL, W_G)),
                  _const_spec((W_A, W_A)),
                  _const_spec((4, W_A))],
        out_specs=[o[0] for o in outs],
        out_shape=[o[1] for o in outs],
        compiler_params=pltpu.CompilerParams(dimension_semantics=("parallel",),
                                             vmem_limit_bytes=VMEM_LIMIT),
        name="in_proj",
    )(x2, mod3, norm1_g.reshape(1, D_MODEL), w_qk, w_vt, w_idx_hi, w_idx_lo, w_g, grp, gains)


def _count_ge(ref, n_rows, cand):
    def body(j, acc):
        blk = ref[pl.ds(pl.multiple_of(j * CNT_ROWS, CNT_ROWS), CNT_ROWS), :]
        m = jnp.where(blk >= cand, 1.0, 0.0)
        parts = [m[r * 8:(r + 1) * 8, :] for r in range(CNT_ROWS // 8)]
        while len(parts) > 1:
            parts = [a + b for a, b in zip(parts[0::2], parts[1::2])]
        return acc + parts[0]
    acc = lax.fori_loop(0, n_rows // CNT_ROWS, body, jnp.zeros((8, TQ), F32), unroll=2)
    return jnp.sum(acc, axis=0, keepdims=True)


def _count_ge_bf16(ref, n_rows, cand):
    rows = 16
    one, zero = jnp.ones((), BF16), jnp.zeros((), BF16)

    def body(j, acc):
        blk = ref[pl.ds(pl.multiple_of(j * CNT_ROWS, CNT_ROWS), CNT_ROWS), :]
        m = jnp.where(blk >= cand, one, zero)
        parts = [m[r * rows:(r + 1) * rows, :] for r in range(CNT_ROWS // rows)]
        while len(parts) > 1:
            parts = [a + b for a, b in zip(parts[0::2], parts[1::2])]
        return acc + parts[0]
    assert n_rows // rows <= 256
    acc = lax.fori_loop(0, n_rows // CNT_ROWS, body, jnp.zeros((rows, TQ), BF16), unroll=2)
    return jnp.sum(acc.astype(F32), axis=0, keepdims=True)


def _rank_to_f32(rank):
    key = rank ^ jnp.int32(-2 ** 31)
    return pltpu.bitcast(jnp.where(key < 0, key ^ jnp.int32(0x7FFFFFFF), key), F32)


def _store_bias(bias_ref, bias_t, L):
    bias_ref[0:L, :] = bias_t


def _select_bias(qb, qcat_ref, kcat_ref, wt_ref, score_ref, coarse_ref, eqi_ref, bias_ref):
    L = (qb + 1) * TQ
    admissible = ((lax.broadcasted_iota(I32, (KT, TQ), 0) // CHUNK)
                  <= (lax.broadcasted_iota(I32, (KT, TQ), 1) // CHUNK))
    if L <= TOPK_MAX:
        _store_bias(bias_ref, jnp.where(admissible, 0.0, MASK_VALUE), L)
        return

    for j in range(L // KT):
        rows = slice(j * KT, (j + 1) * KT)
        score = jnp.zeros((KT, TQ), F32)
        for h in range(H_I):
            logits = _dot_nt(kcat_ref[rows, :], qcat_ref[:, h * 256:(h + 1) * 256])
            score = score + wt_ref[h:h + 1, :] * jnp.maximum(logits, 0.0)
        if j == qb:
            score = jnp.where(admissible, score, -jnp.inf)
        score_ref[rows, :] = score
        coarse_ref[rows, :] = score.astype(BF16)
    k_sel = float(TOPK_MAX)

    def coarse_body(it, rank):
        cand = rank | lax.shift_left(jnp.int32(1), 31 - it)
        c = _count_ge_bf16(coarse_ref, L, _rank_to_f32(cand).astype(BF16))
        return jnp.where(c >= k_sel, cand, rank)
    base = lax.fori_loop(0, 16, coarse_body, jnp.zeros((1, TQ), I32)) - jnp.int32(1 << 16)

    def fine_body(it, st):
        off, cnt = st
        cand = off | lax.shift_left(jnp.int32(1), 16 - it)
        c = _count_ge(score_ref, L, _rank_to_f32(base + cand))
        take = c >= k_sel
        return jnp.where(take, cand, off), jnp.where(take, c, cnt)
    off, cnt = lax.fori_loop(
        0, 17, fine_body, (jnp.zeros((1, TQ), I32), jnp.full((1, TQ), float(L), F32)))
    rank = base + off
    thr = _rank_to_f32(rank)
    tied = jnp.max(cnt) > k_sel

    @pl.when(jnp.logical_not(tied))
    def _():
        _store_bias(bias_ref, jnp.where(score_ref[0:L, :] >= thr, 0.0, MASK_VALUE), L)

    @pl.when(tied)
    def _():
        above = _rank_to_f32(rank + 1)
        need = k_sel - _count_ge(score_ref, L, above)
        big = jnp.int32(1 << 30)
        sc = score_ref[0:L, :]
        s_idx = lax.broadcasted_iota(I32, (L, TQ), 0)
        eqi_ref[0:L, :] = jnp.where(sc >= above, big, jnp.where(sc >= thr, s_idx, big))
        n_bits = max(1, int(np.ceil(np.log2(L))))

        def idx_body(it, j):
            cand = j | lax.shift_left(jnp.int32(1), n_bits - 1 - it)
            cnt_lt = float(L) - _count_ge(eqi_ref, L, cand)
            return jnp.where(cnt_lt < need, cand, j)
        j_sel = lax.fori_loop(0, n_bits, idx_body, jnp.zeros((1, TQ), I32))
        sel = jnp.where(score_ref[0:L, :] >= above, 0.0,
                        jnp.where(eqi_ref[0:L, :] <= j_sel, 0.0, MASK_VALUE))
        _store_bias(bias_ref, sel, L)


def _tile_offsets():
    return (lax.broadcasted_iota(I32, (KT, TQ), 1)
            - lax.broadcasted_iota(I32, (KT, TQ), 0)).astype(F32)


def _masked_halves(q_ref, n_groups):
    lane = lax.broadcasted_iota(I32, (TQ, 128), 1)
    low = jnp.where(lane < 64, 1.0, 0.0)
    high = 1.0 - low
    out = []
    for g in range(n_groups):
        q2 = q_ref[:, g * 128:(g + 1) * 128].astype(F32)
        out += [(q2 * low).astype(BF16), (q2 * high).astype(BF16)]
    return out


ONES_ROWS = 16


def _with_ones(vt):
    return jnp.concatenate([vt, jnp.ones((ONES_ROWS, vt.shape[1]), BF16)], axis=0)


def _online_steps(scores, values, maxes, acc_ref):
    ss = [f() for f in scores]
    out = []
    for i, (s, v, m) in enumerate(zip(ss, values, maxes)):
        m_new = jnp.maximum(m, jnp.max(s, axis=0, keepdims=True))
        p = jnp.exp2(s - m_new).astype(BF16)
        acc_ref[i] = jnp.exp2(m - m_new) * acc_ref[i] + _dot(_with_ones(v()), p)
        out.append(m_new)
    return tuple(out)


def _softmax_init(n, acc_ref):
    acc_ref[...] = jnp.zeros(acc_ref.shape, F32)
    return tuple(jnp.full((1, TQ), -jnp.inf, F32) for _ in range(n))


def _softmax_result(acc_ref, i):
    rows = acc_ref.shape[1] - ONES_ROWS
    return acc_ref[i, 0:rows, :] / acc_ref[i, rows:rows + 1, :]


def _key_rows(j):
    return pl.ds(pl.multiple_of(j * KT, KT), KT)


def _split3(x):
    a = x.astype(BF16)
    r = x - a.astype(F32)
    b = r.astype(BF16)
    return a, b, (r - b.astype(F32)).astype(BF16)


def _alibi_tables(S, slopes):
    pos = jnp.arange(S, dtype=I32)
    hi = (pos // 64 * 64).astype(F32)
    lo = (pos % 64).astype(F32)
    lam = jnp.full((S,), LOG2E, F32)
    k_cols = [*_split3(lam), *_split3(lam), *_split3(lam * hi), *_split3(lam * lo)]
    kaug = jnp.pad(jnp.stack(k_cols, axis=1), ((0, 0), (0, 128 - len(k_cols))))
    qaug = []
    for sl in slopes:
        q_cols = [-sl * hi] * 3 + [-sl * lo] * 3 + [jnp.full((S,), sl, F32)] * 6
        qaug.append(jnp.pad(jnp.stack(q_cols, axis=1), ((0, 0), (0, 128 - len(q_cols)))))
    return kaug, jnp.stack(qaug).astype(BF16)


def _dsa_kernel(qb, qcat_ref, kcat_ref, wt_ref, q_ref, k_ref, vt_ref, kaug_ref, qaug_ref, o_ref,
                score_ref, coarse_ref, eqi_ref, bias_ref, acc_ref):
    _select_bias(qb, qcat_ref, kcat_ref, wt_ref, score_ref, coarse_ref, eqi_ref, bias_ref)

    qms = _masked_halves(q_ref, H_A // 2)
    q_ext = [jnp.concatenate([qms[h], qaug_ref[h]], axis=1) for h in range(H_A)]

    def values(j):
        return [lambda h=h: vt_ref[j, h * D_A:(h + 1) * D_A, :] for h in range(H_A)]

    def off_diagonal(j, carry):
        rows = _key_rows(j)
        bias_t = bias_ref[rows, :]
        k_ext = [jnp.concatenate([k_ref[rows, g * 128:(g + 1) * 128], kaug_ref[rows, :]], axis=1)
                 for g in range(H_A // 2)]
        scores = [lambda h=h: _dot_nt(k_ext[h // 2], q_ext[h]) + bias_t for h in range(H_A)]
        return _online_steps(scores, values(j), carry, acc_ref)

    carry = lax.fori_loop(0, qb, off_diagonal, _softmax_init(H_A, acc_ref))

    diag = slice(qb * KT, (qb + 1) * KT)
    bias_t = bias_ref[diag, :]
    dist = jnp.abs(_tile_offsets())
    scores = [lambda h=h: (_dot_nt(k_ref[diag, (h // 2) * 128:(h // 2 + 1) * 128], qms[h])
                           + (bias_t - (LOG2E * SLOPES_A[h]) * dist)) for h in range(H_A)]
    carry = _online_steps(scores, values(qb), carry, acc_ref)
    out_t = jnp.concatenate([_softmax_result(acc_ref, h) for h in range(H_A)], axis=0)
    o_ref[...] = out_t.T.astype(BF16)


def _dsa_block(qb, qcat, kcat, wt, qa, ka, vat, kaug, qaug):
    B, S = qa.shape[0], qa.shape[1]
    L = (qb + 1) * TQ
    return pl.pallas_call(
        functools.partial(_dsa_kernel, qb),
        grid=(B,),
        in_specs=[pl.BlockSpec((None, TQ, H_I * 256), lambda b: (b, qb, 0)),
                  pl.BlockSpec((None, L, 256), lambda b: (b, 0, 0)),
                  pl.BlockSpec((None, H_I, TQ), lambda b: (b, 0, qb)),
                  pl.BlockSpec((None, TQ, W_A), lambda b: (b, qb, 0)),
                  pl.BlockSpec((None, L, W_A), lambda b: (b, 0, 0)),
                  pl.BlockSpec((S // KT, W_A, KT), lambda b: (b, 0, 0)),
                  pl.BlockSpec((L, 128), lambda b: (0, 0)),
                  pl.BlockSpec((H_A, TQ, 128), lambda b: (0, qb, 0))],
        out_specs=pl.BlockSpec((None, TQ, W_A), lambda b: (b, qb, 0)),
        out_shape=jax.ShapeDtypeStruct(qa.shape, BF16),
        input_output_aliases={3: 0},
        scratch_shapes=[pltpu.VMEM((S, TQ), F32), pltpu.VMEM((S, TQ), BF16),
                        pltpu.VMEM((S, TQ), I32), pltpu.VMEM((S, TQ), F32),
                        pltpu.VMEM((H_A, D_A + ONES_ROWS, TQ), F32)],
        compiler_params=pltpu.CompilerParams(dimension_semantics=("parallel",),
                                             vmem_limit_bytes=VMEM_LIMIT),
        name=f"dsa_attn_q{qb}",
    )(qcat, kcat, wt, qa, ka, vat, kaug, qaug)


def _diff_kernel(qb, lam_ref, sg_ref, q_ref, k_ref, vt_ref, kaug_ref, qaug_ref, o_ref, acc_ref):
    lam = (jnp.exp(jnp.sum(lam_ref[0:1, :] * lam_ref[1:2, :], axis=-1, keepdims=True))
           - jnp.exp(jnp.sum(lam_ref[2:3, :] * lam_ref[3:4, :], axis=-1, keepdims=True))
           + LAMBDA_INIT)
    n = 2 * H_B
    qms = _masked_halves(q_ref, H_B)
    q_ext = [jnp.concatenate([qms[i], qaug_ref[i // 2]], axis=1) for i in range(n)]

    def values(j):
        return [lambda i=i: vt_ref[j, (i // 2) * 128:(i // 2 + 1) * 128, :] for i in range(n)]

    def off_diagonal(j, carry):
        rows = _key_rows(j)
        k_ext = [jnp.concatenate([k_ref[rows, h * 128:(h + 1) * 128], kaug_ref[rows, :]], axis=1)
                 for h in range(H_B)]
        scores = [lambda i=i: _dot_nt(k_ext[i // 2], q_ext[i]) for i in range(n)]
        return _online_steps(scores, values(j), carry, acc_ref)

    carry = lax.fori_loop(0, qb, off_diagonal, _softmax_init(n, acc_ref))

    diag = slice(qb * KT, (qb + 1) * KT)
    key_chunk = lax.broadcasted_iota(I32, (KT, TQ), 0) // CHUNK
    query_chunk = lax.broadcasted_iota(I32, (KT, TQ), 1) // CHUNK
    allowed = key_chunk <= query_chunk
    dist = jnp.abs(_tile_offsets())
    biases = [jnp.where(allowed, -(LOG2E * sl) * dist, MASK_VALUE) for sl in SLOPES_B]
    scores = [lambda i=i: (_dot_nt(k_ref[diag, (i // 2) * 128:(i // 2 + 1) * 128], qms[i])
                           + biases[i // 2]) for i in range(n)]
    carry = _online_steps(scores, values(qb), carry, acc_ref)
    ys = []
    for h in range(H_B):
        o = _softmax_result(acc_ref, 2 * h) - lam * _softmax_result(acc_ref, 2 * h + 1)
        ys.append(o * lax.rsqrt(jnp.mean(o * o, axis=0, keepdims=True) + RMS_EPS))
    y = jnp.concatenate(ys, axis=0).T
    o_ref[...] = (y * sg_ref[...] * (1.0 - LAMBDA_INIT)).astype(BF16)


def _diff_block(qb, lam4, subln_g, qbn, kbn, vbt, kaug, qaug):
    B, S = qbn.shape[0], qbn.shape[1]
    L = (qb + 1) * TQ
    return pl.pallas_call(
        functools.partial(_diff_kernel, qb),
        grid=(B,),
        in_specs=[pl.BlockSpec((4, D_B), lambda b: (0, 0)),
                  pl.BlockSpec((1, W_B), lambda b: (0, 0)),
                  pl.BlockSpec((None, TQ, W_B), lambda b: (b, qb, 0)),
                  pl.BlockSpec((None, L, W_B), lambda b: (b, 0, 0)),
                  pl.BlockSpec((S // KT, W_B, KT), lambda b: (b, 0, 0)),
                  pl.BlockSpec((L, 128), lambda b: (0, 0)),
                  pl.BlockSpec((H_B, TQ, 128), lambda b: (0, qb, 0))],
        out_specs=pl.BlockSpec((None, TQ, W_B), lambda b: (b, qb, 0)),
        out_shape=jax.ShapeDtypeStruct(qbn.shape, BF16),
        input_output_aliases={2: 0},
        scratch_shapes=[pltpu.VMEM((2 * H_B, 2 * D_B + ONES_ROWS, TQ), F32)],
        compiler_params=pltpu.CompilerParams(dimension_semantics=("parallel",),
                                             vmem_limit_bytes=VMEM_LIMIT),
        name=f"diff_attn_q{qb}",
    )(lam4, jnp.tile(subln_g, H_B).reshape(1, W_B), qbn, kbn, vbt, kaug, qaug)


def _ffn_kernel(x_ref, mod_ref, ya_ref, yb_ref, gates_ref, wua_ref, wub_ref, wo_ref, g2_ref,
                w1_ref, w3_ref, w2_ref, o_ref):
    gates = gates_ref[...].astype(F32)
    merged = (gates[:, :D_MODEL] * _dot(ya_ref[...], wua_ref[...])
              + gates[:, D_MODEL:] * _dot(yb_ref[...], wub_ref[...]))
    x1 = x_ref[...] + mod_ref[2:3, :] * _dot(merged.astype(BF16), wo_ref[...])
    ms = jnp.mean(x1 * x1, axis=-1, keepdims=True)
    h2 = x1 * lax.rsqrt(ms + RMS_EPS) * g2_ref[...]
    h2 = (h2 * (1.0 + mod_ref[4:5, :]) + mod_ref[3:4, :]).astype(BF16)
    u = _dot(h2, w1_ref[...])
    act = (u * jax.nn.sigmoid(u) * _dot(h2, w3_ref[...])).astype(BF16)
    o_ref[...] = x1 + mod_ref[5:6, :] * _dot(act, w2_ref[...])


def _merge_ffn(x2, mod3, ya, yb, gates, w_up_a, w_up_b, w_o, norm2_g, w_ff1, w_ff3, w_ff2, S):
    N = x2.shape[0]
    tm = TM_FFN
    per_batch = S // tm

    def row_spec(w):
        return pl.BlockSpec((tm, w), lambda t: (t, 0))

    return pl.pallas_call(
        _ffn_kernel,
        grid=(N // tm,),
        in_specs=[row_spec(D_MODEL),
                  pl.BlockSpec((None, 6, D_MODEL), lambda t: (t // per_batch, 0, 0)),
                  row_spec(W_A), row_spec(W_B), row_spec(W_G),
                  _const_spec((W_A, D_MODEL)), _const_spec((W_B, D_MODEL)),
                  _const_spec((D_MODEL, D_MODEL)), _const_spec((1, D_MODEL)),
                  _const_spec((D_MODEL, D_FF)), _const_spec((D_MODEL, D_FF)),
                  _const_spec((D_FF, D_MODEL))],
        out_specs=row_spec(D_MODEL),
        out_shape=jax.ShapeDtypeStruct((N, D_MODEL), F32),
        compiler_params=pltpu.CompilerParams(dimension_semantics=("parallel",),
                                             vmem_limit_bytes=VMEM_LIMIT),
        name="merge_ffn",
    )(x2, mod3, ya, yb, gates, w_up_a.astype(BF16), w_up_b.astype(BF16), w_o.astype(BF16),
      norm2_g.reshape(1, D_MODEL), w_ff1.astype(BF16), w_ff3.astype(BF16), w_ff2.astype(BF16))


def kernel(x, c, w_ada, b_ada, norm1_g, w_in, qn_a, kn_a, qn_b, kn_b, lam_q1, lam_k1, lam_q2,
           lam_k2, subln_g, w_up_a, w_up_b, w_o, norm2_g, w_ff1, w_ff3, w_ff2):
    B, S, D = x.shape
    assert D == D_MODEL and S % TQ == 0 and w_ada.shape[0] == 1
    N = B * S
    x2 = x.reshape(N, D)
    mod3 = _modulation(c, w_ada[0], b_ada[0]).reshape(B, 6, D)

    (qa, ka, qbn, kbn, vat, vbt, qcat, kcat, wi, gates) = _projection(
        x2, mod3, norm1_g[0], w_in[0], qn_a[0], kn_a[0], qn_b[0], kn_b[0], S)
    r3 = lambda a: a.reshape(B, S, a.shape[-1])
    qa, ka, qbn, kbn, qcat, kcat = map(r3, (qa, ka, qbn, kbn, qcat, kcat))
    wt = wi[:, D_I:D_I + H_I].reshape(B, S, H_I).transpose(0, 2, 1)
    lam4 = jnp.stack([lam_q1[0], lam_k1[0], lam_q2[0], lam_k2[0]])

    kaug, qaug_a = _alibi_tables(S, SLOPES_A)
    _, qaug_b = _alibi_tables(S, SLOPES_B)
    ya, yb = qa, qbn
    for qb in range(S // TQ):
        ya = _dsa_block(qb, qcat, kcat, wt, ya, ka, vat, kaug, qaug_a)
        yb = _diff_block(qb, lam4, subln_g[0], yb, kbn, vbt, kaug, qaug_b)
    ya = ya.reshape(N, W_A)
    yb = yb.reshape(N, W_B)

    out = _merge_ffn(x2, mod3, ya, yb, gates, w_up_a[0], w_up_b[0], w_o[0], norm2_g[0],
                     w_ff1[0], w_ff3[0], w_ff2[0], S)
    return out.reshape(B, S, D)
```

```python
import functools

import numpy as np
import jax
import jax.numpy as jnp
from jax import lax
from jax.experimental import pallas as pl
from jax.experimental.pallas import tpu as pltpu

F32 = jnp.float32
BF16 = jnp.bfloat16
I32 = jnp.int32

D_MODEL = 1024
CHUNK = 64
H_A, D_A = 8, 64
H_I, D_I = 4, 64
TOPK_MAX = 256
H_B, D_B = 4, 64
D_FF = 2816
RMS_EPS = 1e-6
MASK_VALUE = -1e30
LAMBDA_INIT = 0.2
LOG2E = 1.4426950408889634
SLOPES_A = [2.0 ** (-8.0 * (h + 1) / H_A) for h in range(H_A)]
SLOPES_B = [2.0 ** (-8.0 * (h + 1) / H_B) for h in range(H_B)]

W_A = H_A * D_A
W_B = H_B * 2 * D_B
W_QKV = 3 * W_A + 3 * W_B
W_IDX = H_I * D_I + D_I + H_I
W_IDX_PAD = 384
W_G = 2 * D_MODEL

TQ = 256
KT = TQ
TM_PROJ = 2 * KT
TM_FFN = 512
CNT_ROWS = 256
CNT_ACCS = 4
STAGE_GROUP = 8
VMEM_LIMIT = 56 * 1024 * 1024

NT_DIMS = (((1,), (1,)), ((), ()))


def _dot(a, b):
    return jnp.dot(a, b, preferred_element_type=F32)


def _dot_nt(a, b):
    return lax.dot_general(a, b, NT_DIMS, preferred_element_type=F32)


def _split(x):
    hi = x.astype(BF16)
    lo = (x - hi.astype(F32)).astype(BF16)
    return hi, lo


def _dot3(a, b_hi, b_lo):
    a_hi, a_lo = _split(a)
    return _dot(a_hi, b_hi) + _dot(a_hi, b_lo) + _dot(a_lo, b_hi)


def _const_spec(shape):
    nd = len(shape)
    return pl.BlockSpec(shape, lambda *_: (0,) * nd, pipeline_mode=pl.Buffered(1))


def _mod_kernel(c_ref, w_ref, b_ref, o_ref):
    c = c_ref[...]
    cs = c * jax.nn.sigmoid(c)
    w_hi, w_lo = _split(w_ref[...])
    o_ref[...] = _dot3(cs, w_hi, w_lo) + b_ref[...]


def _modulation(c, w_ada, b_ada):
    B = c.shape[0]
    n = w_ada.shape[1]
    tn = 1024
    return pl.pallas_call(
        _mod_kernel,
        grid=(n // tn,),
        in_specs=[pl.BlockSpec((B, D_MODEL), lambda j: (0, 0)),
                  pl.BlockSpec((D_MODEL, tn), lambda j: (0, j)),
                  pl.BlockSpec((1, tn), lambda j: (0, j))],
        out_specs=pl.BlockSpec((B, tn), lambda j: (0, j)),
        out_shape=jax.ShapeDtypeStruct((B, n), F32),
        compiler_params=pltpu.CompilerParams(vmem_limit_bytes=VMEM_LIMIT),
        name="adaln_mod",
    )(c, w_ada, b_ada.reshape(1, n))


def _proj_kernel(x_ref, mod_ref, g1_ref, wqk_ref, wvt_ref, wih_ref, wil_ref, wg_ref, grp_ref,
                 gains_ref,
                 qa_ref, ka_ref, qb_ref, kb_ref, vat_ref, vbt_ref, qcat_ref, kcat_ref, wi_ref,
                 gates_ref):
    x = x_ref[...]
    ms = jnp.mean(x * x, axis=-1, keepdims=True)
    h = x * lax.rsqrt(ms + RMS_EPS) * g1_ref[...]
    h = h * (1.0 + mod_ref[1:2, :]) + mod_ref[0:1, :]
    h_hi, h_lo = _split(h)

    grp = grp_ref[...]

    def head_norm(z, gain):
        m = _dot((z * z).astype(BF16), grp) * (1.0 / D_A)
        return (z * lax.rsqrt(m + RMS_EPS) * gain).astype(BF16)

    qk = _dot(h_hi, wqk_ref[...])
    qa_ref[...] = head_norm(qk[:, 0:W_A], gains_ref[0:1, :])
    ka_ref[...] = head_norm(qk[:, W_A:2 * W_A], gains_ref[1:2, :])
    qb_ref[...] = head_norm(qk[:, 2 * W_A:2 * W_A + W_B], gains_ref[2:3, :])
    kb_ref[...] = head_norm(qk[:, 2 * W_A + W_B:], gains_ref[3:4, :])
    vt = _dot_nt(wvt_ref[...], h_hi)
    for s in range(x.shape[0] // KT):
        vat_ref[s] = vt[0:W_A, s * KT:(s + 1) * KT].astype(BF16)
        vbt_ref[s] = vt[W_A:, s * KT:(s + 1) * KT].astype(BF16)

    pidx = (_dot(h_hi, wih_ref[...]) + _dot(h_hi, wil_ref[...]) + _dot(h_lo, wih_ref[...]))
    low_half = lax.broadcasted_iota(I32, (x.shape[0], 128), 1) < D_I
    for g in range(H_I // 2):
        pair = pidx[:, g * 128:(g + 1) * 128]
        hi = pair.astype(BF16).astype(F32)
        lo = pair - hi
        hi_sw = pltpu.roll(hi, D_I, 1)
        lo_sw = pltpu.roll(lo, D_I, 1)
        base = 2 * g * 256
        qcat_ref[:, base:base + 128] = jnp.where(low_half, hi, lo_sw).astype(BF16)
        qcat_ref[:, base + 128:base + 256] = jnp.where(low_half, hi, 0.0).astype(BF16)
        qcat_ref[:, base + 256:base + 384] = jnp.where(low_half, hi_sw, lo).astype(BF16)
        qcat_ref[:, base + 384:base + 512] = jnp.where(low_half, hi_sw, 0.0).astype(BF16)
    tail = pidx[:, 256:384]
    hi = tail.astype(BF16).astype(F32)
    lo = tail - hi
    kcat_ref[:, 0:128] = jnp.where(low_half, hi, pltpu.roll(hi, D_I, 1)).astype(BF16)
    kcat_ref[:, 128:256] = jnp.where(low_half, lo, 0.0).astype(BF16)
    wi_ref[...] = tail

    gates_ref[...] = jax.nn.sigmoid(_dot(h_hi, wg_ref[...])).astype(BF16)


def _projection(x2, mod3, norm1_g, w_in, qn_a, kn_a, qn_b, kn_b, S):
    N = x2.shape[0]
    tm = TM_PROJ
    per_batch = S // tm
    w_qk = jnp.concatenate([w_in[:, 0:2 * W_A], w_in[:, 3 * W_A:3 * W_A + 2 * W_B]],
                           axis=1).astype(BF16)
    w_vt = jnp.concatenate([w_in[:, 2 * W_A:3 * W_A], w_in[:, 3 * W_A + 2 * W_B:W_QKV]],
                           axis=1).T.astype(BF16)
    w_idx = jnp.pad(w_in[:, W_QKV:W_QKV + W_IDX], ((0, 0), (0, W_IDX_PAD - W_IDX)))
    w_idx_hi = w_idx.astype(BF16)
    w_idx_lo = (w_idx - w_idx_hi.astype(F32)).astype(BF16)
    w_g = w_in[:, W_QKV + W_IDX:].astype(BF16)
    grp = jnp.asarray(np.kron(np.eye(W_A // D_A), np.ones((D_A, D_A))), BF16)
    gains = jnp.stack([jnp.tile(qn_a, H_A) * (LOG2E * D_A ** -0.5), jnp.tile(kn_a, H_A),
                       jnp.tile(qn_b, 2 * H_B) * (LOG2E * D_B ** -0.5), jnp.tile(kn_b, 2 * H_B)])

    def row_spec(w):
        return pl.BlockSpec((tm, w), lambda t: (t, 0))

    def rows(w, dt):
        return row_spec(w), jax.ShapeDtypeStruct((N, w), dt)

    def transposed(w):
        return (pl.BlockSpec((tm // KT, w, KT), lambda t: (t, 0, 0)),
                jax.ShapeDtypeStruct((N // KT, w, KT), BF16))

    outs = [rows(W_A, BF16), rows(W_A, BF16), rows(W_B, BF16), rows(W_B, BF16),
            transposed(W_A), transposed(W_B),
            rows(H_I * 256, BF16), rows(256, BF16), rows(128, F32), rows(W_G, BF16)]
    return pl.pallas_call(
        _proj_kernel,
        grid=(N // tm,),
        in_specs=[row_spec(D_MODEL),
                  pl.BlockSpec((None, 6, D_MODEL), lambda t: (t // per_batch, 0, 0)),
                  _const_spec((1, D_MODEL)),
                  _const_spec((D_MODEL, 2 * W_A + 2 * W_B)),
                  _const_spec((W_A + W_B, D_MODEL)),
                  _const_spec((D_MODEL, W_IDX_PAD)),
                  _const_spec((D_MODEL, W_IDX_PAD)),
                  _const_spec((D_MODEL, W_G)),
                  _const_spec((W_A, W_A)),
                  _const_spec((4, W_A))],
        out_specs=[o[0] for o in outs],
        out_shape=[o[1] for o in outs],
        compiler_params=pltpu.CompilerParams(dimension_semantics=("parallel",),
                                             vmem_limit_bytes=VMEM_LIMIT),
        name="in_proj",
    )(x2, mod3, norm1_g.reshape(1, D_MODEL), w_qk, w_vt, w_idx_hi, w_idx_lo, w_g, grp, gains)


def _count_ge(ref, n_rows, cand):
    accs = [jnp.zeros((8, TQ), F32) for _ in range(CNT_ACCS)]
    for r in range(n_rows // 8):
        blk = ref[r * 8:(r + 1) * 8, :]
        accs[r % CNT_ACCS] = accs[r % CNT_ACCS] + jnp.where(blk >= cand, 1.0, 0.0)
    return jnp.sum(functools.reduce(lambda a, b: a + b, accs), axis=0, keepdims=True)


def _count_ge_bf16(ref, n_rows, cand):
    rows = 16
    one, zero = jnp.ones((), BF16), jnp.zeros((), BF16)

    def body(j, acc):
        blk = ref[pl.ds(pl.multiple_of(j * CNT_ROWS, CNT_ROWS), CNT_ROWS), :]
        m = jnp.where(blk >= cand, one, zero)
        parts = [m[r * rows:(r + 1) * rows, :] for r in range(CNT_ROWS // rows)]
        while len(parts) > 1:
            parts = [a + b for a, b in zip(parts[0::2], parts[1::2])]
        return acc + parts[0]
    assert n_rows // rows <= 256
    acc = lax.fori_loop(0, n_rows // CNT_ROWS, body, jnp.zeros((rows, TQ), BF16), unroll=True)
    return jnp.sum(acc.astype(F32), axis=0, keepdims=True)


def _rank_to_f32(rank):
    key = rank ^ jnp.int32(-2 ** 31)
    return pltpu.bitcast(jnp.where(key < 0, key ^ jnp.int32(0x7FFFFFFF), key), F32)


def _store_bias(bias_ref, bias_t, L):
    bias_ref[0:L, :] = bias_t


def _select_bias(qb, qcat_ref, kcat_ref, wt_ref, score_ref, coarse_ref, eqi_ref, bias_ref):
    L = (qb + 1) * TQ
    admissible = ((lax.broadcasted_iota(I32, (KT, TQ), 0) // CHUNK)
                  <= (lax.broadcasted_iota(I32, (KT, TQ), 1) // CHUNK))
    if L <= TOPK_MAX:
        _store_bias(bias_ref, jnp.where(admissible, 0.0, MASK_VALUE), L)
        return

    for j in range(L // KT):
        rows = slice(j * KT, (j + 1) * KT)
        score = jnp.zeros((KT, TQ), F32)
        for h in range(H_I):
            logits = _dot_nt(kcat_ref[rows, :], qcat_ref[:, h * 256:(h + 1) * 256])
            score = score + wt_ref[h:h + 1, :] * jnp.maximum(logits, 0.0)
        if j == qb:
            score = jnp.where(admissible, score, -jnp.inf)
        score_ref[rows, :] = score
        coarse_ref[rows, :] = score.astype(BF16)
    k_sel = float(TOPK_MAX)

    def coarse_body(it, rank):
        cand = rank | lax.shift_left(jnp.int32(1), 31 - it)
        c = _count_ge_bf16(coarse_ref, L, _rank_to_f32(cand).astype(BF16))
        return jnp.where(c >= k_sel, cand, rank)
    base = lax.fori_loop(0, 16, coarse_body, jnp.zeros((1, TQ), I32)) - jnp.int32(1 << 16)

    def fine_body(it, st):
        off, cnt = st
        cand = off | lax.shift_left(jnp.int32(1), 16 - it)
        c = _count_ge(score_ref, L, _rank_to_f32(base + cand))
        take = c >= k_sel
        return jnp.where(take, cand, off), jnp.where(take, c, cnt)
    off, cnt = lax.fori_loop(
        0, 17, fine_body, (jnp.zeros((1, TQ), I32), jnp.full((1, TQ), float(L), F32)))
    rank = base + off
    thr = _rank_to_f32(rank)
    tied = jnp.max(cnt) > k_sel

    @pl.when(jnp.logical_not(tied))
    def _():
        _store_bias(bias_ref, jnp.where(score_ref[0:L, :] >= thr, 0.0, MASK_VALUE), L)

    @pl.when(tied)
    def _():
        above = _rank_to_f32(rank + 1)
        need = k_sel - _count_ge(score_ref, L, above)
        big = jnp.int32(1 << 30)
        sc = score_ref[0:L, :]
        s_idx = lax.broadcasted_iota(I32, (L, TQ), 0)
        eqi_ref[0:L, :] = jnp.where(sc >= above, big, jnp.where(sc >= thr, s_idx, big))
        n_bits = max(1, int(np.ceil(np.log2(L))))

        def idx_body(it, j):
            cand = j | lax.shift_left(jnp.int32(1), n_bits - 1 - it)
            cnt_lt = float(L) - _count_ge(eqi_ref, L, cand)
            return jnp.where(cnt_lt < need, cand, j)
        j_sel = lax.fori_loop(0, n_bits, idx_body, jnp.zeros((1, TQ), I32))
        sel = jnp.where(score_ref[0:L, :] >= above, 0.0,
                        jnp.where(eqi_ref[0:L, :] <= j_sel, 0.0, MASK_VALUE))
        _store_bias(bias_ref, sel, L)


def _tile_offsets():
    return (lax.broadcasted_iota(I32, (KT, TQ), 1)
            - lax.broadcasted_iota(I32, (KT, TQ), 0)).astype(F32)


def _masked_halves(q_ref, n_groups):
    lane = lax.broadcasted_iota(I32, (TQ, 128), 1)
    low = jnp.where(lane < 64, 1.0, 0.0)
    high = 1.0 - low
    out = []
    for g in range(n_groups):
        q2 = q_ref[:, g * 128:(g + 1) * 128].astype(F32)
        out += [(q2 * low).astype(BF16), (q2 * high).astype(BF16)]
    return out


ONES_ROWS = 16


def _with_ones(vt):
    return jnp.concatenate([vt, jnp.ones((ONES_ROWS, vt.shape[1]), BF16)], axis=0)


def _online_steps(scores, values, maxes, acc_ref):
    ss = [f() for f in scores]
    out = []
    for i, (s, v, m) in enumerate(zip(ss, values, maxes)):
        m_new = jnp.maximum(m, jnp.max(s, axis=0, keepdims=True))
        p = jnp.exp2(s - m_new).astype(BF16)
        acc_ref[i] = jnp.exp2(m - m_new) * acc_ref[i] + _dot(_with_ones(v()), p)
        out.append(m_new)
    return tuple(out)


def _softmax_init(n, acc_ref):
    acc_ref[...] = jnp.zeros(acc_ref.shape, F32)
    return tuple(jnp.full((1, TQ), -jnp.inf, F32) for _ in range(n))


def _softmax_result(acc_ref, i):
    rows = acc_ref.shape[1] - ONES_ROWS
    return acc_ref[i, 0:rows, :] / acc_ref[i, rows:rows + 1, :]


def _key_rows(j):
    return pl.ds(pl.multiple_of(j * KT, KT), KT)


def _split3(x):
    a = x.astype(BF16)
    r = x - a.astype(F32)
    b = r.astype(BF16)
    return a, b, (r - b.astype(F32)).astype(BF16)


def _alibi_tables(S, slopes):
    pos = jnp.arange(S, dtype=I32)
    hi = (pos // 64 * 64).astype(F32)
    lo = (pos % 64).astype(F32)
    lam = jnp.full((S,), LOG2E, F32)
    k_cols = [*_split3(lam), *_split3(lam), *_split3(lam * hi), *_split3(lam * lo)]
    kaug = jnp.pad(jnp.stack(k_cols, axis=1), ((0, 0), (0, 128 - len(k_cols))))
    qaug = []
    for sl in slopes:
        q_cols = [-sl * hi] * 3 + [-sl * lo] * 3 + [jnp.full((S,), sl, F32)] * 6
        qaug.append(jnp.pad(jnp.stack(q_cols, axis=1), ((0, 0), (0, 128 - len(q_cols)))))
    return kaug, jnp.stack(qaug).astype(BF16)


def _dsa_kernel(qb, qcat_ref, kcat_ref, wt_ref, q_ref, k_ref, vt_ref, kaug_ref, qaug_ref, o_ref,
                score_ref, coarse_ref, eqi_ref, bias_ref, acc_ref):
    _select_bias(qb, qcat_ref, kcat_ref, wt_ref, score_ref, coarse_ref, eqi_ref, bias_ref)

    qms = _masked_halves(q_ref, H_A // 2)
    q_ext = [jnp.concatenate([qms[h], qaug_ref[h]], axis=1) for h in range(H_A)]

    def values(j):
        return [lambda h=h: vt_ref[j, h * D_A:(h + 1) * D_A, :] for h in range(H_A)]

    def off_diagonal(j, carry):
        rows = _key_rows(j)
        bias_t = bias_ref[rows, :]
        k_ext = [jnp.concatenate([k_ref[rows, g * 128:(g + 1) * 128], kaug_ref[rows, :]], axis=1)
                 for g in range(H_A // 2)]
        scores = [lambda h=h: _dot_nt(k_ext[h // 2], q_ext[h]) + bias_t for h in range(H_A)]
        return _online_steps(scores, values(j), carry, acc_ref)

    carry = lax.fori_loop(0, qb, off_diagonal, _softmax_init(H_A, acc_ref), unroll=2)

    diag = slice(qb * KT, (qb + 1) * KT)
    bias_t = bias_ref[diag, :]
    dist = jnp.abs(_tile_offsets())
    scores = [lambda h=h: (_dot_nt(k_ref[diag, (h // 2) * 128:(h // 2 + 1) * 128], qms[h])
                           + (bias_t - (LOG2E * SLOPES_A[h]) * dist)) for h in range(H_A)]
    carry = _online_steps(scores, values(qb), carry, acc_ref)
    out_t = jnp.concatenate([_softmax_result(acc_ref, h) for h in range(H_A)], axis=0)
    o_ref[...] = out_t.T.astype(BF16)


def _dsa_block(qb, qcat, kcat, wt, qa, ka, vat, kaug, qaug):
    B, S = qa.shape[0], qa.shape[1]
    L = (qb + 1) * TQ
    return pl.pallas_call(
        functools.partial(_dsa_kernel, qb),
        grid=(B,),
        in_specs=[pl.BlockSpec((None, TQ, H_I * 256), lambda b: (b, qb, 0)),
                  pl.BlockSpec((None, L, 256), lambda b: (b, 0, 0)),
                  pl.BlockSpec((None, H_I, TQ), lambda b: (b, 0, qb)),
                  pl.BlockSpec((None, TQ, W_A), lambda b: (b, qb, 0)),
                  pl.BlockSpec((None, L, W_A), lambda b: (b, 0, 0)),
                  pl.BlockSpec((S // KT, W_A, KT), lambda b: (b, 0, 0)),
                  pl.BlockSpec((L, 128), lambda b: (0, 0)),
                  pl.BlockSpec((H_A, TQ, 128), lambda b: (0, qb, 0))],
        out_specs=pl.BlockSpec((None, TQ, W_A), lambda b: (b, qb, 0)),
        out_shape=jax.ShapeDtypeStruct(qa.shape, BF16),
        input_output_aliases={3: 0},
        scratch_shapes=[pltpu.VMEM((S, TQ), F32), pltpu.VMEM((S, TQ), BF16),
                        pltpu.VMEM((S, TQ), I32), pltpu.VMEM((S, TQ), F32),
                        pltpu.VMEM((H_A, D_A + ONES_ROWS, TQ), F32)],
        compiler_params=pltpu.CompilerParams(dimension_semantics=("parallel",),
                                             vmem_limit_bytes=VMEM_LIMIT),
        name=f"dsa_attn_q{qb}",
    )(qcat, kcat, wt, qa, ka, vat, kaug, qaug)


def _diff_kernel(qb, lam_ref, sg_ref, q_ref, k_ref, vt_ref, kaug_ref, qaug_ref, o_ref, acc_ref):
    lam = (jnp.exp(jnp.sum(lam_ref[0:1, :] * lam_ref[1:2, :], axis=-1, keepdims=True))
           - jnp.exp(jnp.sum(lam_ref[2:3, :] * lam_ref[3:4, :], axis=-1, keepdims=True))
           + LAMBDA_INIT)
    n = 2 * H_B
    qms = _masked_halves(q_ref, H_B)
    q_ext = [jnp.concatenate([qms[i], qaug_ref[i // 2]], axis=1) for i in range(n)]

    def values(j):
        return [lambda i=i: vt_ref[j, (i // 2) * 128:(i // 2 + 1) * 128, :] for i in range(n)]

    def off_diagonal(j, carry):
        rows = _key_rows(j)
        k_ext = [jnp.concatenate([k_ref[rows, h * 128:(h + 1) * 128], kaug_ref[rows, :]], axis=1)
                 for h in range(H_B)]
        scores = [lambda i=i: _dot_nt(k_ext[i // 2], q_ext[i]) for i in range(n)]
        return _online_steps(scores, values(j), carry, acc_ref)

    carry = lax.fori_loop(0, qb, off_diagonal, _softmax_init(n, acc_ref), unroll=2)

    diag = slice(qb * KT, (qb + 1) * KT)
    key_chunk = lax.broadcasted_iota(I32, (KT, TQ), 0) // CHUNK
    query_chunk = lax.broadcasted_iota(I32, (KT, TQ), 1) // CHUNK
    allowed = key_chunk <= query_chunk
    dist = jnp.abs(_tile_offsets())
    biases = [jnp.where(allowed, -(LOG2E * sl) * dist, MASK_VALUE) for sl in SLOPES_B]
    scores = [lambda i=i: (_dot_nt(k_ref[diag, (i // 2) * 128:(i // 2 + 1) * 128], qms[i])
                           + biases[i // 2]) for i in range(n)]
    carry = _online_steps(scores, values(qb), carry, acc_ref)
    ys = []
    for h in range(H_B):
        o = _softmax_result(acc_ref, 2 * h) - lam * _softmax_result(acc_ref, 2 * h + 1)
        ys.append(o * lax.rsqrt(jnp.mean(o * o, axis=0, keepdims=True) + RMS_EPS))
    y = jnp.concatenate(ys, axis=0).T
    o_ref[...] = (y * sg_ref[...] * (1.0 - LAMBDA_INIT)).astype(BF16)


def _diff_block(qb, lam4, subln_g, qbn, kbn, vbt, kaug, qaug):
    B, S = qbn.shape[0], qbn.shape[1]
    L = (qb + 1) * TQ
    return pl.pallas_call(
        functools.partial(_diff_kernel, qb),
        grid=(B,),
        in_specs=[pl.BlockSpec((4, D_B), lambda b: (0, 0)),
                  pl.BlockSpec((1, W_B), lambda b: (0, 0)),
                  pl.BlockSpec((None, TQ, W_B), lambda b: (b, qb, 0)),
                  pl.BlockSpec((None, L, W_B), lambda b: (b, 0, 0)),
                  pl.BlockSpec((S // KT, W_B, KT), lambda b: (b, 0, 0)),
                  pl.BlockSpec((L, 128), lambda b: (0, 0)),
                  pl.BlockSpec((H_B, TQ, 128), lambda b: (0, qb, 0))],
        out_specs=pl.BlockSpec((None, TQ, W_B), lambda b: (b, qb, 0)),
        out_shape=jax.ShapeDtypeStruct(qbn.shape, BF16),
        input_output_aliases={2: 0},
        scratch_shapes=[pltpu.VMEM((2 * H_B, 2 * D_B + ONES_ROWS, TQ), F32)],
        compiler_params=pltpu.CompilerParams(dimension_semantics=("parallel",),
                                             vmem_limit_bytes=VMEM_LIMIT),
        name=f"diff_attn_q{qb}",
    )(lam4, jnp.tile(subln_g, H_B).reshape(1, W_B), qbn, kbn, vbt, kaug, qaug)


def _ffn_kernel(x_ref, mod_ref, ya_ref, yb_ref, gates_ref, wua_ref, wub_ref, wo_ref, g2_ref,
                w1_ref, w3_ref, w2_ref, o_ref):
    gates = gates_ref[...].astype(F32)
    merged = (gates[:, :D_MODEL] * _dot(ya_ref[...], wua_ref[...])
              + gates[:, D_MODEL:] * _dot(yb_ref[...], wub_ref[...]))
    x1 = x_ref[...] + mod_ref[2:3, :] * _dot(merged.astype(BF16), wo_ref[...])
    ms = jnp.mean(x1 * x1, axis=-1, keepdims=True)
    h2 = x1 * lax.rsqrt(ms + RMS_EPS) * g2_ref[...]
    h2 = (h2 * (1.0 + mod_ref[4:5, :]) + mod_ref[3:4, :]).astype(BF16)
    u = _dot(h2, w1_ref[...])
    act = (u * jax.nn.sigmoid(u) * _dot(h2, w3_ref[...])).astype(BF16)
    o_ref[...] = x1 + mod_ref[5:6, :] * _dot(act, w2_ref[...])


def _merge_ffn(x2, mod3, ya, yb, gates, w_up_a, w_up_b, w_o, norm2_g, w_ff1, w_ff3, w_ff2, S):
    N = x2.shape[0]
    tm = TM_FFN
    per_batch = S // tm

    def row_spec(w):
        return pl.BlockSpec((tm, w), lambda t: (t, 0))

    return pl.pallas_call(
        _ffn_kernel,
        grid=(N // tm,),
        in_specs=[row_spec(D_MODEL),
                  pl.BlockSpec((None, 6, D_MODEL), lambda t: (t // per_batch, 0, 0)),
                  row_spec(W_A), row_spec(W_B), row_spec(W_G),
                  _const_spec((W_A, D_MODEL)), _const_spec((W_B, D_MODEL)),
                  _const_spec((D_MODEL, D_MODEL)), _const_spec((1, D_MODEL)),
                  _const_spec((D_MODEL, D_FF)), _const_spec((D_MODEL, D_FF)),
                  _const_spec((D_FF, D_MODEL))],
        out_specs=row_spec(D_MODEL),
        out_shape=jax.ShapeDtypeStruct((N, D_MODEL), F32),
        compiler_params=pltpu.CompilerParams(dimension_semantics=("parallel",),
                                             vmem_limit_bytes=VMEM_LIMIT),
        name="merge_ffn",
    )(x2, mod3, ya, yb, gates, w_up_a.astype(BF16), w_up_b.astype(BF16), w_o.astype(BF16),
      norm2_g.reshape(1, D_MODEL), w_ff1.astype(BF16), w_ff3.astype(BF16), w_ff2.astype(BF16))


def kernel(x, c, w_ada, b_ada, norm1_g, w_in, qn_a, kn_a, qn_b, kn_b, lam_q1, lam_k1, lam_q2,
           lam_k2, subln_g, w_up_a, w_up_b, w_o, norm2_g, w_ff1, w_ff3, w_ff2):
    B, S, D = x.shape
    assert D == D_MODEL and S % TQ == 0 and w_ada.shape[0] == 1
    N = B * S
    x2 = x.reshape(N, D)
    mod3 = _modulation(c, w_ada[0], b_ada[0]).reshape(B, 6, D)

    (qa, ka, qbn, kbn, vat, vbt, qcat, kcat, wi, gates) = _projection(
        x2, mod3, norm1_g[0], w_in[0], qn_a[0], kn_a[0], qn_b[0], kn_b[0], S)
    r3 = lambda a: a.reshape(B, S, a.shape[-1])
    qa, ka, qbn, kbn, qcat, kcat = map(r3, (qa, ka, qbn, kbn, qcat, kcat))
    wt = wi[:, D_I:D_I + H_I].reshape(B, S, H_I).transpose(0, 2, 1)
    lam4 = jnp.stack([lam_q1[0], lam_k1[0], lam_q2[0], lam_k2[0]])

    kaug, qaug_a = _alibi_tables(S, SLOPES_A)
    _, qaug_b = _alibi_tables(S, SLOPES_B)
    ya, yb = qa, qbn
    for qb in range(S // TQ):
        ya = _dsa_block(qb, qcat, kcat, wt, ya, ka, vat, kaug, qaug_a)
        yb = _diff_block(qb, lam4, subln_g[0], yb, kbn, vbt, kaug, qaug_b)
    ya = ya.reshape(N, W_A)
    yb = yb.reshape(N, W_B)

    out = _merge_ffn(x2, mod3, ya, yb, gates, w_up_a[0], w_up_b[0], w_o[0], norm2_g[0],
                     w_ff1[0], w_ff3[0], w_ff2[0], S)
    return out.reshape(B, S, D)
```

```python
import functools

import numpy as np
import jax
import jax.numpy as jnp
from jax import lax
from jax.experimental import pallas as pl
from jax.experimental.pallas import tpu as pltpu

F32 = jnp.float32
BF16 = jnp.bfloat16
I32 = jnp.int32

D_MODEL = 1024
CHUNK = 64
H_A, D_A = 8, 64
H_I, D_I = 4, 64
TOPK_MAX = 256
H_B, D_B = 4, 64
D_FF = 2816
RMS_EPS = 1e-6
MASK_VALUE = -1e30
LAMBDA_INIT = 0.2
LOG2E = 1.4426950408889634
SLOPES_A = [2.0 ** (-8.0 * (h + 1) / H_A) for h in range(H_A)]
SLOPES_B = [2.0 ** (-8.0 * (h + 1) / H_B) for h in range(H_B)]

W_A = H_A * D_A
W_B = H_B * 2 * D_B
W_QKV = 3 * W_A + 3 * W_B
W_IDX = H_I * D_I + D_I + H_I
W_IDX_PAD = 384
W_G = 2 * D_MODEL

TQ = 256
KT = TQ
TM_PROJ = 2 * KT
TM_FFN = 512
CNT_ROWS = 256
CNT_ACCS = 4
ATTN_UNROLL = 2
VMEM_LIMIT = 56 * 1024 * 1024

NT_DIMS = (((1,), (1,)), ((), ()))


def _dot(a, b):
    return jnp.dot(a, b, preferred_element_type=F32)


def _dot_nt(a, b):
    return lax.dot_general(a, b, NT_DIMS, preferred_element_type=F32)


def _split(x):
    hi = x.astype(BF16)
    lo = (x - hi.astype(F32)).astype(BF16)
    return hi, lo


def _dot3(a, b_hi, b_lo):
    a_hi, a_lo = _split(a)
    return _dot(a_hi, b_hi) + _dot(a_hi, b_lo) + _dot(a_lo, b_hi)


def _const_spec(shape):
    nd = len(shape)
    return pl.BlockSpec(shape, lambda *_: (0,) * nd, pipeline_mode=pl.Buffered(1))


def _mod_kernel(c_ref, w_ref, b_ref, o_ref):
    c = c_ref[...]
    cs = c * jax.nn.sigmoid(c)
    w_hi, w_lo = _split(w_ref[...])
    o_ref[...] = _dot3(cs, w_hi, w_lo) + b_ref[...]


def _modulation(c, w_ada, b_ada):
    B = c.shape[0]
    n = w_ada.shape[1]
    tn = 1024
    return pl.pallas_call(
        _mod_kernel,
        grid=(n // tn,),
        in_specs=[pl.BlockSpec((B, D_MODEL), lambda j: (0, 0)),
                  pl.BlockSpec((D_MODEL, tn), lambda j: (0, j)),
                  pl.BlockSpec((1, tn), lambda j: (0, j))],
        out_specs=pl.BlockSpec((B, tn), lambda j: (0, j)),
        out_shape=jax.ShapeDtypeStruct((B, n), F32),
        compiler_params=pltpu.CompilerParams(vmem_limit_bytes=VMEM_LIMIT),
        name="adaln_mod",
    )(c, w_ada, b_ada.reshape(1, n))


def _proj_kernel(x_ref, mod_ref, g1_ref, wqk_ref, wvt_ref, wih_ref, wil_ref, wg_ref, grp_ref,
                 gains_ref,
                 qa_ref, ka_ref, qb_ref, kb_ref, vat_ref, vbt_ref, qcat_ref, kcat_ref, wi_ref,
                 gates_ref):
    x = x_ref[...]
    ms = jnp.mean(x * x, axis=-1, keepdims=True)
    h = x * lax.rsqrt(ms + RMS_EPS) * g1_ref[...]
    h = h * (1.0 + mod_ref[1:2, :]) + mod_ref[0:1, :]
    h_hi, h_lo = _split(h)

    grp = grp_ref[...]

    def head_norm(z, gain):
        m = _dot((z * z).astype(BF16), grp) * (1.0 / D_A)
        return (z * lax.rsqrt(m + RMS_EPS) * gain).astype(BF16)

    qk = _dot(h_hi, wqk_ref[...])
    qa_ref[...] = head_norm(qk[:, 0:W_A], gains_ref[0:1, :])
    ka_ref[...] = head_norm(qk[:, W_A:2 * W_A], gains_ref[1:2, :])
    qb_ref[...] = head_norm(qk[:, 2 * W_A:2 * W_A + W_B], gains_ref[2:3, :])
    kb_ref[...] = head_norm(qk[:, 2 * W_A + W_B:], gains_ref[3:4, :])
    vt = _dot_nt(wvt_ref[...], h_hi)
    for s in range(x.shape[0] // KT):
        vat_ref[s] = vt[0:W_A, s * KT:(s + 1) * KT].astype(BF16)
        vbt_ref[s] = vt[W_A:, s * KT:(s + 1) * KT].astype(BF16)

    pidx = (_dot(h_hi, wih_ref[...]) + _dot(h_hi, wil_ref[...]) + _dot(h_lo, wih_ref[...]))
    low_half = lax.broadcasted_iota(I32, (x.shape[0], 128), 1) < D_I
    for g in range(H_I // 2):
        pair = pidx[:, g * 128:(g + 1) * 128]
        hi = pair.astype(BF16).astype(F32)
        lo = pair - hi
        hi_sw = pltpu.roll(hi, D_I, 1)
        lo_sw = pltpu.roll(lo, D_I, 1)
        base = 2 * g * 256
        qcat_ref[:, base:base + 128] = jnp.where(low_half, hi, lo_sw).astype(BF16)
        qcat_ref[:, base + 128:base + 256] = jnp.where(low_half, hi, 0.0).astype(BF16)
        qcat_ref[:, base + 256:base + 384] = jnp.where(low_half, hi_sw, lo).astype(BF16)
        qcat_ref[:, base + 384:base + 512] = jnp.where(low_half, hi_sw, 0.0).astype(BF16)
    tail = pidx[:, 256:384]
    hi = tail.astype(BF16).astype(F32)
    lo = tail - hi
    kcat_ref[:, 0:128] = jnp.where(low_half, hi, pltpu.roll(hi, D_I, 1)).astype(BF16)
    kcat_ref[:, 128:256] = jnp.where(low_half, lo, 0.0).astype(BF16)
    wi_ref[...] = tail

    gates_ref[...] = jax.nn.sigmoid(_dot(h_hi, wg_ref[...])).astype(BF16)


def _projection(x2, mod3, norm1_g, w_in, qn_a, kn_a, qn_b, kn_b, S):
    N = x2.shape[0]
    tm = TM_PROJ
    per_batch = S // tm
    w_qk = jnp.concatenate([w_in[:, 0:2 * W_A], w_in[:, 3 * W_A:3 * W_A + 2 * W_B]],
                           axis=1).astype(BF16)
    w_vt = jnp.concatenate([w_in[:, 2 * W_A:3 * W_A], w_in[:, 3 * W_A + 2 * W_B:W_QKV]],
                           axis=1).T.astype(BF16)
    w_idx = jnp.pad(w_in[:, W_QKV:W_QKV + W_IDX], ((0, 0), (0, W_IDX_PAD - W_IDX)))
    w_idx_hi = w_idx.astype(BF16)
    w_idx_lo = (w_idx - w_idx_hi.astype(F32)).astype(BF16)
    w_g = w_in[:, W_QKV + W_IDX:].astype(BF16)
    grp = jnp.asarray(np.kron(np.eye(W_A // D_A), np.ones((D_A, D_A))), BF16)
    gains = jnp.stack([jnp.tile(qn_a, H_A) * (LOG2E * D_A ** -0.5), jnp.tile(kn_a, H_A),
                       jnp.tile(qn_b, 2 * H_B) * (LOG2E * D_B ** -0.5), jnp.tile(kn_b, 2 * H_B)])

    def row_spec(w):
        return pl.BlockSpec((tm, w), lambda t: (t, 0))

    def rows(w, dt):
        return row_spec(w), jax.ShapeDtypeStruct((N, w), dt)

    def transposed(w):
        return (pl.BlockSpec((tm // KT, w, KT), lambda t: (t, 0, 0)),
                jax.ShapeDtypeStruct((N // KT, w, KT), BF16))

    outs = [rows(W_A, BF16), rows(W_A, BF16), rows(W_B, BF16), rows(W_B, BF16),
            transposed(W_A), transposed(W_B),
            rows(H_I * 256, BF16), rows(256, BF16), rows(128, F32), rows(W_G, BF16)]
    return pl.pallas_call(
        _proj_kernel,
        grid=(N // tm,),
        in_specs=[row_spec(D_MODEL),
                  pl.BlockSpec((None, 6, D_MODEL), lambda t: (t // per_batch, 0, 0)),
                  _const_spec((1, D_MODEL)),
                  _const_spec((D_MODEL, 2 * W_A + 2 * W_B)),
                  _const_spec((W_A + W_B, D_MODEL)),
                  _const_spec((D_MODEL, W_IDX_PAD)),
                  _const_spec((D_MODEL, W_IDX_PAD)),
                  _const_spec((D_MODEL, W_G)),
                  _const_spec((W_A, W_A)),
                  _const_spec((4, W_A))],
        out_specs=[o[0] for o in outs],
        out_shape=[o[1] for o in outs],
        compiler_params=pltpu.CompilerParams(dimension_semantics=("parallel",),
                                             vmem_limit_bytes=VMEM_LIMIT),
        name="in_proj",
    )(x2, mod3, norm1_g.reshape(1, D_MODEL), w_qk, w_vt, w_idx_hi, w_idx_lo, w_g, grp, gains)


def _count_ge(ref, n_rows, cand):
    accs = [jnp.zeros((8, TQ), F32) for _ in range(CNT_ACCS)]
    for r in range(n_rows // 8):
        blk = ref[r * 8:(r + 1) * 8, :]
        accs[r % CNT_ACCS] = accs[r % CNT_ACCS] + jnp.where(blk >= cand, 1.0, 0.0)
    return jnp.sum(functools.reduce(lambda a, b: a + b, accs), axis=0, keepdims=True)


def _count_ge_bf16(ref, n_rows, cand):
    rows = 16
    one, zero = jnp.ones((), BF16), jnp.zeros((), BF16)

    assert n_rows // rows <= 256
    accs = [jnp.zeros((rows, TQ), BF16) for _ in range(CNT_ACCS)]
    for r in range(n_rows // rows):
        blk = ref[r * rows:(r + 1) * rows, :]
        accs[r % CNT_ACCS] = accs[r % CNT_ACCS] + jnp.where(blk >= cand, one, zero)
    total = functools.reduce(lambda a, b: a + b, [a.astype(F32) for a in accs])
    return jnp.sum(total, axis=0, keepdims=True)


def _rank_to_f32(rank):
    key = rank ^ jnp.int32(-2 ** 31)
    return pltpu.bitcast(jnp.where(key < 0, key ^ jnp.int32(0x7FFFFFFF), key), F32)


def _store_bias(bias_ref, bias_t, L):
    bias_ref[0:L, :] = bias_t


def _select_bias(qb, qcat_ref, kcat_ref, wt_ref, score_ref, coarse_ref, eqi_ref, bias_ref):
    L = (qb + 1) * TQ
    admissible = ((lax.broadcasted_iota(I32, (KT, TQ), 0) // CHUNK)
                  <= (lax.broadcasted_iota(I32, (KT, TQ), 1) // CHUNK))
    if L <= TOPK_MAX:
        _store_bias(bias_ref, jnp.where(admissible, 0.0, MASK_VALUE), L)
        return

    for j in range(L // KT):
        rows = slice(j * KT, (j + 1) * KT)
        score = jnp.zeros((KT, TQ), F32)
        for h in range(H_I):
            logits = _dot_nt(kcat_ref[rows, :], qcat_ref[:, h * 256:(h + 1) * 256])
            score = score + wt_ref[h:h + 1, :] * jnp.maximum(logits, 0.0)
        if j == qb:
            score = jnp.where(admissible, score, -jnp.inf)
        score_ref[rows, :] = score
        coarse_ref[rows, :] = score.astype(BF16)
    k_sel = float(TOPK_MAX)

    def coarse_body(it, rank):
        cand = rank | lax.shift_left(jnp.int32(1), 31 - it)
        c = _count_ge_bf16(coarse_ref, L, _rank_to_f32(cand).astype(BF16))
        return jnp.where(c >= k_sel, cand, rank)
    base = lax.fori_loop(0, 16, coarse_body, jnp.zeros((1, TQ), I32)) - jnp.int32(1 << 16)

    def fine_body(it, st):
        off, cnt = st
        cand = off | lax.shift_left(jnp.int32(1), 16 - it)
        c = _count_ge(score_ref, L, _rank_to_f32(base + cand))
        take = c >= k_sel
        return jnp.where(take, cand, off), jnp.where(take, c, cnt)
    off, cnt = lax.fori_loop(
        0, 17, fine_body, (jnp.zeros((1, TQ), I32), jnp.full((1, TQ), float(L), F32)))
    rank = base + off
    thr = _rank_to_f32(rank)
    tied = jnp.max(cnt) > k_sel

    @pl.when(jnp.logical_not(tied))
    def _():
        _store_bias(bias_ref, jnp.where(score_ref[0:L, :] >= thr, 0.0, MASK_VALUE), L)

    @pl.when(tied)
    def _():
        above = _rank_to_f32(rank + 1)
        need = k_sel - _count_ge(score_ref, L, above)
        big = jnp.int32(1 << 30)
        sc = score_ref[0:L, :]
        s_idx = lax.broadcasted_iota(I32, (L, TQ), 0)
        eqi_ref[0:L, :] = jnp.where(sc >= above, big, jnp.where(sc >= thr, s_idx, big))
        n_bits = max(1, int(np.ceil(np.log2(L))))

        def idx_body(it, j):
            cand = j | lax.shift_left(jnp.int32(1), n_bits - 1 - it)
            cnt_lt = float(L) - _count_ge(eqi_ref, L, cand)
            return jnp.where(cnt_lt < need, cand, j)
        j_sel = lax.fori_loop(0, n_bits, idx_body, jnp.zeros((1, TQ), I32))
        sel = jnp.where(score_ref[0:L, :] >= above, 0.0,
                        jnp.where(eqi_ref[0:L, :] <= j_sel, 0.0, MASK_VALUE))
        _store_bias(bias_ref, sel, L)


def _tile_offsets():
    return (lax.broadcasted_iota(I32, (KT, TQ), 1)
            - lax.broadcasted_iota(I32, (KT, TQ), 0)).astype(F32)


def _masked_halves(q_ref, n_groups):
    lane = lax.broadcasted_iota(I32, (TQ, 128), 1)
    low = jnp.where(lane < 64, 1.0, 0.0)
    high = 1.0 - low
    out = []
    for g in range(n_groups):
        q2 = q_ref[:, g * 128:(g + 1) * 128].astype(F32)
        out += [(q2 * low).astype(BF16), (q2 * high).astype(BF16)]
    return out


ONES_ROWS = 16


def _with_ones(vt):
    return jnp.concatenate([vt, jnp.ones((ONES_ROWS, vt.shape[1]), BF16)], axis=0)


def _online_steps(scores, values, maxes, accs):
    ss = [f() for f in scores]
    out = []
    for s, v, m, (acc_ref, i) in zip(ss, values, maxes, accs):
        m_new = jnp.maximum(m, jnp.max(s, axis=0, keepdims=True))
        p = jnp.exp2(s - m_new).astype(BF16)
        acc_ref[i] = jnp.exp2(m - m_new) * acc_ref[i] + _dot(_with_ones(v()), p)
        out.append(m_new)
    return tuple(out)


def _softmax_init(*acc_refs):
    for acc_ref in acc_refs:
        acc_ref[...] = jnp.zeros(acc_ref.shape, F32)
    n = sum(r.shape[0] for r in acc_refs)
    return tuple(jnp.full((1, TQ), -jnp.inf, F32) for _ in range(n))


def _softmax_result(acc_ref, i):
    rows = acc_ref.shape[1] - ONES_ROWS
    return acc_ref[i, 0:rows, :] / acc_ref[i, rows:rows + 1, :]


def _key_rows(j):
    return pl.ds(pl.multiple_of(j * KT, KT), KT)


def _split3(x):
    a = x.astype(BF16)
    r = x - a.astype(F32)
    b = r.astype(BF16)
    return a, b, (r - b.astype(F32)).astype(BF16)


def _alibi_tables(S, slopes):
    pos = jnp.arange(S, dtype=I32)
    hi = (pos // 64 * 64).astype(F32)
    lo = (pos % 64).astype(F32)
    lam = jnp.full((S,), LOG2E, F32)
    k_cols = [*_split3(lam), *_split3(lam), *_split3(lam * hi), *_split3(lam * lo)]
    kaug = jnp.pad(jnp.stack(k_cols, axis=1), ((0, 0), (0, 128 - len(k_cols))))
    qaug = []
    for sl in slopes:
        q_cols = [-sl * hi] * 3 + [-sl * lo] * 3 + [jnp.full((S,), sl, F32)] * 6
        qaug.append(jnp.pad(jnp.stack(q_cols, axis=1), ((0, 0), (0, 128 - len(q_cols)))))
    return kaug, jnp.stack(qaug).astype(BF16)


def _attn_kernel(qb, qcat_ref, kcat_ref, wt_ref, qa_ref, ka_ref, vat_ref, kaug_ref, qaug_a_ref,
                 lam_ref, sg_ref, qb_ref, kb_ref, vbt_ref, qaug_b_ref, oa_ref, ob_ref,
                 score_ref, coarse_ref, eqi_ref, bias_ref, acc_a_ref, acc_b_ref):
    _select_bias(qb, qcat_ref, kcat_ref, wt_ref, score_ref, coarse_ref, eqi_ref, bias_ref)
    n_b = 2 * H_B
    qms_a = _masked_halves(qa_ref, H_A // 2)
    qms_b = _masked_halves(qb_ref, H_B)
    qx_a = [jnp.concatenate([qms_a[h], qaug_a_ref[h]], axis=1) for h in range(H_A)]
    qx_b = [jnp.concatenate([qms_b[i], qaug_b_ref[i // 2]], axis=1) for i in range(n_b)]
    accs = [(acc_a_ref, h) for h in range(H_A)] + [(acc_b_ref, i) for i in range(n_b)]

    def values(j):
        return ([lambda h=h: vat_ref[j, h * D_A:(h + 1) * D_A, :] for h in range(H_A)]
                + [lambda i=i: vbt_ref[j, (i // 2) * 128:(i // 2 + 1) * 128, :]
                   for i in range(n_b)])

    def off_diagonal(j, maxes):
        rows = _key_rows(j)
        bias_t = bias_ref[rows, :]
        kaug_t = kaug_ref[rows, :]
        kx_a = [jnp.concatenate([ka_ref[rows, g * 128:(g + 1) * 128], kaug_t], axis=1)
                for g in range(H_A // 2)]
        kx_b = [jnp.concatenate([kb_ref[rows, h * 128:(h + 1) * 128], kaug_t], axis=1)
                for h in range(H_B)]
        scores = ([lambda h=h: _dot_nt(kx_a[h // 2], qx_a[h]) + bias_t for h in range(H_A)]
                  + [lambda i=i: _dot_nt(kx_b[i // 2], qx_b[i]) for i in range(n_b)])
        return _online_steps(scores, values(j), maxes, accs)

    maxes = lax.fori_loop(0, qb, off_diagonal, _softmax_init(acc_a_ref, acc_b_ref),
                          unroll=ATTN_UNROLL)

    diag = slice(qb * KT, (qb + 1) * KT)
    bias_t = bias_ref[diag, :]
    dist = jnp.abs(_tile_offsets())
    allowed = ((lax.broadcasted_iota(I32, (KT, TQ), 0) // CHUNK)
               <= (lax.broadcasted_iota(I32, (KT, TQ), 1) // CHUNK))
    biases_b = [jnp.where(allowed, -(LOG2E * sl) * dist, MASK_VALUE) for sl in SLOPES_B]
    scores = ([lambda h=h: (_dot_nt(ka_ref[diag, (h // 2) * 128:(h // 2 + 1) * 128], qms_a[h])
                            + (bias_t - (LOG2E * SLOPES_A[h]) * dist)) for h in range(H_A)]
              + [lambda i=i: (_dot_nt(kb_ref[diag, (i // 2) * 128:(i // 2 + 1) * 128], qms_b[i])
                              + biases_b[i // 2]) for i in range(n_b)])
    _online_steps(scores, values(qb), maxes, accs)

    out_t = jnp.concatenate([_softmax_result(acc_a_ref, h) for h in range(H_A)], axis=0)
    oa_ref[...] = out_t.T.astype(BF16)

    lam = (jnp.exp(jnp.sum(lam_ref[0:1, :] * lam_ref[1:2, :], axis=-1, keepdims=True))
           - jnp.exp(jnp.sum(lam_ref[2:3, :] * lam_ref[3:4, :], axis=-1, keepdims=True))
           + LAMBDA_INIT)
    ys = []
    for h in range(H_B):
        o = _softmax_result(acc_b_ref, 2 * h) - lam * _softmax_result(acc_b_ref, 2 * h + 1)
        ys.append(o * lax.rsqrt(jnp.mean(o * o, axis=0, keepdims=True) + RMS_EPS))
    y = jnp.concatenate(ys, axis=0).T
    ob_ref[...] = (y * sg_ref[...] * (1.0 - LAMBDA_INIT)).astype(BF16)


def _attn_block(qb, qcat, kcat, wt, qa, ka, vat, kaug, qaug_a, lam4, subln_g, qbn, kbn, vbt,
                qaug_b):
    B, S = qa.shape[0], qa.shape[1]
    L = (qb + 1) * TQ

    def q_rows(w):
        return pl.BlockSpec((None, TQ, w), lambda b: (b, qb, 0))

    def key_rows(w):
        return pl.BlockSpec((None, L, w), lambda b: (b, 0, 0))

    def value_tiles(w):
        return pl.BlockSpec((S // KT, w, KT), lambda b: (b, 0, 0))

    def qaug_spec(h):
        return pl.BlockSpec((h, TQ, 128), lambda b: (0, qb, 0))

    return pl.pallas_call(
        functools.partial(_attn_kernel, qb),
        grid=(B,),
        in_specs=[q_rows(H_I * 256), key_rows(256),
                  pl.BlockSpec((None, H_I, TQ), lambda b: (b, 0, qb)),
                  q_rows(W_A), key_rows(W_A), value_tiles(W_A),
                  pl.BlockSpec((L, 128), lambda b: (0, 0)), qaug_spec(H_A),
                  pl.BlockSpec((4, D_B), lambda b: (0, 0)),
                  pl.BlockSpec((1, W_B), lambda b: (0, 0)),
                  q_rows(W_B), key_rows(W_B), value_tiles(W_B), qaug_spec(H_B)],
        out_specs=[q_rows(W_A), q_rows(W_B)],
        out_shape=[jax.ShapeDtypeStruct(qa.shape, BF16), jax.ShapeDtypeStruct(qbn.shape, BF16)],
        input_output_aliases={3: 0, 10: 1},
        scratch_shapes=[pltpu.VMEM((S, TQ), F32), pltpu.VMEM((S, TQ), BF16),
                        pltpu.VMEM((S, TQ), I32), pltpu.VMEM((S, TQ), F32),
                        pltpu.VMEM((H_A, D_A + ONES_ROWS, TQ), F32),
                        pltpu.VMEM((2 * H_B, 2 * D_B + ONES_ROWS, TQ), F32)],
        compiler_params=pltpu.CompilerParams(dimension_semantics=("parallel",),
                                             vmem_limit_bytes=VMEM_LIMIT),
        name=f"attn_q{qb}",
    )(qcat, kcat, wt, qa, ka, vat, kaug, qaug_a,
      lam4, jnp.tile(subln_g, H_B).reshape(1, W_B), qbn, kbn, vbt, qaug_b)


def _ffn_kernel(x_ref, mod_ref, ya_ref, yb_ref, gates_ref, wua_ref, wub_ref, wo_ref, g2_ref,
                w1_ref, w3_ref, w2_ref, o_ref):
    gates = gates_ref[...].astype(F32)
    merged = (gates[:, :D_MODEL] * _dot(ya_ref[...], wua_ref[...])
              + gates[:, D_MODEL:] * _dot(yb_ref[...], wub_ref[...]))
    x1 = x_ref[...] + mod_ref[2:3, :] * _dot(merged.astype(BF16), wo_ref[...])
    ms = jnp.mean(x1 * x1, axis=-1, keepdims=True)
    h2 = x1 * lax.rsqrt(ms + RMS_EPS) * g2_ref[...]
    h2 = (h2 * (1.0 + mod_ref[4:5, :]) + mod_ref[3:4, :]).astype(BF16)
    u = _dot(h2, w1_ref[...])
    act = (u * jax.nn.sigmoid(u) * _dot(h2, w3_ref[...])).astype(BF16)
    o_ref[...] = x1 + mod_ref[5:6, :] * _dot(act, w2_ref[...])


def _merge_ffn(x2, mod3, ya, yb, gates, w_up_a, w_up_b, w_o, norm2_g, w_ff1, w_ff3, w_ff2, S):
    N = x2.shape[0]
    tm = TM_FFN
    per_batch = S // tm

    def row_spec(w):
        return pl.BlockSpec((tm, w), lambda t: (t, 0))

    return pl.pallas_call(
        _ffn_kernel,
        grid=(N // tm,),
        in_specs=[row_spec(D_MODEL),
                  pl.BlockSpec((None, 6, D_MODEL), lambda t: (t // per_batch, 0, 0)),
                  row_spec(W_A), row_spec(W_B), row_spec(W_G),
                  _const_spec((W_A, D_MODEL)), _const_spec((W_B, D_MODEL)),
                  _const_spec((D_MODEL, D_MODEL)), _const_spec((1, D_MODEL)),
                  _const_spec((D_MODEL, D_FF)), _const_spec((D_MODEL, D_FF)),
                  _const_spec((D_FF, D_MODEL))],
        out_specs=row_spec(D_MODEL),
        out_shape=jax.ShapeDtypeStruct((N, D_MODEL), F32),
        compiler_params=pltpu.CompilerParams(dimension_semantics=("parallel",),
                                             vmem_limit_bytes=VMEM_LIMIT),
        name="merge_ffn",
    )(x2, mod3, ya, yb, gates, w_up_a.astype(BF16), w_up_b.astype(BF16), w_o.astype(BF16),
      norm2_g.reshape(1, D_MODEL), w_ff1.astype(BF16), w_ff3.astype(BF16), w_ff2.astype(BF16))


def kernel(x, c, w_ada, b_ada, norm1_g, w_in, qn_a, kn_a, qn_b, kn_b, lam_q1, lam_k1, lam_q2,
           lam_k2, subln_g, w_up_a, w_up_b, w_o, norm2_g, w_ff1, w_ff3, w_ff2):
    B, S, D = x.shape
    assert D == D_MODEL and S % TQ == 0 and w_ada.shape[0] == 1
    N = B * S
    x2 = x.reshape(N, D)
    mod3 = _modulation(c, w_ada[0], b_ada[0]).reshape(B, 6, D)

    (qa, ka, qbn, kbn, vat, vbt, qcat, kcat, wi, gates) = _projection(
        x2, mod3, norm1_g[0], w_in[0], qn_a[0], kn_a[0], qn_b[0], kn_b[0], S)
    r3 = lambda a: a.reshape(B, S, a.shape[-1])
    qa, ka, qbn, kbn, qcat, kcat = map(r3, (qa, ka, qbn, kbn, qcat, kcat))
    wt = wi[:, D_I:D_I + H_I].reshape(B, S, H_I).transpose(0, 2, 1)
    lam4 = jnp.stack([lam_q1[0], lam_k1[0], lam_q2[0], lam_k2[0]])

    kaug, qaug_a = _alibi_tables(S, SLOPES_A)
    _, qaug_b = _alibi_tables(S, SLOPES_B)
    ya, yb = qa, qbn
    for qb in range(S // TQ):
        ya, yb = _attn_block(qb, qcat, kcat, wt, ya, ka, vat, kaug, qaug_a,
                             lam4, subln_g[0], yb, kbn, vbt, qaug_b)
    ya = ya.reshape(N, W_A)
    yb = yb.reshape(N, W_B)

    out = _merge_ffn(x2, mod3, ya, yb, gates, w_up_a[0], w_up_b[0], w_o[0], norm2_g[0],
                     w_ff1[0], w_ff3[0], w_ff2[0], S)
    return out.reshape(B, S, D)
```

```python
import functools

import numpy as np
import jax
import jax.numpy as jnp
from jax import lax
from jax.experimental import pallas as pl
from jax.experimental.pallas import tpu as pltpu

F32 = jnp.float32
BF16 = jnp.bfloat16
I32 = jnp.int32

D_MODEL = 1024
CHUNK = 64
H_A, D_A = 8, 64
H_I, D_I = 4, 64
TOPK_MAX = 256
H_B, D_B = 4, 64
D_FF = 2816
RMS_EPS = 1e-6
MASK_VALUE = -1e30
LAMBDA_INIT = 0.2
LOG2E = 1.4426950408889634
SLOPES_A = [2.0 ** (-8.0 * (h + 1) / H_A) for h in range(H_A)]
SLOPES_B = [2.0 ** (-8.0 * (h + 1) / H_B) for h in range(H_B)]

W_A = H_A * D_A
W_B = H_B * 2 * D_B
W_QKV = 3 * W_A + 3 * W_B
W_IDX = H_I * D_I + D_I + H_I
W_IDX_PAD = 384
W_G = 2 * D_MODEL

TQ = 256
KT = TQ
TM_PROJ = 2 * KT
TM_FFN = 512
CNT_ROWS = 256
CNT_ACCS = 4
ATTN_UNROLL = True
VMEM_LIMIT = 56 * 1024 * 1024

NT_DIMS = (((1,), (1,)), ((), ()))


def _dot(a, b):
    return jnp.dot(a, b, preferred_element_type=F32)


def _dot_nt(a, b):
    return lax.dot_general(a, b, NT_DIMS, preferred_element_type=F32)


def _split(x):
    hi = x.astype(BF16)
    lo = (x - hi.astype(F32)).astype(BF16)
    return hi, lo


def _dot3(a, b_hi, b_lo):
    a_hi, a_lo = _split(a)
    return _dot(a_hi, b_hi) + _dot(a_hi, b_lo) + _dot(a_lo, b_hi)


def _const_spec(shape):
    nd = len(shape)
    return pl.BlockSpec(shape, lambda *_: (0,) * nd, pipeline_mode=pl.Buffered(1))


def _mod_kernel(c_ref, w_ref, b_ref, o_ref):
    c = c_ref[...]
    cs = c * jax.nn.sigmoid(c)
    w_hi, w_lo = _split(w_ref[...])
    o_ref[...] = _dot3(cs, w_hi, w_lo) + b_ref[...]


def _modulation(c, w_ada, b_ada):
    B = c.shape[0]
    n = w_ada.shape[1]
    tn = 1024
    return pl.pallas_call(
        _mod_kernel,
        grid=(n // tn,),
        in_specs=[pl.BlockSpec((B, D_MODEL), lambda j: (0, 0)),
                  pl.BlockSpec((D_MODEL, tn), lambda j: (0, j)),
                  pl.BlockSpec((1, tn), lambda j: (0, j))],
        out_specs=pl.BlockSpec((B, tn), lambda j: (0, j)),
        out_shape=jax.ShapeDtypeStruct((B, n), F32),
        compiler_params=pltpu.CompilerParams(vmem_limit_bytes=VMEM_LIMIT),
        name="adaln_mod",
    )(c, w_ada, b_ada.reshape(1, n))


def _proj_kernel(x_ref, mod_ref, g1_ref, wqk_ref, wvt_ref, wih_ref, wil_ref, wg_ref, grp_ref,
                 gains_ref,
                 qa_ref, ka_ref, qb_ref, kb_ref, vat_ref, vbt_ref, qcat_ref, kcat_ref, wi_ref,
                 gates_ref):
    x = x_ref[...]
    ms = jnp.mean(x * x, axis=-1, keepdims=True)
    h = x * lax.rsqrt(ms + RMS_EPS) * g1_ref[...]
    h = h * (1.0 + mod_ref[1:2, :]) + mod_ref[0:1, :]
    h_hi, h_lo = _split(h)

    qk = _dot(h_hi, wqk_ref[...])
    vt = _dot_nt(wvt_ref[...], h_hi)
    pidx = (_dot(h_hi, wih_ref[...]) + _dot(h_hi, wil_ref[...]) + _dot(h_lo, wih_ref[...]))
    gate_logits = _dot(h_hi, wg_ref[...])

    grp = grp_ref[...]
    zs = [qk[:, i * W_A:(i + 1) * W_A] for i in range(4)]
    squares = [(z * z).astype(BF16) for z in zs]
    means = [_dot(sq, grp) * (1.0 / D_A) for sq in squares]
    for i, out_ref in enumerate((qa_ref, ka_ref, qb_ref, kb_ref)):
        out_ref[...] = (zs[i] * lax.rsqrt(means[i] + RMS_EPS) * gains_ref[i:i + 1, :]).astype(BF16)

    for s in range(x.shape[0] // KT):
        vat_ref[s] = vt[0:W_A, s * KT:(s + 1) * KT].astype(BF16)
        vbt_ref[s] = vt[W_A:, s * KT:(s + 1) * KT].astype(BF16)

    low_half = lax.broadcasted_iota(I32, (x.shape[0], 128), 1) < D_I
    for g in range(H_I // 2):
        pair = pidx[:, g * 128:(g + 1) * 128]
        hi = pair.astype(BF16).astype(F32)
        lo = pair - hi
        hi_sw = pltpu.roll(hi, D_I, 1)
        lo_sw = pltpu.roll(lo, D_I, 1)
        base = 2 * g * 256
        qcat_ref[:, base:base + 128] = jnp.where(low_half, hi, lo_sw).astype(BF16)
        qcat_ref[:, base + 128:base + 256] = jnp.where(low_half, hi, 0.0).astype(BF16)
        qcat_ref[:, base + 256:base + 384] = jnp.where(low_half, hi_sw, lo).astype(BF16)
        qcat_ref[:, base + 384:base + 512] = jnp.where(low_half, hi_sw, 0.0).astype(BF16)
    tail = pidx[:, 256:384]
    hi = tail.astype(BF16).astype(F32)
    lo = tail - hi
    kcat_ref[:, 0:128] = jnp.where(low_half, hi, pltpu.roll(hi, D_I, 1)).astype(BF16)
    kcat_ref[:, 128:256] = jnp.where(low_half, lo, 0.0).astype(BF16)
    wi_ref[...] = tail

    gates_ref[...] = jax.nn.sigmoid(gate_logits).astype(BF16)


def _projection(x2, mod3, norm1_g, w_in, qn_a, kn_a, qn_b, kn_b, S):
    N = x2.shape[0]
    tm = TM_PROJ
    per_batch = S // tm
    w_qk = jnp.concatenate([w_in[:, 0:2 * W_A], w_in[:, 3 * W_A:3 * W_A + 2 * W_B]],
                           axis=1).astype(BF16)
    w_vt = jnp.concatenate([w_in[:, 2 * W_A:3 * W_A], w_in[:, 3 * W_A + 2 * W_B:W_QKV]],
                           axis=1).T.astype(BF16)
    w_idx = jnp.pad(w_in[:, W_QKV:W_QKV + W_IDX], ((0, 0), (0, W_IDX_PAD - W_IDX)))
    w_idx_hi = w_idx.astype(BF16)
    w_idx_lo = (w_idx - w_idx_hi.astype(F32)).astype(BF16)
    w_g = w_in[:, W_QKV + W_IDX:].astype(BF16)
    grp = jnp.asarray(np.kron(np.eye(W_A // D_A), np.ones((D_A, D_A))), BF16)
    gains = jnp.stack([jnp.tile(qn_a, H_A) * (LOG2E * D_A ** -0.5), jnp.tile(kn_a, H_A),
                       jnp.tile(qn_b, 2 * H_B) * (LOG2E * D_B ** -0.5), jnp.tile(kn_b, 2 * H_B)])

    def row_spec(w):
        return pl.BlockSpec((tm, w), lambda t: (t, 0))

    def rows(w, dt):
        return row_spec(w), jax.ShapeDtypeStruct((N, w), dt)

    def transposed(w):
        return (pl.BlockSpec((tm // KT, w, KT), lambda t: (t, 0, 0)),
                jax.ShapeDtypeStruct((N // KT, w, KT), BF16))

    outs = [rows(W_A, BF16), rows(W_A, BF16), rows(W_B, BF16), rows(W_B, BF16),
            transposed(W_A), transposed(W_B),
            rows(H_I * 256, BF16), rows(256, BF16), rows(128, F32), rows(W_G, BF16)]
    return pl.pallas_call(
        _proj_kernel,
        grid=(N // tm,),
        in_specs=[row_spec(D_MODEL),
                  pl.BlockSpec((None, 6, D_MODEL), lambda t: (t // per_batch, 0, 0)),
                  _const_spec((1, D_MODEL)),
                  _const_spec((D_MODEL, 2 * W_A + 2 * W_B)),
                  _const_spec((W_A + W_B, D_MODEL)),
                  _const_spec((D_MODEL, W_IDX_PAD)),
                  _const_spec((D_MODEL, W_IDX_PAD)),
                  _const_spec((D_MODEL, W_G)),
                  _const_spec((W_A, W_A)),
                  _const_spec((4, W_A))],
        out_specs=[o[0] for o in outs],
        out_shape=[o[1] for o in outs],
        compiler_params=pltpu.CompilerParams(dimension_semantics=("parallel",),
                                             vmem_limit_bytes=VMEM_LIMIT),
        name="in_proj",
    )(x2, mod3, norm1_g.reshape(1, D_MODEL), w_qk, w_vt, w_idx_hi, w_idx_lo, w_g, grp, gains)


def _count_ge(ref, n_rows, cand):
    accs = [jnp.zeros((8, TQ), F32) for _ in range(CNT_ACCS)]
    for r in range(n_rows // 8):
        blk = ref[r * 8:(r + 1) * 8, :]
        accs[r % CNT_ACCS] = accs[r % CNT_ACCS] + jnp.where(blk >= cand, 1.0, 0.0)
    return jnp.sum(functools.reduce(lambda a, b: a + b, accs), axis=0, keepdims=True)


def _count_ge_bf16(ref, n_rows, cand):
    rows = 16
    one, zero = jnp.ones((), BF16), jnp.zeros((), BF16)

    assert n_rows // rows <= 256
    accs = [jnp.zeros((rows, TQ), BF16) for _ in range(CNT_ACCS)]
    for r in range(n_rows // rows):
        blk = ref[r * rows:(r + 1) * rows, :]
        accs[r % CNT_ACCS] = accs[r % CNT_ACCS] + jnp.where(blk >= cand, one, zero)
    total = functools.reduce(lambda a, b: a + b, [a.astype(F32) for a in accs])
    return jnp.sum(total, axis=0, keepdims=True)


def _rank_to_f32(rank):
    key = rank ^ jnp.int32(-2 ** 31)
    return pltpu.bitcast(jnp.where(key < 0, key ^ jnp.int32(0x7FFFFFFF), key), F32)


def _store_bias(bias_ref, bias_t, L):
    bias_ref[0:L, :] = bias_t


def _select_bias(qb, qcat_ref, kcat_ref, wt_ref, score_ref, coarse_ref, eqi_ref, bias_ref):
    L = (qb + 1) * TQ
    admissible = ((lax.broadcasted_iota(I32, (KT, TQ), 0) // CHUNK)
                  <= (lax.broadcasted_iota(I32, (KT, TQ), 1) // CHUNK))
    if L <= TOPK_MAX:
        _store_bias(bias_ref, jnp.where(admissible, 0.0, MASK_VALUE), L)
        return

    for j in range(L // KT):
        rows = slice(j * KT, (j + 1) * KT)
        score = jnp.zeros((KT, TQ), F32)
        for h in range(H_I):
            logits = _dot_nt(kcat_ref[rows, :], qcat_ref[:, h * 256:(h + 1) * 256])
            score = score + wt_ref[h:h + 1, :] * jnp.maximum(logits, 0.0)
        if j == qb:
            score = jnp.where(admissible, score, -jnp.inf)
        score_ref[rows, :] = score
        coarse_ref[rows, :] = score.astype(BF16)
    k_sel = float(TOPK_MAX)

    def coarse_body(it, rank):
        cand = rank | lax.shift_left(jnp.int32(1), 31 - it)
        c = _count_ge_bf16(coarse_ref, L, _rank_to_f32(cand).astype(BF16))
        return jnp.where(c >= k_sel, cand, rank)
    base = lax.fori_loop(0, 16, coarse_body, jnp.zeros((1, TQ), I32)) - jnp.int32(1 << 16)

    def fine_body(it, st):
        off, cnt = st
        cand = off | lax.shift_left(jnp.int32(1), 16 - it)
        c = _count_ge(score_ref, L, _rank_to_f32(base + cand))
        take = c >= k_sel
        return jnp.where(take, cand, off), jnp.where(take, c, cnt)
    off, cnt = lax.fori_loop(
        0, 17, fine_body, (jnp.zeros((1, TQ), I32), jnp.full((1, TQ), float(L), F32)))
    rank = base + off
    thr = _rank_to_f32(rank)
    tied = jnp.max(cnt) > k_sel

    @pl.when(jnp.logical_not(tied))
    def _():
        _store_bias(bias_ref, jnp.where(score_ref[0:L, :] >= thr, 0.0, MASK_VALUE), L)

    @pl.when(tied)
    def _():
        above = _rank_to_f32(rank + 1)
        need = k_sel - _count_ge(score_ref, L, above)
        big = jnp.int32(1 << 30)
        sc = score_ref[0:L, :]
        s_idx = lax.broadcasted_iota(I32, (L, TQ), 0)
        eqi_ref[0:L, :] = jnp.where(sc >= above, big, jnp.where(sc >= thr, s_idx, big))
        n_bits = max(1, int(np.ceil(np.log2(L))))

        def idx_body(it, j):
            cand = j | lax.shift_left(jnp.int32(1), n_bits - 1 - it)
            cnt_lt = float(L) - _count_ge(eqi_ref, L, cand)
            return jnp.where(cnt_lt < need, cand, j)
        j_sel = lax.fori_loop(0, n_bits, idx_body, jnp.zeros((1, TQ), I32))
        sel = jnp.where(score_ref[0:L, :] >= above, 0.0,
                        jnp.where(eqi_ref[0:L, :] <= j_sel, 0.0, MASK_VALUE))
        _store_bias(bias_ref, sel, L)


def _tile_offsets():
    return (lax.broadcasted_iota(I32, (KT, TQ), 1)
            - lax.broadcasted_iota(I32, (KT, TQ), 0)).astype(F32)


def _masked_halves(q_ref, n_groups):
    lane = lax.broadcasted_iota(I32, (TQ, 128), 1)
    low = jnp.where(lane < 64, 1.0, 0.0)
    high = 1.0 - low
    out = []
    for g in range(n_groups):
        q2 = q_ref[:, g * 128:(g + 1) * 128].astype(F32)
        out += [(q2 * low).astype(BF16), (q2 * high).astype(BF16)]
    return out


ONES_ROWS = 16


def _with_ones(vt):
    return jnp.concatenate([vt, jnp.ones((ONES_ROWS, vt.shape[1]), BF16)], axis=0)


def _online_steps(scores, values, maxes, accs):
    ss = [f() for f in scores]
    out = []
    for s, v, m, (acc_ref, i) in zip(ss, values, maxes, accs):
        m_new = jnp.maximum(m, jnp.max(s, axis=0, keepdims=True))
        p = jnp.exp2(s - m_new).astype(BF16)
        acc_ref[i] = jnp.exp2(m - m_new) * acc_ref[i] + _dot(_with_ones(v()), p)
        out.append(m_new)
    return tuple(out)


def _softmax_init(*acc_refs):
    for acc_ref in acc_refs:
        acc_ref[...] = jnp.zeros(acc_ref.shape, F32)
    n = sum(r.shape[0] for r in acc_refs)
    return tuple(jnp.full((1, TQ), -jnp.inf, F32) for _ in range(n))


def _softmax_result(acc_ref, i):
    rows = acc_ref.shape[1] - ONES_ROWS
    return acc_ref[i, 0:rows, :] / acc_ref[i, rows:rows + 1, :]


def _key_rows(j):
    return pl.ds(pl.multiple_of(j * KT, KT), KT)


def _split3(x):
    a = x.astype(BF16)
    r = x - a.astype(F32)
    b = r.astype(BF16)
    return a, b, (r - b.astype(F32)).astype(BF16)


def _alibi_tables(S, slopes):
    pos = jnp.arange(S, dtype=I32)
    hi = (pos // 64 * 64).astype(F32)
    lo = (pos % 64).astype(F32)
    lam = jnp.full((S,), LOG2E, F32)
    k_cols = [*_split3(lam), *_split3(lam), *_split3(lam * hi), *_split3(lam * lo)]
    kaug = jnp.pad(jnp.stack(k_cols, axis=1), ((0, 0), (0, 128 - len(k_cols))))
    qaug = []
    for sl in slopes:
        q_cols = [-sl * hi] * 3 + [-sl * lo] * 3 + [jnp.full((S,), sl, F32)] * 6
        qaug.append(jnp.pad(jnp.stack(q_cols, axis=1), ((0, 0), (0, 128 - len(q_cols)))))
    return kaug, jnp.stack(qaug).astype(BF16)


def _attn_kernel(qb, qcat_ref, kcat_ref, wt_ref, qa_ref, ka_ref, vat_ref, kaug_ref, qaug_a_ref,
                 lam_ref, sg_ref, qb_ref, kb_ref, vbt_ref, qaug_b_ref, oa_ref, ob_ref,
                 score_ref, coarse_ref, eqi_ref, bias_ref, acc_a_ref, acc_b_ref):
    _select_bias(qb, qcat_ref, kcat_ref, wt_ref, score_ref, coarse_ref, eqi_ref, bias_ref)
    n_b = 2 * H_B
    qms_a = _masked_halves(qa_ref, H_A // 2)
    qms_b = _masked_halves(qb_ref, H_B)
    qx_a = [jnp.concatenate([qms_a[h], qaug_a_ref[h]], axis=1) for h in range(H_A)]
    qx_b = [jnp.concatenate([qms_b[i], qaug_b_ref[i // 2]], axis=1) for i in range(n_b)]
    accs = [(acc_a_ref, h) for h in range(H_A)] + [(acc_b_ref, i) for i in range(n_b)]

    def values(j):
        return ([lambda h=h: vat_ref[j, h * D_A:(h + 1) * D_A, :] for h in range(H_A)]
                + [lambda i=i: vbt_ref[j, (i // 2) * 128:(i // 2 + 1) * 128, :]
                   for i in range(n_b)])

    def off_diagonal(j, maxes):
        rows = _key_rows(j)
        bias_t = bias_ref[rows, :]
        kaug_t = kaug_ref[rows, :]
        kx_a = [jnp.concatenate([ka_ref[rows, g * 128:(g + 1) * 128], kaug_t], axis=1)
                for g in range(H_A // 2)]
        kx_b = [jnp.concatenate([kb_ref[rows, h * 128:(h + 1) * 128], kaug_t], axis=1)
                for h in range(H_B)]
        scores = ([lambda h=h: _dot_nt(kx_a[h // 2], qx_a[h]) + bias_t for h in range(H_A)]
                  + [lambda i=i: _dot_nt(kx_b[i // 2], qx_b[i]) for i in range(n_b)])
        return _online_steps(scores, values(j), maxes, accs)

    maxes = lax.fori_loop(0, qb, off_diagonal, _softmax_init(acc_a_ref, acc_b_ref),
                          unroll=ATTN_UNROLL)

    diag = slice(qb * KT, (qb + 1) * KT)
    bias_t = bias_ref[diag, :]
    dist = jnp.abs(_tile_offsets())
    allowed = ((lax.broadcasted_iota(I32, (KT, TQ), 0) // CHUNK)
               <= (lax.broadcasted_iota(I32, (KT, TQ), 1) // CHUNK))
    biases_b = [jnp.where(allowed, -(LOG2E * sl) * dist, MASK_VALUE) for sl in SLOPES_B]
    scores = ([lambda h=h: (_dot_nt(ka_ref[diag, (h // 2) * 128:(h // 2 + 1) * 128], qms_a[h])
                            + (bias_t - (LOG2E * SLOPES_A[h]) * dist)) for h in range(H_A)]
              + [lambda i=i: (_dot_nt(kb_ref[diag, (i // 2) * 128:(i // 2 + 1) * 128], qms_b[i])
                              + biases_b[i // 2]) for i in range(n_b)])
    _online_steps(scores, values(qb), maxes, accs)

    out_t = jnp.concatenate([_softmax_result(acc_a_ref, h) for h in range(H_A)], axis=0)
    oa_ref[...] = out_t.T.astype(BF16)

    lam = (jnp.exp(jnp.sum(lam_ref[0:1, :] * lam_ref[1:2, :], axis=-1, keepdims=True))
           - jnp.exp(jnp.sum(lam_ref[2:3, :] * lam_ref[3:4, :], axis=-1, keepdims=True))
           + LAMBDA_INIT)
    ys = []
    for h in range(H_B):
        o = _softmax_result(acc_b_ref, 2 * h) - lam * _softmax_result(acc_b_ref, 2 * h + 1)
        ys.append(o * lax.rsqrt(jnp.mean(o * o, axis=0, keepdims=True) + RMS_EPS))
    y = jnp.concatenate(ys, axis=0).T
    ob_ref[...] = (y * sg_ref[...] * (1.0 - LAMBDA_INIT)).astype(BF16)


def _attn_block(qb, qcat, kcat, wt, qa, ka, vat, kaug, qaug_a, lam4, subln_g, qbn, kbn, vbt,
                qaug_b):
    B, S = qa.shape[0], qa.shape[1]
    L = (qb + 1) * TQ

    def q_rows(w):
        return pl.BlockSpec((None, TQ, w), lambda b: (b, qb, 0))

    def key_rows(w):
        return pl.BlockSpec((None, L, w), lambda b: (b, 0, 0))

    def value_tiles(w):
        return pl.BlockSpec((S // KT, w, KT), lambda b: (b, 0, 0))

    def qaug_spec(h):
        return pl.BlockSpec((h, TQ, 128), lambda b: (0, qb, 0))

    return pl.pallas_call(
        functools.partial(_attn_kernel, qb),
        grid=(B,),
        in_specs=[q_rows(H_I * 256), key_rows(256),
                  pl.BlockSpec((None, H_I, TQ), lambda b: (b, 0, qb)),
                  q_rows(W_A), key_rows(W_A), value_tiles(W_A),
                  pl.BlockSpec((L, 128), lambda b: (0, 0)), qaug_spec(H_A),
                  pl.BlockSpec((4, D_B), lambda b: (0, 0)),
                  pl.BlockSpec((1, W_B), lambda b: (0, 0)),
                  q_rows(W_B), key_rows(W_B), value_tiles(W_B), qaug_spec(H_B)],
        out_specs=[q_rows(W_A), q_rows(W_B)],
        out_shape=[jax.ShapeDtypeStruct(qa.shape, BF16), jax.ShapeDtypeStruct(qbn.shape, BF16)],
        input_output_aliases={3: 0, 10: 1},
        scratch_shapes=[pltpu.VMEM((S, TQ), F32), pltpu.VMEM((S, TQ), BF16),
                        pltpu.VMEM((S, TQ), I32), pltpu.VMEM((S, TQ), F32),
                        pltpu.VMEM((H_A, D_A + ONES_ROWS, TQ), F32),
                        pltpu.VMEM((2 * H_B, 2 * D_B + ONES_ROWS, TQ), F32)],
        compiler_params=pltpu.CompilerParams(dimension_semantics=("parallel",),
                                             vmem_limit_bytes=VMEM_LIMIT),
        name=f"attn_q{qb}",
    )(qcat, kcat, wt, qa, ka, vat, kaug, qaug_a,
      lam4, jnp.tile(subln_g, H_B).reshape(1, W_B), qbn, kbn, vbt, qaug_b)


def _ffn_kernel(x_ref, mod_ref, ya_ref, yb_ref, gates_ref, wua_ref, wub_ref, wo_ref, g2_ref,
                w1_ref, w3_ref, w2_ref, o_ref):
    gates = gates_ref[...].astype(F32)
    merged = (gates[:, :D_MODEL] * _dot(ya_ref[...], wua_ref[...])
              + gates[:, D_MODEL:] * _dot(yb_ref[...], wub_ref[...]))
    x1 = x_ref[...] + mod_ref[2:3, :] * _dot(merged.astype(BF16), wo_ref[...])
    ms = jnp.mean(x1 * x1, axis=-1, keepdims=True)
    h2 = x1 * lax.rsqrt(ms + RMS_EPS) * g2_ref[...]
    h2 = (h2 * (1.0 + mod_ref[4:5, :]) + mod_ref[3:4, :]).astype(BF16)
    u = _dot(h2, w1_ref[...])
    act = (u * jax.nn.sigmoid(u) * _dot(h2, w3_ref[...])).astype(BF16)
    o_ref[...] = x1 + mod_ref[5:6, :] * _dot(act, w2_ref[...])


def _merge_ffn(x2, mod3, ya, yb, gates, w_up_a, w_up_b, w_o, norm2_g, w_ff1, w_ff3, w_ff2, S):
    N = x2.shape[0]
    tm = TM_FFN
    per_batch = S // tm

    def row_spec(w):
        return pl.BlockSpec((tm, w), lambda t: (t, 0))

    return pl.pallas_call(
        _ffn_kernel,
        grid=(N // tm,),
        in_specs=[row_spec(D_MODEL),
                  pl.BlockSpec((None, 6, D_MODEL), lambda t: (t // per_batch, 0, 0)),
                  row_spec(W_A), row_spec(W_B), row_spec(W_G),
                  _const_spec((W_A, D_MODEL)), _const_spec((W_B, D_MODEL)),
                  _const_spec((D_MODEL, D_MODEL)), _const_spec((1, D_MODEL)),
                  _const_spec((D_MODEL, D_FF)), _const_spec((D_MODEL, D_FF)),
                  _const_spec((D_FF, D_MODEL))],
        out_specs=row_spec(D_MODEL),
        out_shape=jax.ShapeDtypeStruct((N, D_MODEL), F32),
        compiler_params=pltpu.CompilerParams(dimension_semantics=("parallel",),
                                             vmem_limit_bytes=VMEM_LIMIT),
        name="merge_ffn",
    )(x2, mod3, ya, yb, gates, w_up_a.astype(BF16), w_up_b.astype(BF16), w_o.astype(BF16),
      norm2_g.reshape(1, D_MODEL), w_ff1.astype(BF16), w_ff3.astype(BF16), w_ff2.astype(BF16))


def kernel(x, c, w_ada, b_ada, norm1_g, w_in, qn_a, kn_a, qn_b, kn_b, lam_q1, lam_k1, lam_q2,
           lam_k2, subln_g, w_up_a, w_up_b, w_o, norm2_g, w_ff1, w_ff3, w_ff2):
    B, S, D = x.shape
    assert D == D_MODEL and S % TQ == 0 and w_ada.shape[0] == 1
    N = B * S
    x2 = x.reshape(N, D)
    mod3 = _modulation(c, w_ada[0], b_ada[0]).reshape(B, 6, D)

    (qa, ka, qbn, kbn, vat, vbt, qcat, kcat, wi, gates) = _projection(
        x2, mod3, norm1_g[0], w_in[0], qn_a[0], kn_a[0], qn_b[0], kn_b[0], S)
    r3 = lambda a: a.reshape(B, S, a.shape[-1])
    qa, ka, qbn, kbn, qcat, kcat = map(r3, (qa, ka, qbn, kbn, qcat, kcat))
    wt = wi[:, D_I:D_I + H_I].reshape(B, S, H_I).transpose(0, 2, 1)
    lam4 = jnp.stack([lam_q1[0], lam_k1[0], lam_q2[0], lam_k2[0]])

    kaug, qaug_a = _alibi_tables(S, SLOPES_A)
    _, qaug_b = _alibi_tables(S, SLOPES_B)
    ya, yb = qa, qbn
    for qb in range(S // TQ):
        ya, yb = _attn_block(qb, qcat, kcat, wt, ya, ka, vat, kaug, qaug_a,
                             lam4, subln_g[0], yb, kbn, vbt, qaug_b)
    ya = ya.reshape(N, W_A)
    yb = yb.reshape(N, W_B)

    out = _merge_ffn(x2, mod3, ya, yb, gates, w_up_a[0], w_up_b[0], w_o[0], norm2_g[0],
                     w_ff1[0], w_ff3[0], w_ff2[0], S)
    return out.reshape(B, S, D)
```

```python
import functools

import numpy as np
import jax
import jax.numpy as jnp
from jax import lax
from jax.experimental import pallas as pl
from jax.experimental.pallas import tpu as pltpu

F32 = jnp.float32
BF16 = jnp.bfloat16
I32 = jnp.int32

D_MODEL = 1024
CHUNK = 64
H_A, D_A = 8, 64
H_I, D_I = 4, 64
TOPK_MAX = 256
H_B, D_B = 4, 64
D_FF = 2816
RMS_EPS = 1e-6
MASK_VALUE = -1e30
LAMBDA_INIT = 0.2
LOG2E = 1.4426950408889634
SLOPES_A = [2.0 ** (-8.0 * (h + 1) / H_A) for h in range(H_A)]
SLOPES_B = [2.0 ** (-8.0 * (h + 1) / H_B) for h in range(H_B)]

W_A = H_A * D_A
W_B = H_B * 2 * D_B
W_QKV = 3 * W_A + 3 * W_B
W_IDX = H_I * D_I + D_I + H_I
W_IDX_PAD = 384
W_G = 2 * D_MODEL

TQ = 256
KT = TQ
TM_PROJ = 2 * KT
TM_FFN = 512
CNT_ROWS = 256
CNT_ACCS = 4
ATTN_UNROLL = True
VMEM_LIMIT = 56 * 1024 * 1024

NT_DIMS = (((1,), (1,)), ((), ()))


def _dot(a, b):
    return jnp.dot(a, b, preferred_element_type=F32)


def _dot_nt(a, b):
    return lax.dot_general(a, b, NT_DIMS, preferred_element_type=F32)


def _split(x):
    hi = x.astype(BF16)
    lo = (x - hi.astype(F32)).astype(BF16)
    return hi, lo


def _dot3(a, b_hi, b_lo):
    a_hi, a_lo = _split(a)
    return _dot(a_hi, b_hi) + _dot(a_hi, b_lo) + _dot(a_lo, b_hi)


def _const_spec(shape):
    nd = len(shape)
    return pl.BlockSpec(shape, lambda *_: (0,) * nd, pipeline_mode=pl.Buffered(1))


def _mod_kernel(c_ref, w_ref, b_ref, o_ref):
    c = c_ref[...]
    cs = c * jax.nn.sigmoid(c)
    w_hi, w_lo = _split(w_ref[...])
    o_ref[...] = _dot3(cs, w_hi, w_lo) + b_ref[...]


def _modulation(c, w_ada, b_ada):
    B = c.shape[0]
    n = w_ada.shape[1]
    tn = 1024
    return pl.pallas_call(
        _mod_kernel,
        grid=(n // tn,),
        in_specs=[pl.BlockSpec((B, D_MODEL), lambda j: (0, 0)),
                  pl.BlockSpec((D_MODEL, tn), lambda j: (0, j)),
                  pl.BlockSpec((1, tn), lambda j: (0, j))],
        out_specs=pl.BlockSpec((B, tn), lambda j: (0, j)),
        out_shape=jax.ShapeDtypeStruct((B, n), F32),
        compiler_params=pltpu.CompilerParams(vmem_limit_bytes=VMEM_LIMIT),
        name="adaln_mod",
    )(c, w_ada, b_ada.reshape(1, n))


def _proj_kernel(x_ref, mod_ref, g1_ref, wqk_ref, wvt_ref, wih_ref, wil_ref, wg_ref, grp_ref,
                 gains_ref,
                 qa_ref, ka_ref, qb_ref, kb_ref, vat_ref, vbt_ref, qcat_ref, kcat_ref, wi_ref,
                 gates_ref):
    x = x_ref[...]
    ms = jnp.mean(x * x, axis=-1, keepdims=True)
    h = x * lax.rsqrt(ms + RMS_EPS) * g1_ref[...]
    h = h * (1.0 + mod_ref[1:2, :]) + mod_ref[0:1, :]
    h_hi, h_lo = _split(h)

    qk = _dot(h_hi, wqk_ref[...])
    vt = _dot_nt(wvt_ref[...], h_hi)
    pidx = (_dot(h_hi, wih_ref[...]) + _dot(h_hi, wil_ref[...]) + _dot(h_lo, wih_ref[...]))
    gate_logits = _dot(h_hi, wg_ref[...])

    grp = grp_ref[...]
    zs = [qk[:, i * W_A:(i + 1) * W_A] for i in range(4)]
    squares = [(z * z).astype(BF16) for z in zs]
    means = [_dot(sq, grp) * (1.0 / D_A) for sq in squares]
    for i, out_ref in enumerate((qa_ref, ka_ref, qb_ref, kb_ref)):
        out_ref[...] = (zs[i] * lax.rsqrt(means[i] + RMS_EPS) * gains_ref[i:i + 1, :]).astype(BF16)

    for s in range(x.shape[0] // KT):
        vat_ref[s] = vt[0:W_A, s * KT:(s + 1) * KT].astype(BF16)
        vbt_ref[s] = vt[W_A:, s * KT:(s + 1) * KT].astype(BF16)

    low_half = lax.broadcasted_iota(I32, (x.shape[0], 128), 1) < D_I
    for g in range(H_I // 2):
        pair = pidx[:, g * 128:(g + 1) * 128]
        hi = pair.astype(BF16).astype(F32)
        lo = pair - hi
        hi_sw = pltpu.roll(hi, D_I, 1)
        lo_sw = pltpu.roll(lo, D_I, 1)
        base = 2 * g * 256
        qcat_ref[:, base:base + 128] = jnp.where(low_half, hi, lo_sw).astype(BF16)
        qcat_ref[:, base + 128:base + 256] = jnp.where(low_half, hi, 0.0).astype(BF16)
        qcat_ref[:, base + 256:base + 384] = jnp.where(low_half, hi_sw, lo).astype(BF16)
        qcat_ref[:, base + 384:base + 512] = jnp.where(low_half, hi_sw, 0.0).astype(BF16)
    tail = pidx[:, 256:384]
    hi = tail.astype(BF16).astype(F32)
    lo = tail - hi
    kcat_ref[:, 0:128] = jnp.where(low_half, hi, pltpu.roll(hi, D_I, 1)).astype(BF16)
    kcat_ref[:, 128:256] = jnp.where(low_half, lo, 0.0).astype(BF16)
    wi_ref[...] = tail

    gates_ref[...] = jax.nn.sigmoid(gate_logits).astype(BF16)


def _projection(x2, mod3, norm1_g, w_in, qn_a, kn_a, qn_b, kn_b, S):
    N = x2.shape[0]
    tm = TM_PROJ
    per_batch = S // tm
    w_qk = jnp.concatenate([w_in[:, 0:2 * W_A], w_in[:, 3 * W_A:3 * W_A + 2 * W_B]],
                           axis=1).astype(BF16)
    w_vt = jnp.concatenate([w_in[:, 2 * W_A:3 * W_A], w_in[:, 3 * W_A + 2 * W_B:W_QKV]],
                           axis=1).T.astype(BF16)
    w_idx = jnp.pad(w_in[:, W_QKV:W_QKV + W_IDX], ((0, 0), (0, W_IDX_PAD - W_IDX)))
    w_idx_hi = w_idx.astype(BF16)
    w_idx_lo = (w_idx - w_idx_hi.astype(F32)).astype(BF16)
    w_g = w_in[:, W_QKV + W_IDX:].astype(BF16)
    grp = jnp.asarray(np.kron(np.eye(W_A // D_A), np.ones((D_A, D_A))), BF16)
    gains = jnp.stack([jnp.tile(qn_a, H_A) * (LOG2E * D_A ** -0.5), jnp.tile(kn_a, H_A),
                       jnp.tile(qn_b, 2 * H_B) * (LOG2E * D_B ** -0.5), jnp.tile(kn_b, 2 * H_B)])

    def row_spec(w):
        return pl.BlockSpec((tm, w), lambda t: (t, 0))

    def rows(w, dt):
        return row_spec(w), jax.ShapeDtypeStruct((N, w), dt)

    def transposed(w):
        return (pl.BlockSpec((tm // KT, w, KT), lambda t: (t, 0, 0)),
                jax.ShapeDtypeStruct((N // KT, w, KT), BF16))

    outs = [rows(W_A, BF16), rows(W_A, BF16), rows(W_B, BF16), rows(W_B, BF16),
            transposed(W_A), transposed(W_B),
            rows(H_I * 256, BF16), rows(256, BF16), rows(128, F32), rows(W_G, BF16)]
    return pl.pallas_call(
        _proj_kernel,
        grid=(N // tm,),
        in_specs=[row_spec(D_MODEL),
                  pl.BlockSpec((None, 6, D_MODEL), lambda t: (t // per_batch, 0, 0)),
                  _const_spec((1, D_MODEL)),
                  _const_spec((D_MODEL, 2 * W_A + 2 * W_B)),
                  _const_spec((W_A + W_B, D_MODEL)),
                  _const_spec((D_MODEL, W_IDX_PAD)),
                  _const_spec((D_MODEL, W_IDX_PAD)),
                  _const_spec((D_MODEL, W_G)),
                  _const_spec((W_A, W_A)),
                  _const_spec((4, W_A))],
        out_specs=[o[0] for o in outs],
        out_shape=[o[1] for o in outs],
        compiler_params=pltpu.CompilerParams(dimension_semantics=("parallel",),
                                             vmem_limit_bytes=VMEM_LIMIT),
        name="in_proj",
    )(x2, mod3, norm1_g.reshape(1, D_MODEL), w_qk, w_vt, w_idx_hi, w_idx_lo, w_g, grp, gains)


def _count_ge(ref, n_rows, cand):
    accs = [jnp.zeros((8, TQ), F32) for _ in range(CNT_ACCS)]
    for r in range(n_rows // 8):
        blk = ref[r * 8:(r + 1) * 8, :]
        accs[r % CNT_ACCS] = accs[r % CNT_ACCS] + jnp.where(blk >= cand, 1.0, 0.0)
    return jnp.sum(functools.reduce(lambda a, b: a + b, accs), axis=0, keepdims=True)


def _count_ge_bf16(ref, n_rows, cand):
    rows = 16
    one, zero = jnp.ones((), BF16), jnp.zeros((), BF16)

    assert n_rows // rows <= 256
    accs = [jnp.zeros((rows, TQ), BF16) for _ in range(CNT_ACCS)]
    for r in range(n_rows // rows):
        blk = ref[r * rows:(r + 1) * rows, :]
        accs[r % CNT_ACCS] = accs[r % CNT_ACCS] + jnp.where(blk >= cand, one, zero)
    total = functools.reduce(lambda a, b: a + b, [a.astype(F32) for a in accs])
    return jnp.sum(total, axis=0, keepdims=True)


def _rank_to_f32(rank):
    key = rank ^ jnp.int32(-2 ** 31)
    return pltpu.bitcast(jnp.where(key < 0, key ^ jnp.int32(0x7FFFFFFF), key), F32)


def _store_bias(bias_ref, bias_t, L):
    bias_ref[0:L, :] = bias_t


def _select_bias(qb, qcat_ref, kcat_ref, wt_ref, score_ref, coarse_ref, eqi_ref, bias_ref):
    L = (qb + 1) * TQ
    admissible = ((lax.broadcasted_iota(I32, (KT, TQ), 0) // CHUNK)
                  <= (lax.broadcasted_iota(I32, (KT, TQ), 1) // CHUNK))
    if L <= TOPK_MAX:
        _store_bias(bias_ref, jnp.where(admissible, 0.0, MASK_VALUE), L)
        return

    for j in range(L // KT):
        rows = slice(j * KT, (j + 1) * KT)
        score = jnp.zeros((KT, TQ), F32)
        for h in range(H_I):
            logits = _dot_nt(kcat_ref[rows, :], qcat_ref[:, h * 256:(h + 1) * 256])
            score = score + wt_ref[h:h + 1, :] * jnp.maximum(logits, 0.0)
        if j == qb:
            score = jnp.where(admissible, score, -jnp.inf)
        score_ref[rows, :] = score
        coarse_ref[rows, :] = score.astype(BF16)
    k_sel = float(TOPK_MAX)

    def coarse_body(it, rank):
        cand = rank | lax.shift_left(jnp.int32(1), 31 - it)
        c = _count_ge_bf16(coarse_ref, L, _rank_to_f32(cand).astype(BF16))
        return jnp.where(c >= k_sel, cand, rank)
    base = lax.fori_loop(0, 16, coarse_body, jnp.zeros((1, TQ), I32)) - jnp.int32(1 << 16)

    def fine_body(it, st):
        off, cnt = st
        cand = off | lax.shift_left(jnp.int32(1), 16 - it)
        c = _count_ge(score_ref, L, _rank_to_f32(base + cand))
        take = c >= k_sel
        return jnp.where(take, cand, off), jnp.where(take, c, cnt)
    off, cnt = lax.fori_loop(
        0, 17, fine_body, (jnp.zeros((1, TQ), I32), jnp.full((1, TQ), float(L), F32)))
    rank = base + off
    thr = _rank_to_f32(rank)
    tied = jnp.max(cnt) > k_sel

    @pl.when(jnp.logical_not(tied))
    def _():
        _store_bias(bias_ref, jnp.where(score_ref[0:L, :] >= thr, 0.0, MASK_VALUE), L)

    @pl.when(tied)
    def _():
        above = _rank_to_f32(rank + 1)
        need = k_sel - _count_ge(score_ref, L, above)
        big = jnp.int32(1 << 30)
        sc = score_ref[0:L, :]
        s_idx = lax.broadcasted_iota(I32, (L, TQ), 0)
        eqi_ref[0:L, :] = jnp.where(sc >= above, big, jnp.where(sc >= thr, s_idx, big))
        n_bits = max(1, int(np.ceil(np.log2(L))))

        def idx_body(it, j):
            cand = j | lax.shift_left(jnp.int32(1), n_bits - 1 - it)
            cnt_lt = float(L) - _count_ge(eqi_ref, L, cand)
            return jnp.where(cnt_lt < need, cand, j)
        j_sel = lax.fori_loop(0, n_bits, idx_body, jnp.zeros((1, TQ), I32))
        sel = jnp.where(score_ref[0:L, :] >= above, 0.0,
                        jnp.where(eqi_ref[0:L, :] <= j_sel, 0.0, MASK_VALUE))
        _store_bias(bias_ref, sel, L)


def _tile_offsets():
    return (lax.broadcasted_iota(I32, (KT, TQ), 1)
            - lax.broadcasted_iota(I32, (KT, TQ), 0)).astype(F32)


def _masked_halves(q_ref, n_groups):
    lane = lax.broadcasted_iota(I32, (TQ, 128), 1)
    low = jnp.where(lane < 64, 1.0, 0.0)
    high = 1.0 - low
    out = []
    for g in range(n_groups):
        q2 = q_ref[:, g * 128:(g + 1) * 128].astype(F32)
        out += [(q2 * low).astype(BF16), (q2 * high).astype(BF16)]
    return out


ONES_ROWS = 16


def _with_ones(vt):
    return jnp.concatenate([vt, jnp.ones((ONES_ROWS, vt.shape[1]), BF16)], axis=0)


def _online_steps(scores, values, maxes, accs):
    ss = [f() for f in scores]
    out = []
    for s, v, m, (acc_ref, i) in zip(ss, values, maxes, accs):
        m_new = jnp.maximum(m, jnp.max(s, axis=0, keepdims=True))
        p = jnp.exp2(s - m_new).astype(BF16)
        acc_ref[i] = jnp.exp2(m - m_new) * acc_ref[i] + _dot(_with_ones(v()), p)
        out.append(m_new)
    return tuple(out)


def _softmax_init(*acc_refs):
    for acc_ref in acc_refs:
        acc_ref[...] = jnp.zeros(acc_ref.shape, F32)
    n = sum(r.shape[0] for r in acc_refs)
    return tuple(jnp.full((1, TQ), -jnp.inf, F32) for _ in range(n))


def _softmax_result(acc_ref, i):
    rows = acc_ref.shape[1] - ONES_ROWS
    return acc_ref[i, 0:rows, :] / acc_ref[i, rows:rows + 1, :]


def _key_rows(j):
    return pl.ds(pl.multiple_of(j * KT, KT), KT)


def _split3(x):
    a = x.astype(BF16)
    r = x - a.astype(F32)
    b = r.astype(BF16)
    return a, b, (r - b.astype(F32)).astype(BF16)


def _alibi_tables(S, slopes):
    pos = jnp.arange(S, dtype=I32)
    hi = (pos // 64 * 64).astype(F32)
    lo = (pos % 64).astype(F32)
    lam = jnp.full((S,), LOG2E, F32)
    k_cols = [*_split3(lam), *_split3(lam), *_split3(lam * hi), *_split3(lam * lo)]
    kaug = jnp.pad(jnp.stack(k_cols, axis=1), ((0, 0), (0, 128 - len(k_cols))))
    qaug = []
    for sl in slopes:
        q_cols = [-sl * hi] * 3 + [-sl * lo] * 3 + [jnp.full((S,), sl, F32)] * 6
        qaug.append(jnp.pad(jnp.stack(q_cols, axis=1), ((0, 0), (0, 128 - len(q_cols)))))
    return kaug, jnp.stack(qaug).astype(BF16)


def _attn_kernel(qb, qcat_ref, kcat_ref, wt_ref, qa_ref, ka_ref, vat_ref, kaug_ref, qaug_a_ref,
                 lam_ref, sg_ref, qb_ref, kb_ref, vbt_ref, qaug_b_ref, oa_ref, ob_ref,
                 score_ref, coarse_ref, eqi_ref, bias_ref, acc_a_ref, acc_b_ref):
    _select_bias(qb, qcat_ref, kcat_ref, wt_ref, score_ref, coarse_ref, eqi_ref, bias_ref)
    n_b = 2 * H_B
    qms_a = _masked_halves(qa_ref, H_A // 2)
    qms_b = _masked_halves(qb_ref, H_B)
    qx_a = [jnp.concatenate([qms_a[h], qaug_a_ref[h]], axis=1) for h in range(H_A)]
    qx_b = [jnp.concatenate([qms_b[i], qaug_b_ref[i // 2]], axis=1) for i in range(n_b)]
    accs = [(acc_a_ref, h) for h in range(H_A)] + [(acc_b_ref, i) for i in range(n_b)]

    def values(j):
        return ([lambda h=h: vat_ref[j, h * D_A:(h + 1) * D_A, :] for h in range(H_A)]
                + [lambda i=i: vbt_ref[j, (i // 2) * 128:(i // 2 + 1) * 128, :]
                   for i in range(n_b)])

    def off_diagonal(j, maxes):
        rows = _key_rows(j)
        bias_t = bias_ref[rows, :]
        kaug_t = kaug_ref[rows, :]
        kx_a = [jnp.concatenate([ka_ref[rows, g * 128:(g + 1) * 128], kaug_t], axis=1)
                for g in range(H_A // 2)]
        kx_b = [jnp.concatenate([kb_ref[rows, h * 128:(h + 1) * 128], kaug_t], axis=1)
                for h in range(H_B)]
        scores = ([lambda h=h: _dot_nt(kx_a[h // 2], qx_a[h]) + bias_t for h in range(H_A)]
                  + [lambda i=i: _dot_nt(kx_b[i // 2], qx_b[i]) for i in range(n_b)])
        return _online_steps(scores, values(j), maxes, accs)

    maxes = lax.fori_loop(0, qb, off_diagonal, _softmax_init(acc_a_ref, acc_b_ref),
                          unroll=ATTN_UNROLL)

    diag = slice(qb * KT, (qb + 1) * KT)
    bias_t = bias_ref[diag, :]
    dist = jnp.abs(_tile_offsets())
    allowed = ((lax.broadcasted_iota(I32, (KT, TQ), 0) // CHUNK)
               <= (lax.broadcasted_iota(I32, (KT, TQ), 1) // CHUNK))
    biases_b = [jnp.where(allowed, -(LOG2E * sl) * dist, MASK_VALUE) for sl in SLOPES_B]
    scores = ([lambda h=h: (_dot_nt(ka_ref[diag, (h // 2) * 128:(h // 2 + 1) * 128], qms_a[h])
                            + (bias_t - (LOG2E * SLOPES_A[h]) * dist)) for h in range(H_A)]
              + [lambda i=i: (_dot_nt(kb_ref[diag, (i // 2) * 128:(i // 2 + 1) * 128], qms_b[i])
                              + biases_b[i // 2]) for i in range(n_b)])
    _online_steps(scores, values(qb), maxes, accs)

    out_t = jnp.concatenate([_softmax_result(acc_a_ref, h) for h in range(H_A)], axis=0)
    oa_ref[...] = out_t.T.astype(BF16)

    lam = (jnp.exp(jnp.sum(lam_ref[0:1, :] * lam_ref[1:2, :], axis=-1, keepdims=True))
           - jnp.exp(jnp.sum(lam_ref[2:3, :] * lam_ref[3:4, :], axis=-1, keepdims=True))
           + LAMBDA_INIT)
    ys = []
    for h in range(H_B):
        o = _softmax_result(acc_b_ref, 2 * h) - lam * _softmax_result(acc_b_ref, 2 * h + 1)
        ys.append(o * lax.rsqrt(jnp.mean(o * o, axis=0, keepdims=True) + RMS_EPS))
    y = jnp.concatenate(ys, axis=0).T
    ob_ref[...] = (y * sg_ref[...] * (1.0 - LAMBDA_INIT)).astype(BF16)


def _attn_block(qb, qcat, kcat, wt, qa, ka, vat, kaug, qaug_a, lam4, subln_g, qbn, kbn, vbt,
                qaug_b):
    B, S = qa.shape[0], qa.shape[1]
    L = (qb + 1) * TQ

    def q_rows(w):
        return pl.BlockSpec((None, TQ, w), lambda b: (b, qb, 0))

    def key_rows(w):
        return pl.BlockSpec((None, L, w), lambda b: (b, 0, 0))

    def value_tiles(w):
        return pl.BlockSpec((None, qb + 1, w, KT), lambda b: (b, 0, 0, 0))

    def qaug_spec(h):
        return pl.BlockSpec((h, TQ, 128), lambda b: (0, qb, 0))

    return pl.pallas_call(
        functools.partial(_attn_kernel, qb),
        grid=(B,),
        in_specs=[q_rows(H_I * 256), key_rows(256),
                  pl.BlockSpec((None, H_I, TQ), lambda b: (b, 0, qb)),
                  q_rows(W_A), key_rows(W_A), value_tiles(W_A),
                  pl.BlockSpec((L, 128), lambda b: (0, 0)), qaug_spec(H_A),
                  pl.BlockSpec((4, D_B), lambda b: (0, 0)),
                  pl.BlockSpec((1, W_B), lambda b: (0, 0)),
                  q_rows(W_B), key_rows(W_B), value_tiles(W_B), qaug_spec(H_B)],
        out_specs=[q_rows(W_A), q_rows(W_B)],
        out_shape=[jax.ShapeDtypeStruct(qa.shape, BF16), jax.ShapeDtypeStruct(qbn.shape, BF16)],
        input_output_aliases={3: 0, 10: 1},
        scratch_shapes=[pltpu.VMEM((S, TQ), F32), pltpu.VMEM((S, TQ), BF16),
                        pltpu.VMEM((S, TQ), I32), pltpu.VMEM((S, TQ), F32),
                        pltpu.VMEM((H_A, D_A + ONES_ROWS, TQ), F32),
                        pltpu.VMEM((2 * H_B, 2 * D_B + ONES_ROWS, TQ), F32)],
        compiler_params=pltpu.CompilerParams(dimension_semantics=("parallel",),
                                             vmem_limit_bytes=VMEM_LIMIT),
        name=f"attn_q{qb}",
    )(qcat, kcat, wt, qa, ka, vat, kaug, qaug_a,
      lam4, jnp.tile(subln_g, H_B).reshape(1, W_B), qbn, kbn, vbt, qaug_b)


def _ffn_kernel(x_ref, mod_ref, ya_ref, yb_ref, gates_ref, wua_ref, wub_ref, wo_ref, g2_ref,
                w1_ref, w3_ref, w2_ref, o_ref):
    gates = gates_ref[...].astype(F32)
    merged = (gates[:, :D_MODEL] * _dot(ya_ref[...], wua_ref[...])
              + gates[:, D_MODEL:] * _dot(yb_ref[...], wub_ref[...]))
    x1 = x_ref[...] + mod_ref[2:3, :] * _dot(merged.astype(BF16), wo_ref[...])
    ms = jnp.mean(x1 * x1, axis=-1, keepdims=True)
    h2 = x1 * lax.rsqrt(ms + RMS_EPS) * g2_ref[...]
    h2 = (h2 * (1.0 + mod_ref[4:5, :]) + mod_ref[3:4, :]).astype(BF16)
    u = _dot(h2, w1_ref[...])
    act = (u * jax.nn.sigmoid(u) * _dot(h2, w3_ref[...])).astype(BF16)
    o_ref[...] = x1 + mod_ref[5:6, :] * _dot(act, w2_ref[...])


def _merge_ffn(x2, mod3, ya, yb, gates, w_up_a, w_up_b, w_o, norm2_g, w_ff1, w_ff3, w_ff2, S):
    N = x2.shape[0]
    tm = TM_FFN
    per_batch = S // tm

    def row_spec(w):
        return pl.BlockSpec((tm, w), lambda t: (t, 0))

    return pl.pallas_call(
        _ffn_kernel,
        grid=(N // tm,),
        in_specs=[row_spec(D_MODEL),
                  pl.BlockSpec((None, 6, D_MODEL), lambda t: (t // per_batch, 0, 0)),
                  row_spec(W_A), row_spec(W_B), row_spec(W_G),
                  _const_spec((W_A, D_MODEL)), _const_spec((W_B, D_MODEL)),
                  _const_spec((D_MODEL, D_MODEL)), _const_spec((1, D_MODEL)),
                  _const_spec((D_MODEL, D_FF)), _const_spec((D_MODEL, D_FF)),
                  _const_spec((D_FF, D_MODEL))],
        out_specs=row_spec(D_MODEL),
        out_shape=jax.ShapeDtypeStruct((N, D_MODEL), F32),
        compiler_params=pltpu.CompilerParams(dimension_semantics=("parallel",),
                                             vmem_limit_bytes=VMEM_LIMIT),
        name="merge_ffn",
    )(x2, mod3, ya, yb, gates, w_up_a.astype(BF16), w_up_b.astype(BF16), w_o.astype(BF16),
      norm2_g.reshape(1, D_MODEL), w_ff1.astype(BF16), w_ff3.astype(BF16), w_ff2.astype(BF16))


def kernel(x, c, w_ada, b_ada, norm1_g, w_in, qn_a, kn_a, qn_b, kn_b, lam_q1, lam_k1, lam_q2,
           lam_k2, subln_g, w_up_a, w_up_b, w_o, norm2_g, w_ff1, w_ff3, w_ff2):
    B, S, D = x.shape
    assert D == D_MODEL and S % TQ == 0 and w_ada.shape[0] == 1
    N = B * S
    x2 = x.reshape(N, D)
    mod3 = _modulation(c, w_ada[0], b_ada[0]).reshape(B, 6, D)

    (qa, ka, qbn, kbn, vat, vbt, qcat, kcat, wi, gates) = _projection(
        x2, mod3, norm1_g[0], w_in[0], qn_a[0], kn_a[0], qn_b[0], kn_b[0], S)
    r3 = lambda a: a.reshape(B, S, a.shape[-1])
    qa, ka, qbn, kbn, qcat, kcat = map(r3, (qa, ka, qbn, kbn, qcat, kcat))
    vat, vbt = (a.reshape(B, S // KT, a.shape[1], KT) for a in (vat, vbt))
    wt = wi[:, D_I:D_I + H_I].reshape(B, S, H_I).transpose(0, 2, 1)
    lam4 = jnp.stack([lam_q1[0], lam_k1[0], lam_q2[0], lam_k2[0]])

    kaug, qaug_a = _alibi_tables(S, SLOPES_A)
    _, qaug_b = _alibi_tables(S, SLOPES_B)
    ya, yb = qa, qbn
    for qb in range(S // TQ):
        ya, yb = _attn_block(qb, qcat, kcat, wt, ya, ka, vat, kaug, qaug_a,
                             lam4, subln_g[0], yb, kbn, vbt, qaug_b)
    ya = ya.reshape(N, W_A)
    yb = yb.reshape(N, W_B)

    out = _merge_ffn(x2, mod3, ya, yb, gates, w_up_a[0], w_up_b[0], w_o[0], norm2_g[0],
                     w_ff1[0], w_ff3[0], w_ff2[0], S)
    return out.reshape(B, S, D)
```

```python
import functools

import numpy as np
import jax
import jax.numpy as jnp
from jax import lax
from jax.experimental import pallas as pl
from jax.experimental.pallas import tpu as pltpu

F32 = jnp.float32
BF16 = jnp.bfloat16
I32 = jnp.int32

D_MODEL = 1024
CHUNK = 64
H_A, D_A = 8, 64
H_I, D_I = 4, 64
TOPK_MAX = 256
H_B, D_B = 4, 64
D_FF = 2816
RMS_EPS = 1e-6
MASK_VALUE = -1e30
LAMBDA_INIT = 0.2
LOG2E = 1.4426950408889634
SLOPES_A = [2.0 ** (-8.0 * (h + 1) / H_A) for h in range(H_A)]
SLOPES_B = [2.0 ** (-8.0 * (h + 1) / H_B) for h in range(H_B)]

W_A = H_A * D_A
W_B = H_B * 2 * D_B
W_QKV = 3 * W_A + 3 * W_B
W_IDX = H_I * D_I + D_I + H_I
W_IDX_PAD = 384
W_G = 2 * D_MODEL

TQ = 256
KT = TQ
TM_PROJ = 2 * KT
TM_FFN = 512
CNT_ROWS = 256
CNT_ACCS = 4
ATTN_UNROLL = True
VMEM_LIMIT = 56 * 1024 * 1024

NT_DIMS = (((1,), (1,)), ((), ()))


def _dot(a, b):
    return jnp.dot(a, b, preferred_element_type=F32)


def _dot_nt(a, b):
    return lax.dot_general(a, b, NT_DIMS, preferred_element_type=F32)


def _split(x):
    hi = x.astype(BF16)
    lo = (x - hi.astype(F32)).astype(BF16)
    return hi, lo


def _dot3(a, b_hi, b_lo):
    a_hi, a_lo = _split(a)
    return _dot(a_hi, b_hi) + _dot(a_hi, b_lo) + _dot(a_lo, b_hi)


def _const_spec(shape):
    nd = len(shape)
    return pl.BlockSpec(shape, lambda *_: (0,) * nd, pipeline_mode=pl.Buffered(1))


def _mod_kernel(c_ref, w_ref, b_ref, o_ref):
    c = c_ref[...]
    cs = c * jax.nn.sigmoid(c)
    w_hi, w_lo = _split(w_ref[...])
    o_ref[...] = _dot3(cs, w_hi, w_lo) + b_ref[...]


def _modulation(c, w_ada, b_ada):
    B = c.shape[0]
    n = w_ada.shape[1]
    tn = 1024
    return pl.pallas_call(
        _mod_kernel,
        grid=(n // tn,),
        in_specs=[pl.BlockSpec((B, D_MODEL), lambda j: (0, 0)),
                  pl.BlockSpec((D_MODEL, tn), lambda j: (0, j)),
                  pl.BlockSpec((1, tn), lambda j: (0, j))],
        out_specs=pl.BlockSpec((B, tn), lambda j: (0, j)),
        out_shape=jax.ShapeDtypeStruct((B, n), F32),
        compiler_params=pltpu.CompilerParams(vmem_limit_bytes=VMEM_LIMIT),
        name="adaln_mod",
    )(c, w_ada, b_ada.reshape(1, n))


def _proj_kernel(x_ref, mod_ref, g1_ref, wqk_ref, wvt_ref, wih_ref, wil_ref, wg_ref, grp_ref,
                 gains_ref,
                 qa_ref, ka_ref, qb_ref, kb_ref, vat_ref, vbt_ref, qcat_ref, kcat_ref, wi_ref,
                 gates_ref):
    x = x_ref[...]
    ms = jnp.mean(x * x, axis=-1, keepdims=True)
    h = x * lax.rsqrt(ms + RMS_EPS) * g1_ref[...]
    h = h * (1.0 + mod_ref[1:2, :]) + mod_ref[0:1, :]
    h_hi, h_lo = _split(h)

    qk = _dot(h_hi, wqk_ref[...])
    vt = _dot_nt(wvt_ref[...], h_hi)
    pidx = (_dot(h_hi, wih_ref[...]) + _dot(h_hi, wil_ref[...]) + _dot(h_lo, wih_ref[...]))
    gate_logits = _dot(h_hi, wg_ref[...])

    grp = grp_ref[...]
    zs = [qk[:, i * W_A:(i + 1) * W_A] for i in range(4)]
    squares = [(z * z).astype(BF16) for z in zs]
    means = [_dot(sq, grp) * (1.0 / D_A) for sq in squares]
    for i, out_ref in enumerate((qa_ref, ka_ref, qb_ref, kb_ref)):
        out_ref[...] = (zs[i] * lax.rsqrt(means[i] + RMS_EPS) * gains_ref[i:i + 1, :]).astype(BF16)

    for s in range(x.shape[0] // KT):
        vat_ref[s] = vt[0:W_A, s * KT:(s + 1) * KT].astype(BF16)
        vbt_ref[s] = vt[W_A:, s * KT:(s + 1) * KT].astype(BF16)

    low_half = lax.broadcasted_iota(I32, (x.shape[0], 128), 1) < D_I
    for g in range(H_I // 2):
        pair = pidx[:, g * 128:(g + 1) * 128]
        hi = pair.astype(BF16).astype(F32)
        lo = pair - hi
        hi_sw = pltpu.roll(hi, D_I, 1)
        lo_sw = pltpu.roll(lo, D_I, 1)
        base = 2 * g * 256
        qcat_ref[:, base:base + 128] = jnp.where(low_half, hi, lo_sw).astype(BF16)
        qcat_ref[:, base + 128:base + 256] = jnp.where(low_half, hi, 0.0).astype(BF16)
        qcat_ref[:, base + 256:base + 384] = jnp.where(low_half, hi_sw, lo).astype(BF16)
        qcat_ref[:, base + 384:base + 512] = jnp.where(low_half, hi_sw, 0.0).astype(BF16)
    tail = pidx[:, 256:384]
    hi = tail.astype(BF16).astype(F32)
    lo = tail - hi
    kcat_ref[:, 0:128] = jnp.where(low_half, hi, pltpu.roll(hi, D_I, 1)).astype(BF16)
    kcat_ref[:, 128:256] = jnp.where(low_half, lo, 0.0).astype(BF16)
    wi_ref[...] = tail

    gates_ref[...] = jax.nn.sigmoid(gate_logits).astype(BF16)


def _projection(x2, mod3, norm1_g, w_in, qn_a, kn_a, qn_b, kn_b, S):
    N = x2.shape[0]
    tm = TM_PROJ
    per_batch = S // tm
    w_qk = jnp.concatenate([w_in[:, 0:2 * W_A], w_in[:, 3 * W_A:3 * W_A + 2 * W_B]],
                           axis=1).astype(BF16)
    w_vt = jnp.concatenate([w_in[:, 2 * W_A:3 * W_A], w_in[:, 3 * W_A + 2 * W_B:W_QKV]],
                           axis=1).T.astype(BF16)
    w_idx = jnp.pad(w_in[:, W_QKV:W_QKV + W_IDX], ((0, 0), (0, W_IDX_PAD - W_IDX)))
    w_idx_hi = w_idx.astype(BF16)
    w_idx_lo = (w_idx - w_idx_hi.astype(F32)).astype(BF16)
    w_g = w_in[:, W_QKV + W_IDX:].astype(BF16)
    grp = jnp.asarray(np.kron(np.eye(W_A // D_A), np.ones((D_A, D_A))), BF16)
    gains = jnp.stack([jnp.tile(qn_a, H_A) * (LOG2E * D_A ** -0.5), jnp.tile(kn_a, H_A),
                       jnp.tile(qn_b, 2 * H_B) * (LOG2E * D_B ** -0.5), jnp.tile(kn_b, 2 * H_B)])

    def row_spec(w):
        return pl.BlockSpec((tm, w), lambda t: (t, 0))

    def rows(w, dt):
        return row_spec(w), jax.ShapeDtypeStruct((N, w), dt)

    def transposed(w):
        return (pl.BlockSpec((tm // KT, w, KT), lambda t: (t, 0, 0)),
                jax.ShapeDtypeStruct((N // KT, w, KT), BF16))

    outs = [rows(W_A, BF16), rows(W_A, BF16), rows(W_B, BF16), rows(W_B, BF16),
            transposed(W_A), transposed(W_B),
            rows(H_I * 256, BF16), rows(256, BF16), rows(128, F32), rows(W_G, BF16)]
    return pl.pallas_call(
        _proj_kernel,
        grid=(N // tm,),
        in_specs=[row_spec(D_MODEL),
                  pl.BlockSpec((None, 6, D_MODEL), lambda t: (t // per_batch, 0, 0)),
                  _const_spec((1, D_MODEL)),
                  _const_spec((D_MODEL, 2 * W_A + 2 * W_B)),
                  _const_spec((W_A + W_B, D_MODEL)),
                  _const_spec((D_MODEL, W_IDX_PAD)),
                  _const_spec((D_MODEL, W_IDX_PAD)),
                  _const_spec((D_MODEL, W_G)),
                  _const_spec((W_A, W_A)),
                  _const_spec((4, W_A))],
        out_specs=[o[0] for o in outs],
        out_shape=[o[1] for o in outs],
        compiler_params=pltpu.CompilerParams(dimension_semantics=("parallel",),
                                             vmem_limit_bytes=VMEM_LIMIT),
        name="in_proj",
    )(x2, mod3, norm1_g.reshape(1, D_MODEL), w_qk, w_vt, w_idx_hi, w_idx_lo, w_g, grp, gains)


def _count_ge(ref, n_rows, cand):
    accs = [jnp.zeros((8, TQ), F32) for _ in range(CNT_ACCS)]
    for r in range(n_rows // 8):
        blk = ref[r * 8:(r + 1) * 8, :]
        accs[r % CNT_ACCS] = accs[r % CNT_ACCS] + jnp.where(blk >= cand, 1.0, 0.0)
    return jnp.sum(functools.reduce(lambda a, b: a + b, accs), axis=0, keepdims=True)


def _count_ge_bf16(ref, n_rows, cand):
    rows = 16
    one, zero = jnp.ones((), BF16), jnp.zeros((), BF16)

    assert n_rows // rows <= 256
    accs = [jnp.zeros((rows, TQ), BF16) for _ in range(CNT_ACCS)]
    for r in range(n_rows // rows):
        blk = ref[r * rows:(r + 1) * rows, :]
        accs[r % CNT_ACCS] = accs[r % CNT_ACCS] + jnp.where(blk >= cand, one, zero)
    total = functools.reduce(lambda a, b: a + b, [a.astype(F32) for a in accs])
    return jnp.sum(total, axis=0, keepdims=True)


def _rank_to_f32(rank):
    key = rank ^ jnp.int32(-2 ** 31)
    return pltpu.bitcast(jnp.where(key < 0, key ^ jnp.int32(0x7FFFFFFF), key), F32)


def _store_bias(bias_ref, bias_t, L):
    bias_ref[0:L, :] = bias_t


def _select_bias(qb, qcat_ref, kcat_ref, wt_ref, score_ref, coarse_ref, eqi_ref, bias_ref):
    L = (qb + 1) * TQ
    admissible = ((lax.broadcasted_iota(I32, (KT, TQ), 0) // CHUNK)
                  <= (lax.broadcasted_iota(I32, (KT, TQ), 1) // CHUNK))
    if L <= TOPK_MAX:
        _store_bias(bias_ref, jnp.where(admissible, 0.0, MASK_VALUE), L)
        return

    for j in range(L // KT):
        rows = slice(j * KT, (j + 1) * KT)
        score = jnp.zeros((KT, TQ), F32)
        for h in range(H_I):
            logits = _dot_nt(kcat_ref[rows, :], qcat_ref[:, h * 256:(h + 1) * 256])
            score = score + wt_ref[h:h + 1, :] * jnp.maximum(logits, 0.0)
        if j == qb:
            score = jnp.where(admissible, score, -jnp.inf)
        score_ref[rows, :] = score
        coarse_ref[rows, :] = score.astype(BF16)
    k_sel = float(TOPK_MAX)

    def coarse_body(it, rank):
        cand = rank | lax.shift_left(jnp.int32(1), 31 - it)
        c = _count_ge_bf16(coarse_ref, L, _rank_to_f32(cand).astype(BF16))
        return jnp.where(c >= k_sel, cand, rank)
    base = lax.fori_loop(0, 16, coarse_body, jnp.zeros((1, TQ), I32)) - jnp.int32(1 << 16)

    def fine_body(it, st):
        off, cnt = st
        cand = off | lax.shift_left(jnp.int32(1), 16 - it)
        c = _count_ge(score_ref, L, _rank_to_f32(base + cand))
        take = c >= k_sel
        return jnp.where(take, cand, off), jnp.where(take, c, cnt)
    off, cnt = lax.fori_loop(
        0, 17, fine_body, (jnp.zeros((1, TQ), I32), jnp.full((1, TQ), float(L), F32)))
    rank = base + off
    thr = _rank_to_f32(rank)
    tied = jnp.max(cnt) > k_sel

    @pl.when(jnp.logical_not(tied))
    def _():
        _store_bias(bias_ref, jnp.where(score_ref[0:L, :] >= thr, 0.0, MASK_VALUE), L)

    @pl.when(tied)
    def _():
        above = _rank_to_f32(rank + 1)
        need = k_sel - _count_ge(score_ref, L, above)
        big = jnp.int32(1 << 30)
        sc = score_ref[0:L, :]
        s_idx = lax.broadcasted_iota(I32, (L, TQ), 0)
        eqi_ref[0:L, :] = jnp.where(sc >= above, big, jnp.where(sc >= thr, s_idx, big))
        n_bits = max(1, int(np.ceil(np.log2(L))))

        def idx_body(it, j):
            cand = j | lax.shift_left(jnp.int32(1), n_bits - 1 - it)
            cnt_lt = float(L) - _count_ge(eqi_ref, L, cand)
            return jnp.where(cnt_lt < need, cand, j)
        j_sel = lax.fori_loop(0, n_bits, idx_body, jnp.zeros((1, TQ), I32))
        sel = jnp.where(score_ref[0:L, :] >= above, 0.0,
                        jnp.where(eqi_ref[0:L, :] <= j_sel, 0.0, MASK_VALUE))
        _store_bias(bias_ref, sel, L)


def _tile_offsets():
    return (lax.broadcasted_iota(I32, (KT, TQ), 1)
            - lax.broadcasted_iota(I32, (KT, TQ), 0)).astype(F32)


def _masked_halves(q_ref, n_groups):
    lane = lax.broadcasted_iota(I32, (TQ, 128), 1)
    low = jnp.where(lane < 64, 1.0, 0.0)
    high = 1.0 - low
    out = []
    for g in range(n_groups):
        q2 = q_ref[:, g * 128:(g + 1) * 128].astype(F32)
        out += [(q2 * low).astype(BF16), (q2 * high).astype(BF16)]
    return out


ONES_ROWS = 16


def _with_ones(vt):
    return jnp.concatenate([vt, jnp.ones((ONES_ROWS, vt.shape[1]), BF16)], axis=0)


def _online_steps(scores, values, maxes, accs):
    ss = [f() for f in scores]
    out = []
    for s, v, m, (acc_ref, i) in zip(ss, values, maxes, accs):
        m_new = jnp.maximum(m, jnp.max(s, axis=0, keepdims=True))
        p = jnp.exp2((s - m_new).astype(BF16))
        acc_ref[i] = jnp.exp2(m - m_new) * acc_ref[i] + _dot(_with_ones(v()), p)
        out.append(m_new)
    return tuple(out)


def _softmax_init(*acc_refs):
    for acc_ref in acc_refs:
        acc_ref[...] = jnp.zeros(acc_ref.shape, F32)
    n = sum(r.shape[0] for r in acc_refs)
    return tuple(jnp.full((1, TQ), -jnp.inf, F32) for _ in range(n))


def _softmax_result(acc_ref, i):
    rows = acc_ref.shape[1] - ONES_ROWS
    return acc_ref[i, 0:rows, :] / acc_ref[i, rows:rows + 1, :]


def _key_rows(j):
    return pl.ds(pl.multiple_of(j * KT, KT), KT)


def _split3(x):
    a = x.astype(BF16)
    r = x - a.astype(F32)
    b = r.astype(BF16)
    return a, b, (r - b.astype(F32)).astype(BF16)


def _alibi_tables(S, slopes):
    pos = jnp.arange(S, dtype=I32)
    hi = (pos // 64 * 64).astype(F32)
    lo = (pos % 64).astype(F32)
    lam = jnp.full((S,), LOG2E, F32)
    k_cols = [*_split3(lam), *_split3(lam), *_split3(lam * hi), *_split3(lam * lo)]
    kaug = jnp.pad(jnp.stack(k_cols, axis=1), ((0, 0), (0, 128 - len(k_cols))))
    qaug = []
    for sl in slopes:
        q_cols = [-sl * hi] * 3 + [-sl * lo] * 3 + [jnp.full((S,), sl, F32)] * 6
        qaug.append(jnp.pad(jnp.stack(q_cols, axis=1), ((0, 0), (0, 128 - len(q_cols)))))
    return kaug, jnp.stack(qaug).astype(BF16)


def _attn_kernel(qb, qcat_ref, kcat_ref, wt_ref, qa_ref, ka_ref, vat_ref, kaug_ref, qaug_a_ref,
                 lam_ref, sg_ref, qb_ref, kb_ref, vbt_ref, qaug_b_ref, oa_ref, ob_ref,
                 score_ref, coarse_ref, eqi_ref, bias_ref, acc_a_ref, acc_b_ref):
    _select_bias(qb, qcat_ref, kcat_ref, wt_ref, score_ref, coarse_ref, eqi_ref, bias_ref)
    n_b = 2 * H_B
    qms_a = _masked_halves(qa_ref, H_A // 2)
    qms_b = _masked_halves(qb_ref, H_B)
    qx_a = [jnp.concatenate([qms_a[h], qaug_a_ref[h]], axis=1) for h in range(H_A)]
    qx_b = [jnp.concatenate([qms_b[i], qaug_b_ref[i // 2]], axis=1) for i in range(n_b)]
    accs = [(acc_a_ref, h) for h in range(H_A)] + [(acc_b_ref, i) for i in range(n_b)]

    def values(j):
        return ([lambda h=h: vat_ref[j, h * D_A:(h + 1) * D_A, :] for h in range(H_A)]
                + [lambda i=i: vbt_ref[j, (i // 2) * 128:(i // 2 + 1) * 128, :]
                   for i in range(n_b)])

    def off_diagonal(j, maxes):
        rows = _key_rows(j)
        bias_t = bias_ref[rows, :]
        kaug_t = kaug_ref[rows, :]
        kx_a = [jnp.concatenate([ka_ref[rows, g * 128:(g + 1) * 128], kaug_t], axis=1)
                for g in range(H_A // 2)]
        kx_b = [jnp.concatenate([kb_ref[rows, h * 128:(h + 1) * 128], kaug_t], axis=1)
                for h in range(H_B)]
        scores = ([lambda h=h: _dot_nt(kx_a[h // 2], qx_a[h]) + bias_t for h in range(H_A)]
                  + [lambda i=i: _dot_nt(kx_b[i // 2], qx_b[i]) for i in range(n_b)])
        return _online_steps(scores, values(j), maxes, accs)

    maxes = lax.fori_loop(0, qb, off_diagonal, _softmax_init(acc_a_ref, acc_b_ref),
                          unroll=ATTN_UNROLL)

    diag = slice(qb * KT, (qb + 1) * KT)
    bias_t = bias_ref[diag, :]
    dist = jnp.abs(_tile_offsets())
    allowed = ((lax.broadcasted_iota(I32, (KT, TQ), 0) // CHUNK)
               <= (lax.broadcasted_iota(I32, (KT, TQ), 1) // CHUNK))
    biases_b = [jnp.where(allowed, -(LOG2E * sl) * dist, MASK_VALUE) for sl in SLOPES_B]
    scores = ([lambda h=h: (_dot_nt(ka_ref[diag, (h // 2) * 128:(h // 2 + 1) * 128], qms_a[h])
                            + (bias_t - (LOG2E * SLOPES_A[h]) * dist)) for h in range(H_A)]
              + [lambda i=i: (_dot_nt(kb_ref[diag, (i // 2) * 128:(i // 2 + 1) * 128], qms_b[i])
                              + biases_b[i // 2]) for i in range(n_b)])
    _online_steps(scores, values(qb), maxes, accs)

    out_t = jnp.concatenate([_softmax_result(acc_a_ref, h) for h in range(H_A)], axis=0)
    oa_ref[...] = out_t.T.astype(BF16)

    lam = (jnp.exp(jnp.sum(lam_ref[0:1, :] * lam_ref[1:2, :], axis=-1, keepdims=True))
           - jnp.exp(jnp.sum(lam_ref[2:3, :] * lam_ref[3:4, :], axis=-1, keepdims=True))
           + LAMBDA_INIT)
    ys = []
    for h in range(H_B):
        o = _softmax_result(acc_b_ref, 2 * h) - lam * _softmax_result(acc_b_ref, 2 * h + 1)
        ys.append(o * lax.rsqrt(jnp.mean(o * o, axis=0, keepdims=True) + RMS_EPS))
    y = jnp.concatenate(ys, axis=0).T
    ob_ref[...] = (y * sg_ref[...] * (1.0 - LAMBDA_INIT)).astype(BF16)


def _attn_block(qb, qcat, kcat, wt, qa, ka, vat, kaug, qaug_a, lam4, subln_g, qbn, kbn, vbt,
                qaug_b):
    B, S = qa.shape[0], qa.shape[1]
    L = (qb + 1) * TQ

    def q_rows(w):
        return pl.BlockSpec((None, TQ, w), lambda b: (b, qb, 0))

    def key_rows(w):
        return pl.BlockSpec((None, L, w), lambda b: (b, 0, 0))

    def value_tiles(w):
        return pl.BlockSpec((None, qb + 1, w, KT), lambda b: (b, 0, 0, 0))

    def qaug_spec(h):
        return pl.BlockSpec((h, TQ, 128), lambda b: (0, qb, 0))

    return pl.pallas_call(
        functools.partial(_attn_kernel, qb),
        grid=(B,),
        in_specs=[q_rows(H_I * 256), key_rows(256),
                  pl.BlockSpec((None, H_I, TQ), lambda b: (b, 0, qb)),
                  q_rows(W_A), key_rows(W_A), value_tiles(W_A),
                  pl.BlockSpec((L, 128), lambda b: (0, 0)), qaug_spec(H_A),
                  pl.BlockSpec((4, D_B), lambda b: (0, 0)),
                  pl.BlockSpec((1, W_B), lambda b: (0, 0)),
                  q_rows(W_B), key_rows(W_B), value_tiles(W_B), qaug_spec(H_B)],
        out_specs=[q_rows(W_A), q_rows(W_B)],
        out_shape=[jax.ShapeDtypeStruct(qa.shape, BF16), jax.ShapeDtypeStruct(qbn.shape, BF16)],
        input_output_aliases={3: 0, 10: 1},
        scratch_shapes=[pltpu.VMEM((S, TQ), F32), pltpu.VMEM((S, TQ), BF16),
                        pltpu.VMEM((S, TQ), I32), pltpu.VMEM((S, TQ), F32),
                        pltpu.VMEM((H_A, D_A + ONES_ROWS, TQ), F32),
                        pltpu.VMEM((2 * H_B, 2 * D_B + ONES_ROWS, TQ), F32)],
        compiler_params=pltpu.CompilerParams(dimension_semantics=("parallel",),
                                             vmem_limit_bytes=VMEM_LIMIT),
        name=f"attn_q{qb}",
    )(qcat, kcat, wt, qa, ka, vat, kaug, qaug_a,
      lam4, jnp.tile(subln_g, H_B).reshape(1, W_B), qbn, kbn, vbt, qaug_b)


def _ffn_kernel(x_ref, mod_ref, ya_ref, yb_ref, gates_ref, wua_ref, wub_ref, wo_ref, g2_ref,
                w1_ref, w3_ref, w2_ref, o_ref):
    gates = gates_ref[...].astype(F32)
    merged = (gates[:, :D_MODEL] * _dot(ya_ref[...], wua_ref[...])
              + gates[:, D_MODEL:] * _dot(yb_ref[...], wub_ref[...]))
    x1 = x_ref[...] + mod_ref[2:3, :] * _dot(merged.astype(BF16), wo_ref[...])
    ms = jnp.mean(x1 * x1, axis=-1, keepdims=True)
    h2 = x1 * lax.rsqrt(ms + RMS_EPS) * g2_ref[...]
    h2 = (h2 * (1.0 + mod_ref[4:5, :]) + mod_ref[3:4, :]).astype(BF16)
    u = _dot(h2, w1_ref[...])
    act = (u * jax.nn.sigmoid(u) * _dot(h2, w3_ref[...])).astype(BF16)
    o_ref[...] = x1 + mod_ref[5:6, :] * _dot(act, w2_ref[...])


def _merge_ffn(x2, mod3, ya, yb, gates, w_up_a, w_up_b, w_o, norm2_g, w_ff1, w_ff3, w_ff2, S):
    N = x2.shape[0]
    tm = TM_FFN
    per_batch = S // tm

    def row_spec(w):
        return pl.BlockSpec((tm, w), lambda t: (t, 0))

    return pl.pallas_call(
        _ffn_kernel,
        grid=(N // tm,),
        in_specs=[row_spec(D_MODEL),
                  pl.BlockSpec((None, 6, D_MODEL), lambda t: (t // per_batch, 0, 0)),
                  row_spec(W_A), row_spec(W_B), row_spec(W_G),
                  _const_spec((W_A, D_MODEL)), _const_spec((W_B, D_MODEL)),
                  _const_spec((D_MODEL, D_MODEL)), _const_spec((1, D_MODEL)),
                  _const_spec((D_MODEL, D_FF)), _const_spec((D_MODEL, D_FF)),
                  _const_spec((D_FF, D_MODEL))],
        out_specs=row_spec(D_MODEL),
        out_shape=jax.ShapeDtypeStruct((N, D_MODEL), F32),
        compiler_params=pltpu.CompilerParams(dimension_semantics=("parallel",),
                                             vmem_limit_bytes=VMEM_LIMIT),
        name="merge_ffn",
    )(x2, mod3, ya, yb, gates, w_up_a.astype(BF16), w_up_b.astype(BF16), w_o.astype(BF16),
      norm2_g.reshape(1, D_MODEL), w_ff1.astype(BF16), w_ff3.astype(BF16), w_ff2.astype(BF16))


def kernel(x, c, w_ada, b_ada, norm1_g, w_in, qn_a, kn_a, qn_b, kn_b, lam_q1, lam_k1, lam_q2,
           lam_k2, subln_g, w_up_a, w_up_b, w_o, norm2_g, w_ff1, w_ff3, w_ff2):
    B, S, D = x.shape
    assert D == D_MODEL and S % TQ == 0 and w_ada.shape[0] == 1
    N = B * S
    x2 = x.reshape(N, D)
    mod3 = _modulation(c, w_ada[0], b_ada[0]).reshape(B, 6, D)

    (qa, ka, qbn, kbn, vat, vbt, qcat, kcat, wi, gates) = _projection(
        x2, mod3, norm1_g[0], w_in[0], qn_a[0], kn_a[0], qn_b[0], kn_b[0], S)
    r3 = lambda a: a.reshape(B, S, a.shape[-1])
    qa, ka, qbn, kbn, qcat, kcat = map(r3, (qa, ka, qbn, kbn, qcat, kcat))
    vat, vbt = (a.reshape(B, S // KT, a.shape[1], KT) for a in (vat, vbt))
    wt = wi[:, D_I:D_I + H_I].reshape(B, S, H_I).transpose(0, 2, 1)
    lam4 = jnp.stack([lam_q1[0], lam_k1[0], lam_q2[0], lam_k2[0]])

    kaug, qaug_a = _alibi_tables(S, SLOPES_A)
    _, qaug_b = _alibi_tables(S, SLOPES_B)
    ya, yb = qa, qbn
    for qb in range(S // TQ):
        ya, yb = _attn_block(qb, qcat, kcat, wt, ya, ka, vat, kaug, qaug_a,
                             lam4, subln_g[0], yb, kbn, vbt, qaug_b)
    ya = ya.reshape(N, W_A)
    yb = yb.reshape(N, W_B)

    out = _merge_ffn(x2, mod3, ya, yb, gates, w_up_a[0], w_up_b[0], w_o[0], norm2_g[0],
                     w_ff1[0], w_ff3[0], w_ff2[0], S)
    return out.reshape(B, S, D)
```

```python
import functools

import numpy as np
import jax
import jax.numpy as jnp
from jax import lax
from jax.experimental import pallas as pl
from jax.experimental.pallas import tpu as pltpu

F32 = jnp.float32
BF16 = jnp.bfloat16
I32 = jnp.int32

D_MODEL = 1024
CHUNK = 64
H_A, D_A = 8, 64
H_I, D_I = 4, 64
TOPK_MAX = 256
H_B, D_B = 4, 64
D_FF = 2816
RMS_EPS = 1e-6
MASK_VALUE = -1e30
LAMBDA_INIT = 0.2
LOG2E = 1.4426950408889634
SLOPES_A = [2.0 ** (-8.0 * (h + 1) / H_A) for h in range(H_A)]
SLOPES_B = [2.0 ** (-8.0 * (h + 1) / H_B) for h in range(H_B)]

W_A = H_A * D_A
W_B = H_B * 2 * D_B
W_QKV = 3 * W_A + 3 * W_B
W_IDX = H_I * D_I + D_I + H_I
W_IDX_PAD = 384
W_G = 2 * D_MODEL

LANES = 128
IDX_K = 2 * LANES
BF16_EXACT_INT = 256

TQ = 256
KT = TQ
TM_PROJ = 2 * KT
TM_FFN = 512
CNT_ACCS = 4
ATTN_UNROLL = True
VMEM_LIMIT = 56 * 1024 * 1024

NT_DIMS = (((1,), (1,)), ((), ()))


def _dot(a, b):
    return jnp.dot(a, b, preferred_element_type=F32)


def _dot_nt(a, b):
    return lax.dot_general(a, b, NT_DIMS, preferred_element_type=F32)


def _split(x):
    hi = x.astype(BF16)
    lo = (x - hi.astype(F32)).astype(BF16)
    return hi, lo


def _dot3(a, b_hi, b_lo):
    a_hi, a_lo = _split(a)
    return _dot(a_hi, b_hi) + _dot(a_hi, b_lo) + _dot(a_lo, b_hi)


def _const_spec(shape):
    nd = len(shape)
    return pl.BlockSpec(shape, lambda *_: (0,) * nd, pipeline_mode=pl.Buffered(1))


def _mod_kernel(c_ref, w_ref, b_ref, o_ref):
    c = c_ref[...]
    cs = c * jax.nn.sigmoid(c)
    w_hi, w_lo = _split(w_ref[...])
    o_ref[...] = _dot3(cs, w_hi, w_lo) + b_ref[...]


def _modulation(c, w_ada, b_ada):
    B = c.shape[0]
    n = w_ada.shape[1]
    tn = 1024
    return pl.pallas_call(
        _mod_kernel,
        grid=(n // tn,),
        in_specs=[pl.BlockSpec((B, D_MODEL), lambda j: (0, 0)),
                  pl.BlockSpec((D_MODEL, tn), lambda j: (0, j)),
                  pl.BlockSpec((1, tn), lambda j: (0, j))],
        out_specs=pl.BlockSpec((B, tn), lambda j: (0, j)),
        out_shape=jax.ShapeDtypeStruct((B, n), F32),
        compiler_params=pltpu.CompilerParams(vmem_limit_bytes=VMEM_LIMIT),
        name="adaln_mod",
    )(c, w_ada, b_ada.reshape(1, n))


def _proj_kernel(x_ref, mod_ref, g1_ref, wqk_ref, wvt_ref, wih_ref, wil_ref, wg_ref, grp_ref,
                 gains_ref,
                 qa_ref, ka_ref, qb_ref, kb_ref, vat_ref, vbt_ref, qcat_ref, kcat_ref, wi_ref,
                 gates_ref):
    x = x_ref[...]
    ms = jnp.mean(x * x, axis=-1, keepdims=True)
    h = x * lax.rsqrt(ms + RMS_EPS) * g1_ref[...]
    h = h * (1.0 + mod_ref[1:2, :]) + mod_ref[0:1, :]
    h_hi, h_lo = _split(h)

    qk = _dot(h_hi, wqk_ref[...])
    vt = _dot_nt(wvt_ref[...], h_hi)
    pidx = (_dot(h_hi, wih_ref[...]) + _dot(h_hi, wil_ref[...]) + _dot(h_lo, wih_ref[...]))
    gate_logits = _dot(h_hi, wg_ref[...])

    grp = grp_ref[...]
    zs = [qk[:, i * W_A:(i + 1) * W_A] for i in range(4)]
    squares = [(z * z).astype(BF16) for z in zs]
    means = [_dot(sq, grp) * (1.0 / D_A) for sq in squares]
    for i, out_ref in enumerate((qa_ref, ka_ref, qb_ref, kb_ref)):
        out_ref[...] = (zs[i] * lax.rsqrt(means[i] + RMS_EPS) * gains_ref[i:i + 1, :]).astype(BF16)

    for s in range(x.shape[0] // KT):
        vat_ref[s] = vt[0:W_A, s * KT:(s + 1) * KT].astype(BF16)
        vbt_ref[s] = vt[W_A:, s * KT:(s + 1) * KT].astype(BF16)

    low_half = lax.broadcasted_iota(I32, (x.shape[0], LANES), 1) < D_I
    for g in range(H_I // 2):
        pair = pidx[:, g * LANES:(g + 1) * LANES]
        hi = pair.astype(BF16).astype(F32)
        lo = pair - hi
        hi_sw = pltpu.roll(hi, D_I, 1)
        lo_sw = pltpu.roll(lo, D_I, 1)
        base = 2 * g * IDX_K
        for part, val in enumerate((jnp.where(low_half, hi, lo_sw), jnp.where(low_half, hi, 0.0),
                                    jnp.where(low_half, hi_sw, lo), jnp.where(low_half, hi_sw, 0.0))):
            qcat_ref[:, base + part * LANES:base + (part + 1) * LANES] = val.astype(BF16)
    tail = pidx[:, 2 * LANES:3 * LANES]
    hi = tail.astype(BF16).astype(F32)
    lo = tail - hi
    kcat_ref[:, 0:LANES] = jnp.where(low_half, hi, pltpu.roll(hi, D_I, 1)).astype(BF16)
    kcat_ref[:, LANES:2 * LANES] = jnp.where(low_half, lo, 0.0).astype(BF16)
    wi_ref[...] = tail

    gates_ref[...] = jax.nn.sigmoid(gate_logits).astype(BF16)


def _projection(x2, mod3, norm1_g, w_in, qn_a, kn_a, qn_b, kn_b, S):
    N = x2.shape[0]
    tm = TM_PROJ
    per_batch = S // tm
    w_qk = jnp.concatenate([w_in[:, 0:2 * W_A], w_in[:, 3 * W_A:3 * W_A + 2 * W_B]],
                           axis=1).astype(BF16)
    w_vt = jnp.concatenate([w_in[:, 2 * W_A:3 * W_A], w_in[:, 3 * W_A + 2 * W_B:W_QKV]],
                           axis=1).T.astype(BF16)
    w_idx = jnp.pad(w_in[:, W_QKV:W_QKV + W_IDX], ((0, 0), (0, W_IDX_PAD - W_IDX)))
    w_idx_hi = w_idx.astype(BF16)
    w_idx_lo = (w_idx - w_idx_hi.astype(F32)).astype(BF16)
    w_g = w_in[:, W_QKV + W_IDX:].astype(BF16)
    grp = jnp.asarray(np.kron(np.eye(W_A // D_A), np.ones((D_A, D_A))), BF16)
    gains = jnp.stack([jnp.tile(qn_a, H_A) * (LOG2E * D_A ** -0.5), jnp.tile(kn_a, H_A),
                       jnp.tile(qn_b, 2 * H_B) * (LOG2E * D_B ** -0.5), jnp.tile(kn_b, 2 * H_B)])

    def row_spec(w):
        return pl.BlockSpec((tm, w), lambda t: (t, 0))

    def rows(w, dt):
        return row_spec(w), jax.ShapeDtypeStruct((N, w), dt)

    def transposed(w):
        return (pl.BlockSpec((tm // KT, w, KT), lambda t: (t, 0, 0)),
                jax.ShapeDtypeStruct((N // KT, w, KT), BF16))

    outs = [rows(W_A, BF16), rows(W_A, BF16), rows(W_B, BF16), rows(W_B, BF16),
            transposed(W_A), transposed(W_B),
            rows(H_I * IDX_K, BF16), rows(IDX_K, BF16), rows(LANES, F32), rows(W_G, BF16)]
    return pl.pallas_call(
        _proj_kernel,
        grid=(N // tm,),
        in_specs=[row_spec(D_MODEL),
                  pl.BlockSpec((None, 6, D_MODEL), lambda t: (t // per_batch, 0, 0)),
                  _const_spec((1, D_MODEL)),
                  _const_spec((D_MODEL, 2 * W_A + 2 * W_B)),
                  _const_spec((W_A + W_B, D_MODEL)),
                  _const_spec((D_MODEL, W_IDX_PAD)),
                  _const_spec((D_MODEL, W_IDX_PAD)),
                  _const_spec((D_MODEL, W_G)),
                  _const_spec((W_A, W_A)),
                  _const_spec((4, W_A))],
        out_specs=[o[0] for o in outs],
        out_shape=[o[1] for o in outs],
        compiler_params=pltpu.CompilerParams(dimension_semantics=("parallel",),
                                             vmem_limit_bytes=VMEM_LIMIT),
        name="in_proj",
    )(x2, mod3, norm1_g.reshape(1, D_MODEL), w_qk, w_vt, w_idx_hi, w_idx_lo, w_g, grp, gains)


def _count_ge(ref, n_rows, cand):
    accs = [jnp.zeros((8, TQ), F32) for _ in range(CNT_ACCS)]
    for r in range(n_rows // 8):
        blk = ref[r * 8:(r + 1) * 8, :]
        accs[r % CNT_ACCS] = accs[r % CNT_ACCS] + jnp.where(blk >= cand, 1.0, 0.0)
    return jnp.sum(functools.reduce(lambda a, b: a + b, accs), axis=0, keepdims=True)


def _count_ge_bf16(ref, n_rows, cand):
    rows = 16
    one, zero = jnp.ones((), BF16), jnp.zeros((), BF16)

    assert n_rows // rows <= BF16_EXACT_INT
    accs = [jnp.zeros((rows, TQ), BF16) for _ in range(CNT_ACCS)]
    for r in range(n_rows // rows):
        blk = ref[r * rows:(r + 1) * rows, :]
        accs[r % CNT_ACCS] = accs[r % CNT_ACCS] + jnp.where(blk >= cand, one, zero)
    total = functools.reduce(lambda a, b: a + b, [a.astype(F32) for a in accs])
    return jnp.sum(total, axis=0, keepdims=True)


def _rank_to_f32(rank):
    key = rank ^ jnp.int32(-2 ** 31)
    return pltpu.bitcast(jnp.where(key < 0, key ^ jnp.int32(0x7FFFFFFF), key), F32)


def _store_bias(bias_ref, bias_t, L):
    bias_ref[0:L, :] = bias_t


def _select_bias(qb, qcat_ref, kcat_ref, wt_ref, score_ref, coarse_ref, eqi_ref, bias_ref):
    L = (qb + 1) * TQ
    admissible = ((lax.broadcasted_iota(I32, (KT, TQ), 0) // CHUNK)
                  <= (lax.broadcasted_iota(I32, (KT, TQ), 1) // CHUNK))
    if L <= TOPK_MAX:
        _store_bias(bias_ref, jnp.where(admissible, 0.0, MASK_VALUE), L)
        return

    for j in range(L // KT):
        rows = slice(j * KT, (j + 1) * KT)
        score = jnp.zeros((KT, TQ), F32)
        for h in range(H_I):
            logits = _dot_nt(kcat_ref[rows, :], qcat_ref[:, h * IDX_K:(h + 1) * IDX_K])
            score = score + wt_ref[h:h + 1, :] * jnp.maximum(logits, 0.0)
        if j == qb:
            score = jnp.where(admissible, score, -jnp.inf)
        score_ref[rows, :] = score
        coarse_ref[rows, :] = score.astype(BF16)
    k_sel = float(TOPK_MAX)

    def coarse_body(it, rank):
        cand = rank | lax.shift_left(jnp.int32(1), 31 - it)
        c = _count_ge_bf16(coarse_ref, L, _rank_to_f32(cand).astype(BF16))
        return jnp.where(c >= k_sel, cand, rank)
    base = lax.fori_loop(0, 16, coarse_body, jnp.zeros((1, TQ), I32)) - jnp.int32(1 << 16)

    def fine_body(it, st):
        off, cnt = st
        cand = off | lax.shift_left(jnp.int32(1), 16 - it)
        c = _count_ge(score_ref, L, _rank_to_f32(base + cand))
        take = c >= k_sel
        return jnp.where(take, cand, off), jnp.where(take, c, cnt)
    off, cnt = lax.fori_loop(
        0, 17, fine_body, (jnp.zeros((1, TQ), I32), jnp.full((1, TQ), float(L), F32)))
    rank = base + off
    thr = _rank_to_f32(rank)
    tied = jnp.max(cnt) > k_sel

    @pl.when(jnp.logical_not(tied))
    def _():
        _store_bias(bias_ref, jnp.where(score_ref[0:L, :] >= thr, 0.0, MASK_VALUE), L)

    @pl.when(tied)
    def _():
        above = _rank_to_f32(rank + 1)
        need = k_sel - _count_ge(score_ref, L, above)
        big = jnp.int32(1 << 30)
        sc = score_ref[0:L, :]
        s_idx = lax.broadcasted_iota(I32, (L, TQ), 0)
        eqi_ref[0:L, :] = jnp.where(sc >= above, big, jnp.where(sc >= thr, s_idx, big))
        n_bits = max(1, int(np.ceil(np.log2(L))))

        def idx_body(it, j):
            cand = j | lax.shift_left(jnp.int32(1), n_bits - 1 - it)
            cnt_lt = float(L) - _count_ge(eqi_ref, L, cand)
            return jnp.where(cnt_lt < need, cand, j)
        j_sel = lax.fori_loop(0, n_bits, idx_body, jnp.zeros((1, TQ), I32))
        sel = jnp.where(score_ref[0:L, :] >= above, 0.0,
                        jnp.where(eqi_ref[0:L, :] <= j_sel, 0.0, MASK_VALUE))
        _store_bias(bias_ref, sel, L)


def _tile_offsets():
    return (lax.broadcasted_iota(I32, (KT, TQ), 1)
            - lax.broadcasted_iota(I32, (KT, TQ), 0)).astype(F32)


def _masked_halves(q_ref, n_groups):
    lane = lax.broadcasted_iota(I32, (TQ, LANES), 1)
    low = jnp.where(lane < 64, 1.0, 0.0)
    high = 1.0 - low
    out = []
    for g in range(n_groups):
        q2 = q_ref[:, g * LANES:(g + 1) * LANES].astype(F32)
        out += [(q2 * low).astype(BF16), (q2 * high).astype(BF16)]
    return out


ONES_ROWS = 16


def _with_ones(vt):
    return jnp.concatenate([vt, jnp.ones((ONES_ROWS, vt.shape[1]), BF16)], axis=0)


def _online_steps(scores, values, maxes, accs):
    ss = [f() for f in scores]
    out = []
    for s, v, m, (acc_ref, i) in zip(ss, values, maxes, accs):
        m_new = jnp.maximum(m, jnp.max(s, axis=0, keepdims=True))
        p = jnp.exp2((s - m_new).astype(BF16))
        acc_ref[i] = jnp.exp2(m - m_new) * acc_ref[i] + _dot(_with_ones(v()), p)
        out.append(m_new)
    return tuple(out)


def _softmax_init(*acc_refs):
    for acc_ref in acc_refs:
        acc_ref[...] = jnp.zeros(acc_ref.shape, F32)
    n = sum(r.shape[0] for r in acc_refs)
    return tuple(jnp.full((1, TQ), -jnp.inf, F32) for _ in range(n))


def _softmax_result(acc_ref, i):
    rows = acc_ref.shape[1] - ONES_ROWS
    return acc_ref[i, 0:rows, :] / acc_ref[i, rows:rows + 1, :]


def _key_rows(j):
    return pl.ds(pl.multiple_of(j * KT, KT), KT)


def _split3(x):
    a = x.astype(BF16)
    r = x - a.astype(F32)
    b = r.astype(BF16)
    return a, b, (r - b.astype(F32)).astype(BF16)


def _alibi_tables(S, slopes):
    pos = jnp.arange(S, dtype=I32)
    hi = (pos // 64 * 64).astype(F32)
    lo = (pos % 64).astype(F32)
    lam = jnp.full((S,), LOG2E, F32)
    k_cols = [*_split3(lam), *_split3(lam), *_split3(lam * hi), *_split3(lam * lo)]
    kaug = jnp.pad(jnp.stack(k_cols, axis=1), ((0, 0), (0, LANES - len(k_cols))))
    qaug = []
    for sl in slopes:
        q_cols = [-sl * hi] * 3 + [-sl * lo] * 3 + [jnp.full((S,), sl, F32)] * 6
        qaug.append(jnp.pad(jnp.stack(q_cols, axis=1), ((0, 0), (0, LANES - len(q_cols)))))
    return kaug, jnp.stack(qaug).astype(BF16)


def _attn_kernel(qb, qcat_ref, kcat_ref, wt_ref, qa_ref, ka_ref, vat_ref, kaug_ref, qaug_a_ref,
                 lam_ref, sg_ref, qb_ref, kb_ref, vbt_ref, qaug_b_ref, oa_ref, ob_ref,
                 score_ref, coarse_ref, eqi_ref, bias_ref, acc_a_ref, acc_b_ref):
    _select_bias(qb, qcat_ref, kcat_ref, wt_ref, score_ref, coarse_ref, eqi_ref, bias_ref)
    n_b = 2 * H_B
    qms_a = _masked_halves(qa_ref, H_A // 2)
    qms_b = _masked_halves(qb_ref, H_B)
    qx_a = [jnp.concatenate([qms_a[h], qaug_a_ref[h]], axis=1) for h in range(H_A)]
    qx_b = [jnp.concatenate([qms_b[i], qaug_b_ref[i // 2]], axis=1) for i in range(n_b)]
    accs = [(acc_a_ref, h) for h in range(H_A)] + [(acc_b_ref, i) for i in range(n_b)]

    def values(j):
        return ([lambda h=h: vat_ref[j, h * D_A:(h + 1) * D_A, :] for h in range(H_A)]
                + [lambda i=i: vbt_ref[j, (i // 2) * LANES:(i // 2 + 1) * LANES, :]
                   for i in range(n_b)])

    def off_diagonal(j, maxes):
        rows = _key_rows(j)
        bias_t = bias_ref[rows, :]
        kaug_t = kaug_ref[rows, :]
        kx_a = [jnp.concatenate([ka_ref[rows, g * LANES:(g + 1) * LANES], kaug_t], axis=1)
                for g in range(H_A // 2)]
        kx_b = [jnp.concatenate([kb_ref[rows, h * LANES:(h + 1) * LANES], kaug_t], axis=1)
                for h in range(H_B)]
        scores = ([lambda h=h: _dot_nt(kx_a[h // 2], qx_a[h]) + bias_t for h in range(H_A)]
                  + [lambda i=i: _dot_nt(kx_b[i // 2], qx_b[i]) for i in range(n_b)])
        return _online_steps(scores, values(j), maxes, accs)

    maxes = lax.fori_loop(0, qb, off_diagonal, _softmax_init(acc_a_ref, acc_b_ref),
                          unroll=ATTN_UNROLL)

    diag = slice(qb * KT, (qb + 1) * KT)
    bias_t = bias_ref[diag, :]
    dist = jnp.abs(_tile_offsets())
    allowed = ((lax.broadcasted_iota(I32, (KT, TQ), 0) // CHUNK)
               <= (lax.broadcasted_iota(I32, (KT, TQ), 1) // CHUNK))
    biases_b = [jnp.where(allowed, -(LOG2E * sl) * dist, MASK_VALUE) for sl in SLOPES_B]
    scores = ([lambda h=h: (_dot_nt(ka_ref[diag, (h // 2) * LANES:(h // 2 + 1) * LANES], qms_a[h])
                            + (bias_t - (LOG2E * SLOPES_A[h]) * dist)) for h in range(H_A)]
              + [lambda i=i: (_dot_nt(kb_ref[diag, (i // 2) * LANES:(i // 2 + 1) * LANES], qms_b[i])
                              + biases_b[i // 2]) for i in range(n_b)])
    _online_steps(scores, values(qb), maxes, accs)

    out_t = jnp.concatenate([_softmax_result(acc_a_ref, h) for h in range(H_A)], axis=0)
    oa_ref[...] = out_t.T.astype(BF16)

    lam = (jnp.exp(jnp.sum(lam_ref[0:1, :] * lam_ref[1:2, :], axis=-1, keepdims=True))
           - jnp.exp(jnp.sum(lam_ref[2:3, :] * lam_ref[3:4, :], axis=-1, keepdims=True))
           + LAMBDA_INIT)
    ys = []
    for h in range(H_B):
        o = _softmax_result(acc_b_ref, 2 * h) - lam * _softmax_result(acc_b_ref, 2 * h + 1)
        ys.append(o * lax.rsqrt(jnp.mean(o * o, axis=0, keepdims=True) + RMS_EPS))
    y = jnp.concatenate(ys, axis=0).T
    ob_ref[...] = (y * sg_ref[...] * (1.0 - LAMBDA_INIT)).astype(BF16)


def _attn_block(qb, qcat, kcat, wt, qa, ka, vat, kaug, qaug_a, lam4, subln_g, qbn, kbn, vbt,
                qaug_b):
    B, S = qa.shape[0], qa.shape[1]
    L = (qb + 1) * TQ

    def q_rows(w):
        return pl.BlockSpec((None, TQ, w), lambda b: (b, qb, 0))

    def key_rows(w):
        return pl.BlockSpec((None, L, w), lambda b: (b, 0, 0))

    def value_tiles(w):
        return pl.BlockSpec((None, qb + 1, w, KT), lambda b: (b, 0, 0, 0))

    def qaug_spec(h):
        return pl.BlockSpec((h, TQ, LANES), lambda b: (0, qb, 0))

    return pl.pallas_call(
        functools.partial(_attn_kernel, qb),
        grid=(B,),
        in_specs=[q_rows(H_I * IDX_K), key_rows(IDX_K),
                  pl.BlockSpec((None, H_I, TQ), lambda b: (b, 0, qb)),
                  q_rows(W_A), key_rows(W_A), value_tiles(W_A),
                  pl.BlockSpec((L, LANES), lambda b: (0, 0)), qaug_spec(H_A),
                  pl.BlockSpec((4, D_B), lambda b: (0, 0)),
                  pl.BlockSpec((1, W_B), lambda b: (0, 0)),
                  q_rows(W_B), key_rows(W_B), value_tiles(W_B), qaug_spec(H_B)],
        out_specs=[q_rows(W_A), q_rows(W_B)],
        out_shape=[jax.ShapeDtypeStruct(qa.shape, BF16), jax.ShapeDtypeStruct(qbn.shape, BF16)],
        input_output_aliases={3: 0, 10: 1},
        scratch_shapes=[pltpu.VMEM((S, TQ), F32), pltpu.VMEM((S, TQ), BF16),
                        pltpu.VMEM((S, TQ), I32), pltpu.VMEM((S, TQ), F32),
                        pltpu.VMEM((H_A, D_A + ONES_ROWS, TQ), F32),
                        pltpu.VMEM((2 * H_B, 2 * D_B + ONES_ROWS, TQ), F32)],
        compiler_params=pltpu.CompilerParams(dimension_semantics=("parallel",),
                                             vmem_limit_bytes=VMEM_LIMIT),
        name=f"attn_q{qb}",
    )(qcat, kcat, wt, qa, ka, vat, kaug, qaug_a,
      lam4, jnp.tile(subln_g, H_B).reshape(1, W_B), qbn, kbn, vbt, qaug_b)


def _ffn_kernel(x_ref, mod_ref, ya_ref, yb_ref, gates_ref, wua_ref, wub_ref, wo_ref, g2_ref,
                w1_ref, w3_ref, w2_ref, o_ref):
    gates = gates_ref[...].astype(F32)
    merged = (gates[:, :D_MODEL] * _dot(ya_ref[...], wua_ref[...])
              + gates[:, D_MODEL:] * _dot(yb_ref[...], wub_ref[...]))
    x1 = x_ref[...] + mod_ref[2:3, :] * _dot(merged.astype(BF16), wo_ref[...])
    ms = jnp.mean(x1 * x1, axis=-1, keepdims=True)
    h2 = x1 * lax.rsqrt(ms + RMS_EPS) * g2_ref[...]
    h2 = (h2 * (1.0 + mod_ref[4:5, :]) + mod_ref[3:4, :]).astype(BF16)
    u = _dot(h2, w1_ref[...])
    act = (u * jax.nn.sigmoid(u) * _dot(h2, w3_ref[...])).astype(BF16)
    o_ref[...] = x1 + mod_ref[5:6, :] * _dot(act, w2_ref[...])


def _merge_ffn(x2, mod3, ya, yb, gates, w_up_a, w_up_b, w_o, norm2_g, w_ff1, w_ff3, w_ff2, S):
    N = x2.shape[0]
    tm = TM_FFN
    per_batch = S // tm

    def row_spec(w):
        return pl.BlockSpec((tm, w), lambda t: (t, 0))

    return pl.pallas_call(
        _ffn_kernel,
        grid=(N // tm,),
        in_specs=[row_spec(D_MODEL),
                  pl.BlockSpec((None, 6, D_MODEL), lambda t: (t // per_batch, 0, 0)),
                  row_spec(W_A), row_spec(W_B), row_spec(W_G),
                  _const_spec((W_A, D_MODEL)), _const_spec((W_B, D_MODEL)),
                  _const_spec((D_MODEL, D_MODEL)), _const_spec((1, D_MODEL)),
                  _const_spec((D_MODEL, D_FF)), _const_spec((D_MODEL, D_FF)),
                  _const_spec((D_FF, D_MODEL))],
        out_specs=row_spec(D_MODEL),
        out_shape=jax.ShapeDtypeStruct((N, D_MODEL), F32),
        compiler_params=pltpu.CompilerParams(dimension_semantics=("parallel",),
                                             vmem_limit_bytes=VMEM_LIMIT),
        name="merge_ffn",
    )(x2, mod3, ya, yb, gates, w_up_a.astype(BF16), w_up_b.astype(BF16), w_o.astype(BF16),
      norm2_g.reshape(1, D_MODEL), w_ff1.astype(BF16), w_ff3.astype(BF16), w_ff2.astype(BF16))


def kernel(x, c, w_ada, b_ada, norm1_g, w_in, qn_a, kn_a, qn_b, kn_b, lam_q1, lam_k1, lam_q2,
           lam_k2, subln_g, w_up_a, w_up_b, w_o, norm2_g, w_ff1, w_ff3, w_ff2):
    B, S, D = x.shape
    assert D == D_MODEL and S % TQ == 0 and w_ada.shape[0] == 1
    N = B * S
    x2 = x.reshape(N, D)
    mod3 = _modulation(c, w_ada[0], b_ada[0]).reshape(B, 6, D)

    (qa, ka, qbn, kbn, vat, vbt, qcat, kcat, wi, gates) = _projection(
        x2, mod3, norm1_g[0], w_in[0], qn_a[0], kn_a[0], qn_b[0], kn_b[0], S)
    r3 = lambda a: a.reshape(B, S, a.shape[-1])
    qa, ka, qbn, kbn, qcat, kcat = map(r3, (qa, ka, qbn, kbn, qcat, kcat))
    vat, vbt = (a.reshape(B, S // KT, a.shape[1], KT) for a in (vat, vbt))
    wt = wi[:, D_I:D_I + H_I].reshape(B, S, H_I).transpose(0, 2, 1)
    lam4 = jnp.stack([lam_q1[0], lam_k1[0], lam_q2[0], lam_k2[0]])

    kaug, qaug_a = _alibi_tables(S, SLOPES_A)
    _, qaug_b = _alibi_tables(S, SLOPES_B)
    ya, yb = qa, qbn
    for qb in range(S // TQ):
        ya, yb = _attn_block(qb, qcat, kcat, wt, ya, ka, vat, kaug, qaug_a,
                             lam4, subln_g[0], yb, kbn, vbt, qaug_b)
    ya = ya.reshape(N, W_A)
    yb = yb.reshape(N, W_B)

    out = _merge_ffn(x2, mod3, ya, yb, gates, w_up_a[0], w_up_b[0], w_o[0], norm2_g[0],
                     w_ff1[0], w_ff3[0], w_ff2[0], S)
    return out.reshape(B, S, D)
```

```python
import functools

import numpy as np
import jax
import jax.numpy as jnp
from jax import lax
from jax.experimental import pallas as pl
from jax.experimental.pallas import tpu as pltpu

F32 = jnp.float32
BF16 = jnp.bfloat16
I32 = jnp.int32

D_MODEL = 1024
CHUNK = 64
H_A, D_A = 8, 64
H_I, D_I = 4, 64
TOPK_MAX = 256
H_B, D_B = 4, 64
D_FF = 2816
RMS_EPS = 1e-6
MASK_VALUE = -1e30
LAMBDA_INIT = 0.2
LOG2E = 1.4426950408889634
SLOPES_A = [2.0 ** (-8.0 * (h + 1) / H_A) for h in range(H_A)]
SLOPES_B = [2.0 ** (-8.0 * (h + 1) / H_B) for h in range(H_B)]

W_A = H_A * D_A
W_B = H_B * 2 * D_B
W_QKV = 3 * W_A + 3 * W_B
W_IDX = H_I * D_I + D_I + H_I
W_IDX_PAD = 384
W_G = 2 * D_MODEL

LANES = 128
IDX_K = 2 * LANES
BF16_EXACT_INT = 256
WI_ROWS = 8

TQ = 256
KT = TQ
TM_PROJ = 2 * KT
TM_FFN = 512
CNT_ACCS = 4
ATTN_UNROLL = True
VMEM_LIMIT = 56 * 1024 * 1024

NT_DIMS = (((1,), (1,)), ((), ()))


def _dot(a, b):
    return jnp.dot(a, b, preferred_element_type=F32)


def _dot_nt(a, b):
    return lax.dot_general(a, b, NT_DIMS, preferred_element_type=F32)


def _split(x):
    hi = x.astype(BF16)
    lo = (x - hi.astype(F32)).astype(BF16)
    return hi, lo


def _dot3(a, b_hi, b_lo):
    a_hi, a_lo = _split(a)
    return _dot(a_hi, b_hi) + _dot(a_hi, b_lo) + _dot(a_lo, b_hi)


def _const_spec(shape):
    nd = len(shape)
    return pl.BlockSpec(shape, lambda *_: (0,) * nd, pipeline_mode=pl.Buffered(1))


def _mod_kernel(c_ref, w_ref, b_ref, o_ref):
    c = c_ref[...]
    cs = c * jax.nn.sigmoid(c)
    w_hi, w_lo = _split(w_ref[...])
    o_ref[...] = _dot3(cs, w_hi, w_lo) + b_ref[...]


def _modulation(c, w_ada, b_ada):
    B = c.shape[0]
    n = w_ada.shape[1]
    tn = 1024
    return pl.pallas_call(
        _mod_kernel,
        grid=(n // tn,),
        in_specs=[pl.BlockSpec((B, D_MODEL), lambda j: (0, 0)),
                  pl.BlockSpec((D_MODEL, tn), lambda j: (0, j)),
                  pl.BlockSpec((1, tn), lambda j: (0, j))],
        out_specs=pl.BlockSpec((B, tn), lambda j: (0, j)),
        out_shape=jax.ShapeDtypeStruct((B, n), F32),
        compiler_params=pltpu.CompilerParams(vmem_limit_bytes=VMEM_LIMIT),
        name="adaln_mod",
    )(c, w_ada, b_ada.reshape(1, n))


def _proj_kernel(x_ref, mod_ref, g1_ref, wqk_ref, wvt_ref, wih_ref, wil_ref, wg_ref, grp_ref,
                 gains_ref,
                 qa_ref, ka_ref, qb_ref, kb_ref, vat_ref, vbt_ref, qcat_ref, kcat_ref, wi_ref,
                 gates_ref):
    x = x_ref[...]
    ms = jnp.mean(x * x, axis=-1, keepdims=True)
    h = x * lax.rsqrt(ms + RMS_EPS) * g1_ref[...]
    h = h * (1.0 + mod_ref[1:2, :]) + mod_ref[0:1, :]
    h_hi, h_lo = _split(h)

    qk = _dot(h_hi, wqk_ref[...])
    vt = _dot_nt(wvt_ref[...], h_hi)
    pidx = (_dot(h_hi, wih_ref[...]) + _dot(h_hi, wil_ref[...]) + _dot(h_lo, wih_ref[...]))
    gate_logits = _dot(h_hi, wg_ref[...])

    grp = grp_ref[...]
    zs = [qk[:, i * W_A:(i + 1) * W_A] for i in range(4)]
    squares = [(z * z).astype(BF16) for z in zs]
    means = [_dot(sq, grp) * (1.0 / D_A) for sq in squares]
    for i, out_ref in enumerate((qa_ref, ka_ref, qb_ref, kb_ref)):
        out_ref[...] = (zs[i] * lax.rsqrt(means[i] + RMS_EPS) * gains_ref[i:i + 1, :]).astype(BF16)

    for s in range(x.shape[0] // KT):
        vat_ref[s] = vt[0:W_A, s * KT:(s + 1) * KT].astype(BF16)
        vbt_ref[s] = vt[W_A:, s * KT:(s + 1) * KT].astype(BF16)

    low_half = lax.broadcasted_iota(I32, (x.shape[0], LANES), 1) < D_I
    for g in range(H_I // 2):
        pair = pidx[:, g * LANES:(g + 1) * LANES]
        hi = pair.astype(BF16).astype(F32)
        lo = pair - hi
        hi_sw = pltpu.roll(hi, D_I, 1)
        lo_sw = pltpu.roll(lo, D_I, 1)
        base = 2 * g * IDX_K
        for part, val in enumerate((jnp.where(low_half, hi, lo_sw), jnp.where(low_half, hi, 0.0),
                                    jnp.where(low_half, hi_sw, lo), jnp.where(low_half, hi_sw, 0.0))):
            qcat_ref[:, base + part * LANES:base + (part + 1) * LANES] = val.astype(BF16)
    tail = pidx[:, 2 * LANES:3 * LANES]
    hi = tail.astype(BF16).astype(F32)
    lo = tail - hi
    kcat_ref[:, 0:LANES] = jnp.where(low_half, hi, pltpu.roll(hi, D_I, 1)).astype(BF16)
    kcat_ref[:, LANES:2 * LANES] = jnp.where(low_half, lo, 0.0).astype(BF16)
    wi_ref[...] = tail.T[D_I:D_I + WI_ROWS, :]

    gates_ref[...] = jax.nn.sigmoid(gate_logits).astype(BF16)


def _projection(x2, mod3, norm1_g, w_in, qn_a, kn_a, qn_b, kn_b, S):
    N = x2.shape[0]
    tm = TM_PROJ
    per_batch = S // tm
    w_qk = jnp.concatenate([w_in[:, 0:2 * W_A], w_in[:, 3 * W_A:3 * W_A + 2 * W_B]],
                           axis=1).astype(BF16)
    w_vt = jnp.concatenate([w_in[:, 2 * W_A:3 * W_A], w_in[:, 3 * W_A + 2 * W_B:W_QKV]],
                           axis=1).T.astype(BF16)
    w_idx = jnp.pad(w_in[:, W_QKV:W_QKV + W_IDX], ((0, 0), (0, W_IDX_PAD - W_IDX)))
    w_idx_hi = w_idx.astype(BF16)
    w_idx_lo = (w_idx - w_idx_hi.astype(F32)).astype(BF16)
    w_g = w_in[:, W_QKV + W_IDX:].astype(BF16)
    grp = jnp.asarray(np.kron(np.eye(W_A // D_A), np.ones((D_A, D_A))), BF16)
    gains = jnp.stack([jnp.tile(qn_a, H_A) * (LOG2E * D_A ** -0.5), jnp.tile(kn_a, H_A),
                       jnp.tile(qn_b, 2 * H_B) * (LOG2E * D_B ** -0.5), jnp.tile(kn_b, 2 * H_B)])

    def row_spec(w):
        return pl.BlockSpec((tm, w), lambda t: (t, 0))

    def rows(w, dt):
        return row_spec(w), jax.ShapeDtypeStruct((N, w), dt)

    def transposed(w):
        return (pl.BlockSpec((tm // KT, w, KT), lambda t: (t, 0, 0)),
                jax.ShapeDtypeStruct((N // KT, w, KT), BF16))

    outs = [rows(W_A, BF16), rows(W_A, BF16), rows(W_B, BF16), rows(W_B, BF16),
            transposed(W_A), transposed(W_B),
            rows(H_I * IDX_K, BF16), rows(IDX_K, BF16),
            (pl.BlockSpec((None, WI_ROWS, tm), lambda t: (t, 0, 0)),
             jax.ShapeDtypeStruct((N // tm, WI_ROWS, tm), F32)),
            rows(W_G, BF16)]
    return pl.pallas_call(
        _proj_kernel,
        grid=(N // tm,),
        in_specs=[row_spec(D_MODEL),
                  pl.BlockSpec((None, 6, D_MODEL), lambda t: (t // per_batch, 0, 0)),
                  _const_spec((1, D_MODEL)),
                  _const_spec((D_MODEL, 2 * W_A + 2 * W_B)),
                  _const_spec((W_A + W_B, D_MODEL)),
                  _const_spec((D_MODEL, W_IDX_PAD)),
                  _const_spec((D_MODEL, W_IDX_PAD)),
                  _const_spec((D_MODEL, W_G)),
                  _const_spec((W_A, W_A)),
                  _const_spec((4, W_A))],
        out_specs=[o[0] for o in outs],
        out_shape=[o[1] for o in outs],
        compiler_params=pltpu.CompilerParams(dimension_semantics=("parallel",),
                                             vmem_limit_bytes=VMEM_LIMIT),
        name="in_proj",
    )(x2, mod3, norm1_g.reshape(1, D_MODEL), w_qk, w_vt, w_idx_hi, w_idx_lo, w_g, grp, gains)


def _count_ge(ref, n_rows, cand):
    accs = [jnp.zeros((8, TQ), F32) for _ in range(CNT_ACCS)]
    for r in range(n_rows // 8):
        blk = ref[r * 8:(r + 1) * 8, :]
        accs[r % CNT_ACCS] = accs[r % CNT_ACCS] + jnp.where(blk >= cand, 1.0, 0.0)
    return jnp.sum(functools.reduce(lambda a, b: a + b, accs), axis=0, keepdims=True)


def _count_ge_bf16(ref, n_rows, cand):
    rows = 16
    one, zero = jnp.ones((), BF16), jnp.zeros((), BF16)

    assert n_rows // rows <= BF16_EXACT_INT
    accs = [jnp.zeros((rows, TQ), BF16) for _ in range(CNT_ACCS)]
    for r in range(n_rows // rows):
        blk = ref[r * rows:(r + 1) * rows, :]
        accs[r % CNT_ACCS] = accs[r % CNT_ACCS] + jnp.where(blk >= cand, one, zero)
    total = functools.reduce(lambda a, b: a + b, [a.astype(F32) for a in accs])
    return jnp.sum(total, axis=0, keepdims=True)


def _rank_to_f32(rank):
    key = rank ^ jnp.int32(-2 ** 31)
    return pltpu.bitcast(jnp.where(key < 0, key ^ jnp.int32(0x7FFFFFFF), key), F32)


def _store_bias(bias_ref, bias_t, L):
    bias_ref[0:L, :] = bias_t


def _select_bias(qb, qcat_ref, kcat_ref, wt_ref, score_ref, coarse_ref, eqi_ref, bias_ref):
    L = (qb + 1) * TQ
    admissible = ((lax.broadcasted_iota(I32, (KT, TQ), 0) // CHUNK)
                  <= (lax.broadcasted_iota(I32, (KT, TQ), 1) // CHUNK))
    if L <= TOPK_MAX:
        _store_bias(bias_ref, jnp.where(admissible, 0.0, MASK_VALUE), L)
        return

    for j in range(L // KT):
        rows = slice(j * KT, (j + 1) * KT)
        score = jnp.zeros((KT, TQ), F32)
        for h in range(H_I):
            logits = _dot_nt(kcat_ref[rows, :], qcat_ref[:, h * IDX_K:(h + 1) * IDX_K])
            score = score + wt_ref[h:h + 1, :] * jnp.maximum(logits, 0.0)
        if j == qb:
            score = jnp.where(admissible, score, -jnp.inf)
        score_ref[rows, :] = score
        coarse_ref[rows, :] = score.astype(BF16)
    k_sel = float(TOPK_MAX)

    def coarse_body(it, rank):
        cand = rank | lax.shift_left(jnp.int32(1), 31 - it)
        c = _count_ge_bf16(coarse_ref, L, _rank_to_f32(cand).astype(BF16))
        return jnp.where(c >= k_sel, cand, rank)
    base = lax.fori_loop(0, 16, coarse_body, jnp.zeros((1, TQ), I32)) - jnp.int32(1 << 16)

    def fine_body(it, st):
        off, cnt = st
        cand = off | lax.shift_left(jnp.int32(1), 16 - it)
        c = _count_ge(score_ref, L, _rank_to_f32(base + cand))
        take = c >= k_sel
        return jnp.where(take, cand, off), jnp.where(take, c, cnt)
    off, cnt = lax.fori_loop(
        0, 17, fine_body, (jnp.zeros((1, TQ), I32), jnp.full((1, TQ), float(L), F32)))
    rank = base + off
    thr = _rank_to_f32(rank)
    tied = jnp.max(cnt) > k_sel

    @pl.when(jnp.logical_not(tied))
    def _():
        _store_bias(bias_ref, jnp.where(score_ref[0:L, :] >= thr, 0.0, MASK_VALUE), L)

    @pl.when(tied)
    def _():
        above = _rank_to_f32(rank + 1)
        need = k_sel - _count_ge(score_ref, L, above)
        big = jnp.int32(1 << 30)
        sc = score_ref[0:L, :]
        s_idx = lax.broadcasted_iota(I32, (L, TQ), 0)
        eqi_ref[0:L, :] = jnp.where(sc >= above, big, jnp.where(sc >= thr, s_idx, big))
        n_bits = max(1, int(np.ceil(np.log2(L))))

        def idx_body(it, j):
            cand = j | lax.shift_left(jnp.int32(1), n_bits - 1 - it)
            cnt_lt = float(L) - _count_ge(eqi_ref, L, cand)
            return jnp.where(cnt_lt < need, cand, j)
        j_sel = lax.fori_loop(0, n_bits, idx_body, jnp.zeros((1, TQ), I32))
        sel = jnp.where(score_ref[0:L, :] >= above, 0.0,
                        jnp.where(eqi_ref[0:L, :] <= j_sel, 0.0, MASK_VALUE))
        _store_bias(bias_ref, sel, L)


def _tile_offsets():
    return (lax.broadcasted_iota(I32, (KT, TQ), 1)
            - lax.broadcasted_iota(I32, (KT, TQ), 0)).astype(F32)


def _masked_halves(q_ref, n_groups):
    lane = lax.broadcasted_iota(I32, (TQ, LANES), 1)
    low = jnp.where(lane < 64, 1.0, 0.0)
    high = 1.0 - low
    out = []
    for g in range(n_groups):
        q2 = q_ref[:, g * LANES:(g + 1) * LANES].astype(F32)
        out += [(q2 * low).astype(BF16), (q2 * high).astype(BF16)]
    return out


ONES_ROWS = 16


def _with_ones(vt):
    return jnp.concatenate([vt, jnp.ones((ONES_ROWS, vt.shape[1]), BF16)], axis=0)


def _online_steps(scores, values, maxes, accs):
    ss = [f() for f in scores]
    out = []
    for s, v, m, (acc_ref, i) in zip(ss, values, maxes, accs):
        m_new = jnp.maximum(m, jnp.max(s, axis=0, keepdims=True))
        p = jnp.exp2((s - m_new).astype(BF16))
        acc_ref[i] = jnp.exp2(m - m_new) * acc_ref[i] + _dot(_with_ones(v()), p)
        out.append(m_new)
    return tuple(out)


def _softmax_init(*acc_refs):
    for acc_ref in acc_refs:
        acc_ref[...] = jnp.zeros(acc_ref.shape, F32)
    n = sum(r.shape[0] for r in acc_refs)
    return tuple(jnp.full((1, TQ), -jnp.inf, F32) for _ in range(n))


def _softmax_result(acc_ref, i):
    rows = acc_ref.shape[1] - ONES_ROWS
    return acc_ref[i, 0:rows, :] / acc_ref[i, rows:rows + 1, :]


def _key_rows(j):
    return pl.ds(pl.multiple_of(j * KT, KT), KT)


def _split3(x):
    a = x.astype(BF16)
    r = x - a.astype(F32)
    b = r.astype(BF16)
    return a, b, (r - b.astype(F32)).astype(BF16)


def _alibi_tables(S, slopes):
    pos = jnp.arange(S, dtype=I32)
    hi = (pos // 64 * 64).astype(F32)
    lo = (pos % 64).astype(F32)
    lam = jnp.full((S,), LOG2E, F32)
    k_cols = [*_split3(lam), *_split3(lam), *_split3(lam * hi), *_split3(lam * lo)]
    kaug = jnp.pad(jnp.stack(k_cols, axis=1), ((0, 0), (0, LANES - len(k_cols))))
    qaug = []
    for sl in slopes:
        q_cols = [-sl * hi] * 3 + [-sl * lo] * 3 + [jnp.full((S,), sl, F32)] * 6
        qaug.append(jnp.pad(jnp.stack(q_cols, axis=1), ((0, 0), (0, LANES - len(q_cols)))))
    return kaug, jnp.stack(qaug).astype(BF16)


def _attn_kernel(qb, qcat_ref, kcat_ref, wt_ref, qa_ref, ka_ref, vat_ref, kaug_ref, qaug_a_ref,
                 lam_ref, sg_ref, qb_ref, kb_ref, vbt_ref, qaug_b_ref, oa_ref, ob_ref,
                 score_ref, coarse_ref, eqi_ref, bias_ref, acc_a_ref, acc_b_ref):
    _select_bias(qb, qcat_ref, kcat_ref, wt_ref, score_ref, coarse_ref, eqi_ref, bias_ref)
    n_b = 2 * H_B
    qms_a = _masked_halves(qa_ref, H_A // 2)
    qms_b = _masked_halves(qb_ref, H_B)
    qx_a = [jnp.concatenate([qms_a[h], qaug_a_ref[h]], axis=1) for h in range(H_A)]
    qx_b = [jnp.concatenate([qms_b[i], qaug_b_ref[i // 2]], axis=1) for i in range(n_b)]
    accs = [(acc_a_ref, h) for h in range(H_A)] + [(acc_b_ref, i) for i in range(n_b)]

    def values(j):
        return ([lambda h=h: vat_ref[j, h * D_A:(h + 1) * D_A, :] for h in range(H_A)]
                + [lambda i=i: vbt_ref[j, (i // 2) * LANES:(i // 2 + 1) * LANES, :]
                   for i in range(n_b)])

    def off_diagonal(j, maxes):
        rows = _key_rows(j)
        bias_t = bias_ref[rows, :]
        kaug_t = kaug_ref[rows, :]
        kx_a = [jnp.concatenate([ka_ref[rows, g * LANES:(g + 1) * LANES], kaug_t], axis=1)
                for g in range(H_A // 2)]
        kx_b = [jnp.concatenate([kb_ref[rows, h * LANES:(h + 1) * LANES], kaug_t], axis=1)
                for h in range(H_B)]
        scores = ([lambda h=h: _dot_nt(kx_a[h // 2], qx_a[h]) + bias_t for h in range(H_A)]
                  + [lambda i=i: _dot_nt(kx_b[i // 2], qx_b[i]) for i in range(n_b)])
        return _online_steps(scores, values(j), maxes, accs)

    maxes = lax.fori_loop(0, qb, off_diagonal, _softmax_init(acc_a_ref, acc_b_ref),
                          unroll=ATTN_UNROLL)

    diag = slice(qb * KT, (qb + 1) * KT)
    bias_t = bias_ref[diag, :]
    dist = jnp.abs(_tile_offsets())
    allowed = ((lax.broadcasted_iota(I32, (KT, TQ), 0) // CHUNK)
               <= (lax.broadcasted_iota(I32, (KT, TQ), 1) // CHUNK))
    biases_b = [jnp.where(allowed, -(LOG2E * sl) * dist, MASK_VALUE) for sl in SLOPES_B]
    scores = ([lambda h=h: (_dot_nt(ka_ref[diag, (h // 2) * LANES:(h // 2 + 1) * LANES], qms_a[h])
                            + (bias_t - (LOG2E * SLOPES_A[h]) * dist)) for h in range(H_A)]
              + [lambda i=i: (_dot_nt(kb_ref[diag, (i // 2) * LANES:(i // 2 + 1) * LANES], qms_b[i])
                              + biases_b[i // 2]) for i in range(n_b)])
    _online_steps(scores, values(qb), maxes, accs)

    out_t = jnp.concatenate([_softmax_result(acc_a_ref, h) for h in range(H_A)], axis=0)
    oa_ref[...] = out_t.T.astype(BF16)

    lam = (jnp.exp(jnp.sum(lam_ref[0:1, :] * lam_ref[1:2, :], axis=-1, keepdims=True))
           - jnp.exp(jnp.sum(lam_ref[2:3, :] * lam_ref[3:4, :], axis=-1, keepdims=True))
           + LAMBDA_INIT)
    ys = []
    for h in range(H_B):
        o = _softmax_result(acc_b_ref, 2 * h) - lam * _softmax_result(acc_b_ref, 2 * h + 1)
        ys.append(o * lax.rsqrt(jnp.mean(o * o, axis=0, keepdims=True) + RMS_EPS))
    y = jnp.concatenate(ys, axis=0).T
    ob_ref[...] = (y * sg_ref[...] * (1.0 - LAMBDA_INIT)).astype(BF16)


def _attn_block(qb, qcat, kcat, wt, qa, ka, vat, kaug, qaug_a, lam4, subln_g, qbn, kbn, vbt,
                qaug_b):
    B, S = qa.shape[0], qa.shape[1]
    L = (qb + 1) * TQ

    def q_rows(w):
        return pl.BlockSpec((None, TQ, w), lambda b: (b, qb, 0))

    def key_rows(w):
        return pl.BlockSpec((None, L, w), lambda b: (b, 0, 0))

    def value_tiles(w):
        return pl.BlockSpec((None, qb + 1, w, KT), lambda b: (b, 0, 0, 0))

    def qaug_spec(h):
        return pl.BlockSpec((h, TQ, LANES), lambda b: (0, qb, 0))

    return pl.pallas_call(
        functools.partial(_attn_kernel, qb),
        grid=(B,),
        in_specs=[q_rows(H_I * IDX_K), key_rows(IDX_K),
                  pl.BlockSpec((None, None, WI_ROWS, TQ),
                               lambda b: (b, qb // (TM_PROJ // TQ), 0, qb % (TM_PROJ // TQ))),
                  q_rows(W_A), key_rows(W_A), value_tiles(W_A),
                  pl.BlockSpec((L, LANES), lambda b: (0, 0)), qaug_spec(H_A),
                  pl.BlockSpec((4, D_B), lambda b: (0, 0)),
                  pl.BlockSpec((1, W_B), lambda b: (0, 0)),
                  q_rows(W_B), key_rows(W_B), value_tiles(W_B), qaug_spec(H_B)],
        out_specs=[q_rows(W_A), q_rows(W_B)],
        out_shape=[jax.ShapeDtypeStruct(qa.shape, BF16), jax.ShapeDtypeStruct(qbn.shape, BF16)],
        input_output_aliases={3: 0, 10: 1},
        scratch_shapes=[pltpu.VMEM((S, TQ), F32), pltpu.VMEM((S, TQ), BF16),
                        pltpu.VMEM((S, TQ), I32), pltpu.VMEM((S, TQ), F32),
                        pltpu.VMEM((H_A, D_A + ONES_ROWS, TQ), F32),
                        pltpu.VMEM((2 * H_B, 2 * D_B + ONES_ROWS, TQ), F32)],
        compiler_params=pltpu.CompilerParams(dimension_semantics=("parallel",),
                                             vmem_limit_bytes=VMEM_LIMIT),
        name=f"attn_q{qb}",
    )(qcat, kcat, wt, qa, ka, vat, kaug, qaug_a,
      lam4, jnp.tile(subln_g, H_B).reshape(1, W_B), qbn, kbn, vbt, qaug_b)


def _ffn_kernel(x_ref, mod_ref, ya_ref, yb_ref, gates_ref, wua_ref, wub_ref, wo_ref, g2_ref,
                w1_ref, w3_ref, w2_ref, o_ref):
    gates = gates_ref[...].astype(F32)
    merged = (gates[:, :D_MODEL] * _dot(ya_ref[...], wua_ref[...])
              + gates[:, D_MODEL:] * _dot(yb_ref[...], wub_ref[...]))
    x1 = x_ref[...] + mod_ref[2:3, :] * _dot(merged.astype(BF16), wo_ref[...])
    ms = jnp.mean(x1 * x1, axis=-1, keepdims=True)
    h2 = x1 * lax.rsqrt(ms + RMS_EPS) * g2_ref[...]
    h2 = (h2 * (1.0 + mod_ref[4:5, :]) + mod_ref[3:4, :]).astype(BF16)
    u = _dot(h2, w1_ref[...])
    act = (u * jax.nn.sigmoid(u) * _dot(h2, w3_ref[...])).astype(BF16)
    o_ref[...] = x1 + mod_ref[5:6, :] * _dot(act, w2_ref[...])


def _merge_ffn(x2, mod3, ya, yb, gates, w_up_a, w_up_b, w_o, norm2_g, w_ff1, w_ff3, w_ff2, S):
    N = x2.shape[0]
    tm = TM_FFN
    per_batch = S // tm

    def row_spec(w):
        return pl.BlockSpec((tm, w), lambda t: (t, 0))

    return pl.pallas_call(
        _ffn_kernel,
        grid=(N // tm,),
        in_specs=[row_spec(D_MODEL),
                  pl.BlockSpec((None, 6, D_MODEL), lambda t: (t // per_batch, 0, 0)),
                  row_spec(W_A), row_spec(W_B), row_spec(W_G),
                  _const_spec((W_A, D_MODEL)), _const_spec((W_B, D_MODEL)),
                  _const_spec((D_MODEL, D_MODEL)), _const_spec((1, D_MODEL)),
                  _const_spec((D_MODEL, D_FF)), _const_spec((D_MODEL, D_FF)),
                  _const_spec((D_FF, D_MODEL))],
        out_specs=row_spec(D_MODEL),
        out_shape=jax.ShapeDtypeStruct((N, D_MODEL), F32),
        compiler_params=pltpu.CompilerParams(dimension_semantics=("parallel",),
                                             vmem_limit_bytes=VMEM_LIMIT),
        name="merge_ffn",
    )(x2, mod3, ya, yb, gates, w_up_a.astype(BF16), w_up_b.astype(BF16), w_o.astype(BF16),
      norm2_g.reshape(1, D_MODEL), w_ff1.astype(BF16), w_ff3.astype(BF16), w_ff2.astype(BF16))


def kernel(x, c, w_ada, b_ada, norm1_g, w_in, qn_a, kn_a, qn_b, kn_b, lam_q1, lam_k1, lam_q2,
           lam_k2, subln_g, w_up_a, w_up_b, w_o, norm2_g, w_ff1, w_ff3, w_ff2):
    B, S, D = x.shape
    assert D == D_MODEL and S % TQ == 0 and w_ada.shape[0] == 1
    N = B * S
    x2 = x.reshape(N, D)
    mod3 = _modulation(c, w_ada[0], b_ada[0]).reshape(B, 6, D)

    (qa, ka, qbn, kbn, vat, vbt, qcat, kcat, wi, gates) = _projection(
        x2, mod3, norm1_g[0], w_in[0], qn_a[0], kn_a[0], qn_b[0], kn_b[0], S)
    r3 = lambda a: a.reshape(B, S, a.shape[-1])
    qa, ka, qbn, kbn, qcat, kcat = map(r3, (qa, ka, qbn, kbn, qcat, kcat))
    vat, vbt = (a.reshape(B, S // KT, a.shape[1], KT) for a in (vat, vbt))
    wt = wi.reshape(B, S // TM_PROJ, WI_ROWS, TM_PROJ)
    lam4 = jnp.stack([lam_q1[0], lam_k1[0], lam_q2[0], lam_k2[0]])

    kaug, qaug_a = _alibi_tables(S, SLOPES_A)
    _, qaug_b = _alibi_tables(S, SLOPES_B)
    ya, yb = qa, qbn
    for qb in range(S // TQ):
        ya, yb = _attn_block(qb, qcat, kcat, wt, ya, ka, vat, kaug, qaug_a,
                             lam4, subln_g[0], yb, kbn, vbt, qaug_b)
    ya = ya.reshape(N, W_A)
    yb = yb.reshape(N, W_B)

    out = _merge_ffn(x2, mod3, ya, yb, gates, w_up_a[0], w_up_b[0], w_o[0], norm2_g[0],
                     w_ff1[0], w_ff3[0], w_ff2[0], S)
    return out.reshape(B, S, D)
```

```python
import functools

import numpy as np
import jax
import jax.numpy as jnp
from jax import lax
from jax.experimental import pallas as pl
from jax.experimental.pallas import tpu as pltpu

F32 = jnp.float32
BF16 = jnp.bfloat16
I32 = jnp.int32

D_MODEL = 1024
CHUNK = 64
H_A, D_A = 8, 64
H_I, D_I = 4, 64
TOPK_MAX = 256
H_B, D_B = 4, 64
D_FF = 2816
RMS_EPS = 1e-6
MASK_VALUE = -1e30
LAMBDA_INIT = 0.2
LOG2E = 1.4426950408889634
SLOPES_A = [2.0 ** (-8.0 * (h + 1) / H_A) for h in range(H_A)]
SLOPES_B = [2.0 ** (-8.0 * (h + 1) / H_B) for h in range(H_B)]

W_A = H_A * D_A
W_B = H_B * 2 * D_B
W_QKV = 3 * W_A + 3 * W_B
W_IDX = H_I * D_I + D_I + H_I
W_IDX_PAD = 384
W_G = 2 * D_MODEL

LANES = 128
IDX_K = 2 * LANES
BF16_EXACT_INT = 256
WI_ROWS = 8

TQ = 256
KT = TQ
TM_PROJ = 2 * KT
TM_FFN = 512
CNT_ACCS = 4
ATTN_UNROLL = True
VMEM_LIMIT = 56 * 1024 * 1024

NT_DIMS = (((1,), (1,)), ((), ()))


def _dot(a, b):
    return jnp.dot(a, b, preferred_element_type=F32)


def _dot_nt(a, b):
    return lax.dot_general(a, b, NT_DIMS, preferred_element_type=F32)


def _split(x):
    hi = x.astype(BF16)
    lo = (x - hi.astype(F32)).astype(BF16)
    return hi, lo


def _dot3(a, b_hi, b_lo):
    a_hi, a_lo = _split(a)
    return _dot(a_hi, b_hi) + _dot(a_hi, b_lo) + _dot(a_lo, b_hi)


def _const_spec(shape):
    nd = len(shape)
    return pl.BlockSpec(shape, lambda *_: (0,) * nd, pipeline_mode=pl.Buffered(1))


def _mod_kernel(c_ref, w_ref, b_ref, o_ref):
    c = c_ref[...]
    cs = c * jax.nn.sigmoid(c)
    w_hi, w_lo = _split(w_ref[...])
    o_ref[...] = _dot3(cs, w_hi, w_lo) + b_ref[...]


def _modulation(c, w_ada, b_ada):
    B = c.shape[0]
    n = w_ada.shape[1]
    tn = 1024
    return pl.pallas_call(
        _mod_kernel,
        grid=(n // tn,),
        in_specs=[pl.BlockSpec((B, D_MODEL), lambda j: (0, 0)),
                  pl.BlockSpec((D_MODEL, tn), lambda j: (0, j)),
                  pl.BlockSpec((1, tn), lambda j: (0, j))],
        out_specs=pl.BlockSpec((B, tn), lambda j: (0, j)),
        out_shape=jax.ShapeDtypeStruct((B, n), F32),
        compiler_params=pltpu.CompilerParams(vmem_limit_bytes=VMEM_LIMIT),
        name="adaln_mod",
    )(c, w_ada, b_ada.reshape(1, n))


def _proj_kernel(x_ref, mod_ref, g1_ref, wqk_ref, wvt_ref, wih_ref, wil_ref, wg_ref, grp_ref,
                 gains_ref,
                 qa_ref, ka_ref, qb_ref, kb_ref, vat_ref, vbt_ref, qcat_ref, kcat_ref, wi_ref,
                 gates_ref):
    x = x_ref[...]
    ms = jnp.mean(x * x, axis=-1, keepdims=True)
    h = x * lax.rsqrt(ms + RMS_EPS) * g1_ref[...]
    h = h * (1.0 + mod_ref[1:2, :]) + mod_ref[0:1, :]
    h_hi, h_lo = _split(h)

    qk = _dot(h_hi, wqk_ref[...])
    vt = _dot_nt(wvt_ref[...], h_hi)
    pidx = (_dot(h_hi, wih_ref[...]) + _dot(h_hi, wil_ref[...]) + _dot(h_lo, wih_ref[...]))
    gate_logits = _dot(h_hi, wg_ref[...])

    grp = grp_ref[...]
    zs = [qk[:, i * W_A:(i + 1) * W_A] for i in range(4)]
    squares = [(z * z).astype(BF16) for z in zs]
    means = [_dot(sq, grp) * (1.0 / D_A) for sq in squares]
    for i, out_ref in enumerate((qa_ref, ka_ref, qb_ref, kb_ref)):
        out_ref[...] = (zs[i] * lax.rsqrt(means[i] + RMS_EPS) * gains_ref[i:i + 1, :]).astype(BF16)

    for s in range(x.shape[0] // KT):
        vat_ref[s] = vt[0:W_A, s * KT:(s + 1) * KT].astype(BF16)
        vbt_ref[s] = vt[W_A:, s * KT:(s + 1) * KT].astype(BF16)

    low_half = lax.broadcasted_iota(I32, (x.shape[0], LANES), 1) < D_I
    for g in range(H_I // 2):
        pair = pidx[:, g * LANES:(g + 1) * LANES]
        hi = pair.astype(BF16).astype(F32)
        lo = pair - hi
        hi_sw = pltpu.roll(hi, D_I, 1)
        lo_sw = pltpu.roll(lo, D_I, 1)
        base = 2 * g * IDX_K
        for part, val in enumerate((jnp.where(low_half, hi, lo_sw), jnp.where(low_half, hi, 0.0),
                                    jnp.where(low_half, hi_sw, lo), jnp.where(low_half, hi_sw, 0.0))):
            qcat_ref[:, base + part * LANES:base + (part + 1) * LANES] = val.astype(BF16)
    tail = pidx[:, 2 * LANES:3 * LANES]
    hi = tail.astype(BF16).astype(F32)
    lo = tail - hi
    kcat_ref[:, 0:LANES] = jnp.where(low_half, hi, pltpu.roll(hi, D_I, 1)).astype(BF16)
    kcat_ref[:, LANES:2 * LANES] = jnp.where(low_half, lo, 0.0).astype(BF16)
    wi_ref[...] = tail.T[D_I:D_I + WI_ROWS, :]

    gates_ref[...] = jax.nn.sigmoid(gate_logits).astype(BF16)


def _projection(x2, mod3, norm1_g, w_in, qn_a, kn_a, qn_b, kn_b, S):
    N = x2.shape[0]
    tm = TM_PROJ
    per_batch = S // tm
    w_qk = jnp.concatenate([w_in[:, 0:2 * W_A], w_in[:, 3 * W_A:3 * W_A + 2 * W_B]],
                           axis=1).astype(BF16)
    w_vt = jnp.concatenate([w_in[:, 2 * W_A:3 * W_A], w_in[:, 3 * W_A + 2 * W_B:W_QKV]],
                           axis=1).T.astype(BF16)
    w_idx = jnp.pad(w_in[:, W_QKV:W_QKV + W_IDX], ((0, 0), (0, W_IDX_PAD - W_IDX)))
    w_idx_hi = w_idx.astype(BF16)
    w_idx_lo = (w_idx - w_idx_hi.astype(F32)).astype(BF16)
    w_g = w_in[:, W_QKV + W_IDX:].astype(BF16)
    grp = jnp.asarray(np.kron(np.eye(W_A // D_A), np.ones((D_A, D_A))), BF16)
    gains = jnp.stack([jnp.tile(qn_a, H_A) * (LOG2E * D_A ** -0.5), jnp.tile(kn_a, H_A),
                       jnp.tile(qn_b, 2 * H_B) * (LOG2E * D_B ** -0.5), jnp.tile(kn_b, 2 * H_B)])

    def row_spec(w):
        return pl.BlockSpec((tm, w), lambda t: (t, 0))

    def rows(w, dt):
        return row_spec(w), jax.ShapeDtypeStruct((N, w), dt)

    def transposed(w):
        return (pl.BlockSpec((tm // KT, w, KT), lambda t: (t, 0, 0)),
                jax.ShapeDtypeStruct((N // KT, w, KT), BF16))

    outs = [rows(W_A, BF16), rows(W_A, BF16), rows(W_B, BF16), rows(W_B, BF16),
            transposed(W_A), transposed(W_B),
            rows(H_I * IDX_K, BF16), rows(IDX_K, BF16),
            (pl.BlockSpec((None, WI_ROWS, tm), lambda t: (t, 0, 0)),
             jax.ShapeDtypeStruct((N // tm, WI_ROWS, tm), F32)),
            rows(W_G, BF16)]
    return pl.pallas_call(
        _proj_kernel,
        grid=(N // tm,),
        in_specs=[row_spec(D_MODEL),
                  pl.BlockSpec((None, 6, D_MODEL), lambda t: (t // per_batch, 0, 0)),
                  _const_spec((1, D_MODEL)),
                  _const_spec((D_MODEL, 2 * W_A + 2 * W_B)),
                  _const_spec((W_A + W_B, D_MODEL)),
                  _const_spec((D_MODEL, W_IDX_PAD)),
                  _const_spec((D_MODEL, W_IDX_PAD)),
                  _const_spec((D_MODEL, W_G)),
                  _const_spec((W_A, W_A)),
                  _const_spec((4, W_A))],
        out_specs=[o[0] for o in outs],
        out_shape=[o[1] for o in outs],
        compiler_params=pltpu.CompilerParams(dimension_semantics=("parallel",),
                                             vmem_limit_bytes=VMEM_LIMIT),
        name="in_proj",
    )(x2, mod3, norm1_g.reshape(1, D_MODEL), w_qk, w_vt, w_idx_hi, w_idx_lo, w_g, grp, gains)


def _count_ge(ref, n_rows, cand):
    accs = [jnp.zeros((8, TQ), F32) for _ in range(CNT_ACCS)]
    for r in range(n_rows // 8):
        blk = ref[r * 8:(r + 1) * 8, :]
        accs[r % CNT_ACCS] = accs[r % CNT_ACCS] + jnp.where(blk >= cand, 1.0, 0.0)
    return jnp.sum(functools.reduce(lambda a, b: a + b, accs), axis=0, keepdims=True)


def _count_ge_bf16(ref, n_rows, cand):
    rows = 16
    one, zero = jnp.ones((), BF16), jnp.zeros((), BF16)

    assert n_rows // rows <= BF16_EXACT_INT
    accs = [jnp.zeros((rows, TQ), BF16) for _ in range(CNT_ACCS)]
    for r in range(n_rows // rows):
        blk = ref[r * rows:(r + 1) * rows, :]
        accs[r % CNT_ACCS] = accs[r % CNT_ACCS] + jnp.where(blk >= cand, one, zero)
    total = functools.reduce(lambda a, b: a + b, [a.astype(F32) for a in accs])
    return jnp.sum(total, axis=0, keepdims=True)


def _rank_to_f32(rank):
    key = rank ^ jnp.int32(-2 ** 31)
    return pltpu.bitcast(jnp.where(key < 0, key ^ jnp.int32(0x7FFFFFFF), key), F32)


def _store_bias(bias_ref, bias_t, L):
    bias_ref[0:L, :] = bias_t


def _select_bias(qb, qcat_ref, kcat_ref, wt_ref, score_ref, coarse_ref, eqi_ref, bias_ref):
    L = (qb + 1) * TQ
    admissible = ((lax.broadcasted_iota(I32, (KT, TQ), 0) // CHUNK)
                  <= (lax.broadcasted_iota(I32, (KT, TQ), 1) // CHUNK))
    if L <= TOPK_MAX:
        _store_bias(bias_ref, jnp.where(admissible, 0.0, MASK_VALUE), L)
        return

    for j in range(L // KT):
        rows = slice(j * KT, (j + 1) * KT)
        score = jnp.zeros((KT, TQ), F32)
        for h in range(H_I):
            logits = _dot_nt(kcat_ref[rows, :], qcat_ref[:, h * IDX_K:(h + 1) * IDX_K])
            score = score + wt_ref[h:h + 1, :] * jnp.maximum(logits, 0.0)
        if j == qb:
            score = jnp.where(admissible, score, -jnp.inf)
        score_ref[rows, :] = score
        coarse_ref[rows, :] = score.astype(BF16)
    k_sel = float(TOPK_MAX)

    def coarse_body(it, rank):
        cand = rank | lax.shift_left(jnp.int32(1), 31 - it)
        c = _count_ge_bf16(coarse_ref, L, _rank_to_f32(cand).astype(BF16))
        return jnp.where(c >= k_sel, cand, rank)
    base = lax.fori_loop(0, 16, coarse_body, jnp.zeros((1, TQ), I32)) - jnp.int32(1 << 16)

    def fine_body(it, st):
        off, cnt = st
        cand = off | lax.shift_left(jnp.int32(1), 16 - it)
        c = _count_ge(score_ref, L, _rank_to_f32(base + cand))
        take = c >= k_sel
        return jnp.where(take, cand, off), jnp.where(take, c, cnt)
    off, cnt = lax.fori_loop(
        0, 17, fine_body, (jnp.zeros((1, TQ), I32), jnp.full((1, TQ), float(L), F32)))
    rank = base + off
    thr = _rank_to_f32(rank)
    tied = jnp.max(cnt) > k_sel

    @pl.when(jnp.logical_not(tied))
    def _():
        _store_bias(bias_ref, jnp.where(score_ref[0:L, :] >= thr, 0.0, MASK_VALUE), L)

    @pl.when(tied)
    def _():
        above = _rank_to_f32(rank + 1)
        need = k_sel - _count_ge(score_ref, L, above)
        big = jnp.int32(1 << 30)
        sc = score_ref[0:L, :]
        s_idx = lax.broadcasted_iota(I32, (L, TQ), 0)
        eqi_ref[0:L, :] = jnp.where(sc >= above, big, jnp.where(sc >= thr, s_idx, big))
        n_bits = max(1, int(np.ceil(np.log2(L))))

        def idx_body(it, j):
            cand = j | lax.shift_left(jnp.int32(1), n_bits - 1 - it)
            cnt_lt = float(L) - _count_ge(eqi_ref, L, cand)
            return jnp.where(cnt_lt < need, cand, j)
        j_sel = lax.fori_loop(0, n_bits, idx_body, jnp.zeros((1, TQ), I32))
        sel = jnp.where(score_ref[0:L, :] >= above, 0.0,
                        jnp.where(eqi_ref[0:L, :] <= j_sel, 0.0, MASK_VALUE))
        _store_bias(bias_ref, sel, L)


def _tile_offsets():
    return (lax.broadcasted_iota(I32, (KT, TQ), 1)
            - lax.broadcasted_iota(I32, (KT, TQ), 0)).astype(F32)


def _masked_halves(q_ref, n_groups):
    lane = lax.broadcasted_iota(I32, (TQ, LANES), 1)
    low = jnp.where(lane < 64, 1.0, 0.0)
    high = 1.0 - low
    out = []
    for g in range(n_groups):
        q2 = q_ref[:, g * LANES:(g + 1) * LANES].astype(F32)
        out += [(q2 * low).astype(BF16), (q2 * high).astype(BF16)]
    return out


ONES_ROWS = 16


def _with_ones(vt):
    return jnp.concatenate([vt, jnp.ones((ONES_ROWS, vt.shape[1]), BF16)], axis=0)


def _online_steps(scores, values, maxes, accs):
    ss = [f() for f in scores]
    out = []
    for s, v, m, (acc_ref, i) in zip(ss, values, maxes, accs):
        m_new = jnp.maximum(m, jnp.max(s, axis=0, keepdims=True))
        p = jnp.exp2((s - m_new).astype(BF16))
        acc_ref[i] = jnp.exp2(m - m_new) * acc_ref[i] + _dot(_with_ones(v()), p)
        out.append(m_new)
    return tuple(out)


def _softmax_init(*acc_refs):
    for acc_ref in acc_refs:
        acc_ref[...] = jnp.zeros(acc_ref.shape, F32)
    n = sum(r.shape[0] for r in acc_refs)
    return tuple(jnp.full((1, TQ), -jnp.inf, F32) for _ in range(n))


def _softmax_result(acc_ref, i):
    rows = acc_ref.shape[1] - ONES_ROWS
    return acc_ref[i, 0:rows, :] / acc_ref[i, rows:rows + 1, :]


def _key_rows(j):
    return pl.ds(pl.multiple_of(j * KT, KT), KT)


def _split3(x):
    a = x.astype(BF16)
    r = x - a.astype(F32)
    b = r.astype(BF16)
    return a, b, (r - b.astype(F32)).astype(BF16)


def _alibi_tables(S, slopes):
    pos = jnp.arange(S, dtype=I32)
    hi = (pos // 64 * 64).astype(F32)
    lo = (pos % 64).astype(F32)
    lam = jnp.full((S,), LOG2E, F32)
    k_cols = [*_split3(lam), *_split3(lam), *_split3(lam * hi), *_split3(lam * lo)]
    kaug = jnp.pad(jnp.stack(k_cols, axis=1), ((0, 0), (0, LANES - len(k_cols))))
    qaug = []
    for sl in slopes:
        q_cols = [-sl * hi] * 3 + [-sl * lo] * 3 + [jnp.full((S,), sl, F32)] * 6
        qaug.append(jnp.pad(jnp.stack(q_cols, axis=1), ((0, 0), (0, LANES - len(q_cols)))))
    return kaug, jnp.stack(qaug).astype(BF16)


def _attn_kernel(qb, qcat_ref, kcat_ref, wt_ref, qa_ref, ka_ref, vat_ref, kaug_ref, qaug_a_ref,
                 lam_ref, sg_ref, qb_ref, kb_ref, vbt_ref, qaug_b_ref, oa_ref, ob_ref,
                 score_ref, coarse_ref, eqi_ref, bias_ref, acc_a_ref, acc_b_ref):
    _select_bias(qb, qcat_ref, kcat_ref, wt_ref, score_ref, coarse_ref, eqi_ref, bias_ref)
    n_b = 2 * H_B
    qms_a = _masked_halves(qa_ref, H_A // 2)
    qms_b = _masked_halves(qb_ref, H_B)
    qx_a = [jnp.concatenate([qms_a[h], qaug_a_ref[h]], axis=1) for h in range(H_A)]
    qx_b = [jnp.concatenate([qms_b[i], qaug_b_ref[i // 2]], axis=1) for i in range(n_b)]

    def mix(xs):
        return [xs[i // 2 + (i % 2) * H_A] for i in range(H_A + n_b)]
    accs = mix([(acc_a_ref, h) for h in range(H_A)] + [(acc_b_ref, i) for i in range(n_b)])

    def values(j):
        return ([lambda h=h: vat_ref[j, h * D_A:(h + 1) * D_A, :] for h in range(H_A)]
                + [lambda i=i: vbt_ref[j, (i // 2) * LANES:(i // 2 + 1) * LANES, :]
                   for i in range(n_b)])

    def off_diagonal(j, maxes):
        rows = _key_rows(j)
        bias_t = bias_ref[rows, :]
        kaug_t = kaug_ref[rows, :]
        kx_a = [jnp.concatenate([ka_ref[rows, g * LANES:(g + 1) * LANES], kaug_t], axis=1)
                for g in range(H_A // 2)]
        kx_b = [jnp.concatenate([kb_ref[rows, h * LANES:(h + 1) * LANES], kaug_t], axis=1)
                for h in range(H_B)]
        scores = ([lambda h=h: _dot_nt(kx_a[h // 2], qx_a[h]) + bias_t for h in range(H_A)]
                  + [lambda i=i: _dot_nt(kx_b[i // 2], qx_b[i]) for i in range(n_b)])
        return _online_steps(mix(scores), mix(values(j)), maxes, accs)

    maxes = lax.fori_loop(0, qb, off_diagonal, _softmax_init(acc_a_ref, acc_b_ref),
                          unroll=ATTN_UNROLL)

    diag = slice(qb * KT, (qb + 1) * KT)
    bias_t = bias_ref[diag, :]
    dist = jnp.abs(_tile_offsets())
    allowed = ((lax.broadcasted_iota(I32, (KT, TQ), 0) // CHUNK)
               <= (lax.broadcasted_iota(I32, (KT, TQ), 1) // CHUNK))
    biases_b = [jnp.where(allowed, -(LOG2E * sl) * dist, MASK_VALUE) for sl in SLOPES_B]
    scores = ([lambda h=h: (_dot_nt(ka_ref[diag, (h // 2) * LANES:(h // 2 + 1) * LANES], qms_a[h])
                            + (bias_t - (LOG2E * SLOPES_A[h]) * dist)) for h in range(H_A)]
              + [lambda i=i: (_dot_nt(kb_ref[diag, (i // 2) * LANES:(i // 2 + 1) * LANES], qms_b[i])
                              + biases_b[i // 2]) for i in range(n_b)])
    _online_steps(mix(scores), mix(values(qb)), maxes, accs)

    out_t = jnp.concatenate([_softmax_result(acc_a_ref, h) for h in range(H_A)], axis=0)
    oa_ref[...] = out_t.T.astype(BF16)

    lam = (jnp.exp(jnp.sum(lam_ref[0:1, :] * lam_ref[1:2, :], axis=-1, keepdims=True))
           - jnp.exp(jnp.sum(lam_ref[2:3, :] * lam_ref[3:4, :], axis=-1, keepdims=True))
           + LAMBDA_INIT)
    ys = []
    for h in range(H_B):
        o = _softmax_result(acc_b_ref, 2 * h) - lam * _softmax_result(acc_b_ref, 2 * h + 1)
        ys.append(o * lax.rsqrt(jnp.mean(o * o, axis=0, keepdims=True) + RMS_EPS))
    y = jnp.concatenate(ys, axis=0).T
    ob_ref[...] = (y * sg_ref[...] * (1.0 - LAMBDA_INIT)).astype(BF16)


def _attn_block(qb, qcat, kcat, wt, qa, ka, vat, kaug, qaug_a, lam4, subln_g, qbn, kbn, vbt,
                qaug_b):
    B, S = qa.shape[0], qa.shape[1]
    L = (qb + 1) * TQ

    def q_rows(w):
        return pl.BlockSpec((None, TQ, w), lambda b: (b, qb, 0))

    def key_rows(w):
        return pl.BlockSpec((None, L, w), lambda b: (b, 0, 0))

    def value_tiles(w):
        return pl.BlockSpec((None, qb + 1, w, KT), lambda b: (b, 0, 0, 0))

    def qaug_spec(h):
        return pl.BlockSpec((h, TQ, LANES), lambda b: (0, qb, 0))

    return pl.pallas_call(
        functools.partial(_attn_kernel, qb),
        grid=(B,),
        in_specs=[q_rows(H_I * IDX_K), key_rows(IDX_K),
                  pl.BlockSpec((None, None, WI_ROWS, TQ),
                               lambda b: (b, qb // (TM_PROJ // TQ), 0, qb % (TM_PROJ // TQ))),
                  q_rows(W_A), key_rows(W_A), value_tiles(W_A),
                  pl.BlockSpec((L, LANES), lambda b: (0, 0)), qaug_spec(H_A),
                  pl.BlockSpec((4, D_B), lambda b: (0, 0)),
                  pl.BlockSpec((1, W_B), lambda b: (0, 0)),
                  q_rows(W_B), key_rows(W_B), value_tiles(W_B), qaug_spec(H_B)],
        out_specs=[q_rows(W_A), q_rows(W_B)],
        out_shape=[jax.ShapeDtypeStruct(qa.shape, BF16), jax.ShapeDtypeStruct(qbn.shape, BF16)],
        input_output_aliases={3: 0, 10: 1},
        scratch_shapes=[pltpu.VMEM((S, TQ), F32), pltpu.VMEM((S, TQ), BF16),
                        pltpu.VMEM((S, TQ), I32), pltpu.VMEM((S, TQ), F32),
                        pltpu.VMEM((H_A, D_A + ONES_ROWS, TQ), F32),
                        pltpu.VMEM((2 * H_B, 2 * D_B + ONES_ROWS, TQ), F32)],
        compiler_params=pltpu.CompilerParams(dimension_semantics=("parallel",),
                                             vmem_limit_bytes=VMEM_LIMIT),
        name=f"attn_q{qb}",
    )(qcat, kcat, wt, qa, ka, vat, kaug, qaug_a,
      lam4, jnp.tile(subln_g, H_B).reshape(1, W_B), qbn, kbn, vbt, qaug_b)


def _ffn_kernel(x_ref, mod_ref, ya_ref, yb_ref, gates_ref, wua_ref, wub_ref, wo_ref, g2_ref,
                w1_ref, w3_ref, w2_ref, o_ref):
    gates = gates_ref[...].astype(F32)
    merged = (gates[:, :D_MODEL] * _dot(ya_ref[...], wua_ref[...])
              + gates[:, D_MODEL:] * _dot(yb_ref[...], wub_ref[...]))
    x1 = x_ref[...] + mod_ref[2:3, :] * _dot(merged.astype(BF16), wo_ref[...])
    ms = jnp.mean(x1 * x1, axis=-1, keepdims=True)
    h2 = x1 * lax.rsqrt(ms + RMS_EPS) * g2_ref[...]
    h2 = (h2 * (1.0 + mod_ref[4:5, :]) + mod_ref[3:4, :]).astype(BF16)
    u = _dot(h2, w1_ref[...])
    act = (u * jax.nn.sigmoid(u) * _dot(h2, w3_ref[...])).astype(BF16)
    o_ref[...] = x1 + mod_ref[5:6, :] * _dot(act, w2_ref[...])


def _merge_ffn(x2, mod3, ya, yb, gates, w_up_a, w_up_b, w_o, norm2_g, w_ff1, w_ff3, w_ff2, S):
    N = x2.shape[0]
    tm = TM_FFN
    per_batch = S // tm

    def row_spec(w):
        return pl.BlockSpec((tm, w), lambda t: (t, 0))

    return pl.pallas_call(
        _ffn_kernel,
        grid=(N // tm,),
        in_specs=[row_spec(D_MODEL),
                  pl.BlockSpec((None, 6, D_MODEL), lambda t: (t // per_batch, 0, 0)),
                  row_spec(W_A), row_spec(W_B), row_spec(W_G),
                  _const_spec((W_A, D_MODEL)), _const_spec((W_B, D_MODEL)),
                  _const_spec((D_MODEL, D_MODEL)), _const_spec((1, D_MODEL)),
                  _const_spec((D_MODEL, D_FF)), _const_spec((D_MODEL, D_FF)),
                  _const_spec((D_FF, D_MODEL))],
        out_specs=row_spec(D_MODEL),
        out_shape=jax.ShapeDtypeStruct((N, D_MODEL), F32),
        compiler_params=pltpu.CompilerParams(dimension_semantics=("parallel",),
                                             vmem_limit_bytes=VMEM_LIMIT),
        name="merge_ffn",
    )(x2, mod3, ya, yb, gates, w_up_a.astype(BF16), w_up_b.astype(BF16), w_o.astype(BF16),
      norm2_g.reshape(1, D_MODEL), w_ff1.astype(BF16), w_ff3.astype(BF16), w_ff2.astype(BF16))


def kernel(x, c, w_ada, b_ada, norm1_g, w_in, qn_a, kn_a, qn_b, kn_b, lam_q1, lam_k1, lam_q2,
           lam_k2, subln_g, w_up_a, w_up_b, w_o, norm2_g, w_ff1, w_ff3, w_ff2):
    B, S, D = x.shape
    assert D == D_MODEL and S % TQ == 0 and w_ada.shape[0] == 1
    N = B * S
    x2 = x.reshape(N, D)
    mod3 = _modulation(c, w_ada[0], b_ada[0]).reshape(B, 6, D)

    (qa, ka, qbn, kbn, vat, vbt, qcat, kcat, wi, gates) = _projection(
        x2, mod3, norm1_g[0], w_in[0], qn_a[0], kn_a[0], qn_b[0], kn_b[0], S)
    r3 = lambda a: a.reshape(B, S, a.shape[-1])
    qa, ka, qbn, kbn, qcat, kcat = map(r3, (qa, ka, qbn, kbn, qcat, kcat))
    vat, vbt = (a.reshape(B, S // KT, a.shape[1], KT) for a in (vat, vbt))
    wt = wi.reshape(B, S // TM_PROJ, WI_ROWS, TM_PROJ)
    lam4 = jnp.stack([lam_q1[0], lam_k1[0], lam_q2[0], lam_k2[0]])

    kaug, qaug_a = _alibi_tables(S, SLOPES_A)
    _, qaug_b = _alibi_tables(S, SLOPES_B)
    ya, yb = qa, qbn
    for qb in range(S // TQ):
        ya, yb = _attn_block(qb, qcat, kcat, wt, ya, ka, vat, kaug, qaug_a,
                             lam4, subln_g[0], yb, kbn, vbt, qaug_b)
    ya = ya.reshape(N, W_A)
    yb = yb.reshape(N, W_B)

    out = _merge_ffn(x2, mod3, ya, yb, gates, w_up_a[0], w_up_b[0], w_o[0], norm2_g[0],
                     w_ff1[0], w_ff3[0], w_ff2[0], S)
    return out.reshape(B, S, D)
```

```python
import functools

import numpy as np
import jax
import jax.numpy as jnp
from jax import lax
from jax.experimental import pallas as pl
from jax.experimental.pallas import tpu as pltpu

F32 = jnp.float32
BF16 = jnp.bfloat16
I32 = jnp.int32

D_MODEL = 1024
CHUNK = 64
H_A, D_A = 8, 64
H_I, D_I = 4, 64
TOPK_MAX = 256
H_B, D_B = 4, 64
D_FF = 2816
RMS_EPS = 1e-6
MASK_VALUE = -1e30
LAMBDA_INIT = 0.2
LOG2E = 1.4426950408889634
SLOPES_A = [2.0 ** (-8.0 * (h + 1) / H_A) for h in range(H_A)]
SLOPES_B = [2.0 ** (-8.0 * (h + 1) / H_B) for h in range(H_B)]

W_A = H_A * D_A
W_B = H_B * 2 * D_B
W_QKV = 3 * W_A + 3 * W_B
W_IDX = H_I * D_I + D_I + H_I
W_IDX_PAD = 384
W_G = 2 * D_MODEL

LANES = 128
IDX_K = 2 * LANES
BF16_EXACT_INT = 256
WI_ROWS = 8

TQ = 256
KT = TQ
TM_PROJ = 2 * KT
TM_FFN = 512
CNT_ACCS = 4
VMEM_LIMIT = 56 * 1024 * 1024

NT_DIMS = (((1,), (1,)), ((), ()))


def _dot(a, b):
    return jnp.dot(a, b, preferred_element_type=F32)


def _dot_nt(a, b):
    return lax.dot_general(a, b, NT_DIMS, preferred_element_type=F32)


def _split(x):
    hi = x.astype(BF16)
    lo = (x - hi.astype(F32)).astype(BF16)
    return hi, lo


def _dot3(a, b_hi, b_lo):
    a_hi, a_lo = _split(a)
    return _dot(a_hi, b_hi) + _dot(a_hi, b_lo) + _dot(a_lo, b_hi)


def _const_spec(shape):
    nd = len(shape)
    return pl.BlockSpec(shape, lambda *_: (0,) * nd, pipeline_mode=pl.Buffered(1))


def _mod_kernel(c_ref, w_ref, b_ref, o_ref):
    c = c_ref[...]
    cs = c * jax.nn.sigmoid(c)
    w_hi, w_lo = _split(w_ref[...])
    o_ref[...] = _dot3(cs, w_hi, w_lo) + b_ref[...]


def _modulation(c, w_ada, b_ada):
    B = c.shape[0]
    n = w_ada.shape[1]
    tn = 1024
    return pl.pallas_call(
        _mod_kernel,
        grid=(n // tn,),
        in_specs=[pl.BlockSpec((B, D_MODEL), lambda j: (0, 0)),
                  pl.BlockSpec((D_MODEL, tn), lambda j: (0, j)),
                  pl.BlockSpec((1, tn), lambda j: (0, j))],
        out_specs=pl.BlockSpec((B, tn), lambda j: (0, j)),
        out_shape=jax.ShapeDtypeStruct((B, n), F32),
        compiler_params=pltpu.CompilerParams(vmem_limit_bytes=VMEM_LIMIT),
        name="adaln_mod",
    )(c, w_ada, b_ada.reshape(1, n))


def _proj_kernel(x_ref, mod_ref, g1_ref, wqk_ref, wvt_ref, wih_ref, wil_ref, wg_ref, grp_ref,
                 gains_ref,
                 qa_ref, ka_ref, qb_ref, kb_ref, vat_ref, vbt_ref, qcat_ref, kcat_ref, wi_ref,
                 gates_ref):
    x = x_ref[...]
    ms = jnp.mean(x * x, axis=-1, keepdims=True)
    h = x * lax.rsqrt(ms + RMS_EPS) * g1_ref[...]
    h = h * (1.0 + mod_ref[1:2, :]) + mod_ref[0:1, :]
    h_hi, h_lo = _split(h)

    qk = _dot(h_hi, wqk_ref[...])
    vt = _dot_nt(wvt_ref[...], h_hi)
    pidx = (_dot(h_hi, wih_ref[...]) + _dot(h_hi, wil_ref[...]) + _dot(h_lo, wih_ref[...]))
    gate_logits = _dot(h_hi, wg_ref[...])

    grp = grp_ref[...]
    zs = [qk[:, i * W_A:(i + 1) * W_A] for i in range(4)]
    squares = [(z * z).astype(BF16) for z in zs]
    means = [_dot(sq, grp) * (1.0 / D_A) for sq in squares]
    for i, out_ref in enumerate((qa_ref, ka_ref, qb_ref, kb_ref)):
        out_ref[...] = (zs[i] * lax.rsqrt(means[i] + RMS_EPS) * gains_ref[i:i + 1, :]).astype(BF16)

    for s in range(x.shape[0] // KT):
        vat_ref[s] = vt[0:W_A, s * KT:(s + 1) * KT].astype(BF16)
        vbt_ref[s] = vt[W_A:, s * KT:(s + 1) * KT].astype(BF16)

    low_half = lax.broadcasted_iota(I32, (x.shape[0], LANES), 1) < D_I
    for g in range(H_I // 2):
        pair = pidx[:, g * LANES:(g + 1) * LANES]
        hi = pair.astype(BF16).astype(F32)
        lo = pair - hi
        hi_sw = pltpu.roll(hi, D_I, 1)
        lo_sw = pltpu.roll(lo, D_I, 1)
        base = 2 * g * IDX_K
        for part, val in enumerate((jnp.where(low_half, hi, lo_sw), jnp.where(low_half, hi, 0.0),
                                    jnp.where(low_half, hi_sw, lo), jnp.where(low_half, hi_sw, 0.0))):
            qcat_ref[:, base + part * LANES:base + (part + 1) * LANES] = val.astype(BF16)
    tail = pidx[:, 2 * LANES:3 * LANES]
    hi = tail.astype(BF16).astype(F32)
    lo = tail - hi
    kcat_ref[:, 0:LANES] = jnp.where(low_half, hi, pltpu.roll(hi, D_I, 1)).astype(BF16)
    kcat_ref[:, LANES:2 * LANES] = jnp.where(low_half, lo, 0.0).astype(BF16)
    wi_ref[...] = tail.T[D_I:D_I + WI_ROWS, :]

    gates_ref[...] = jax.nn.sigmoid(gate_logits).astype(BF16)


def _projection(x2, mod3, norm1_g, w_in, qn_a, kn_a, qn_b, kn_b, S):
    N = x2.shape[0]
    tm = TM_PROJ
    per_batch = S // tm
    w_qk = jnp.concatenate([w_in[:, 0:2 * W_A], w_in[:, 3 * W_A:3 * W_A + 2 * W_B]],
                           axis=1).astype(BF16)
    w_vt = jnp.concatenate([w_in[:, 2 * W_A:3 * W_A], w_in[:, 3 * W_A + 2 * W_B:W_QKV]],
                           axis=1).T.astype(BF16)
    w_idx = jnp.pad(w_in[:, W_QKV:W_QKV + W_IDX], ((0, 0), (0, W_IDX_PAD - W_IDX)))
    w_idx_hi = w_idx.astype(BF16)
    w_idx_lo = (w_idx - w_idx_hi.astype(F32)).astype(BF16)
    w_g = w_in[:, W_QKV + W_IDX:].astype(BF16)
    grp = jnp.asarray(np.kron(np.eye(W_A // D_A), np.ones((D_A, D_A))), BF16)
    gains = jnp.stack([jnp.tile(qn_a, H_A) * (LOG2E * D_A ** -0.5), jnp.tile(kn_a, H_A),
                       jnp.tile(qn_b, 2 * H_B) * (LOG2E * D_B ** -0.5), jnp.tile(kn_b, 2 * H_B)])

    def row_spec(w):
        return pl.BlockSpec((tm, w), lambda t: (t, 0))

    def rows(w, dt):
        return row_spec(w), jax.ShapeDtypeStruct((N, w), dt)

    def transposed(w):
        return (pl.BlockSpec((tm // KT, w, KT), lambda t: (t, 0, 0)),
                jax.ShapeDtypeStruct((N // KT, w, KT), BF16))

    outs = [rows(W_A, BF16), rows(W_A, BF16), rows(W_B, BF16), rows(W_B, BF16),
            transposed(W_A), transposed(W_B),
            rows(H_I * IDX_K, BF16), rows(IDX_K, BF16),
            (pl.BlockSpec((None, WI_ROWS, tm), lambda t: (t, 0, 0)),
             jax.ShapeDtypeStruct((N // tm, WI_ROWS, tm), F32)),
            rows(W_G, BF16)]
    return pl.pallas_call(
        _proj_kernel,
        grid=(N // tm,),
        in_specs=[row_spec(D_MODEL),
                  pl.BlockSpec((None, 6, D_MODEL), lambda t: (t // per_batch, 0, 0)),
                  _const_spec((1, D_MODEL)),
                  _const_spec((D_MODEL, 2 * W_A + 2 * W_B)),
                  _const_spec((W_A + W_B, D_MODEL)),
                  _const_spec((D_MODEL, W_IDX_PAD)),
                  _const_spec((D_MODEL, W_IDX_PAD)),
                  _const_spec((D_MODEL, W_G)),
                  _const_spec((W_A, W_A)),
                  _const_spec((4, W_A))],
        out_specs=[o[0] for o in outs],
        out_shape=[o[1] for o in outs],
        compiler_params=pltpu.CompilerParams(dimension_semantics=("parallel",),
                                             vmem_limit_bytes=VMEM_LIMIT),
        name="in_proj",
    )(x2, mod3, norm1_g.reshape(1, D_MODEL), w_qk, w_vt, w_idx_hi, w_idx_lo, w_g, grp, gains)


def _count_ge(ref, n_rows, cand):
    accs = [jnp.zeros((8, TQ), F32) for _ in range(CNT_ACCS)]
    for r in range(n_rows // 8):
        blk = ref[r * 8:(r + 1) * 8, :]
        accs[r % CNT_ACCS] = accs[r % CNT_ACCS] + jnp.where(blk >= cand, 1.0, 0.0)
    return jnp.sum(functools.reduce(lambda a, b: a + b, accs), axis=0, keepdims=True)


def _count_ge_bf16(ref, n_rows, cand):
    rows = 16
    one, zero = jnp.ones((), BF16), jnp.zeros((), BF16)

    assert n_rows // rows <= BF16_EXACT_INT
    accs = [jnp.zeros((rows, TQ), BF16) for _ in range(CNT_ACCS)]
    for r in range(n_rows // rows):
        blk = ref[r * rows:(r + 1) * rows, :]
        accs[r % CNT_ACCS] = accs[r % CNT_ACCS] + jnp.where(blk >= cand, one, zero)
    total = functools.reduce(lambda a, b: a + b, [a.astype(F32) for a in accs])
    return jnp.sum(total, axis=0, keepdims=True)


def _rank_to_f32(rank):
    key = rank ^ jnp.int32(-2 ** 31)
    return pltpu.bitcast(jnp.where(key < 0, key ^ jnp.int32(0x7FFFFFFF), key), F32)


def _store_bias(bias_ref, bias_t, L):
    bias_ref[0:L, :] = bias_t


def _select_bias(qb, qcat_ref, kcat_ref, wt_ref, score_ref, coarse_ref, eqi_ref, bias_ref):
    L = (qb + 1) * TQ
    admissible = ((lax.broadcasted_iota(I32, (KT, TQ), 0) // CHUNK)
                  <= (lax.broadcasted_iota(I32, (KT, TQ), 1) // CHUNK))
    if L <= TOPK_MAX:
        _store_bias(bias_ref, jnp.where(admissible, 0.0, MASK_VALUE), L)
        return

    for j in range(L // KT):
        rows = slice(j * KT, (j + 1) * KT)
        score = jnp.zeros((KT, TQ), F32)
        for h in range(H_I):
            logits = _dot_nt(kcat_ref[rows, :], qcat_ref[:, h * IDX_K:(h + 1) * IDX_K])
            score = score + wt_ref[h:h + 1, :] * jnp.maximum(logits, 0.0)
        if j == qb:
            score = jnp.where(admissible, score, -jnp.inf)
        score_ref[rows, :] = score
        coarse_ref[rows, :] = score.astype(BF16)
    k_sel = float(TOPK_MAX)

    def coarse_body(it, rank):
        cand = rank | lax.shift_left(jnp.int32(1), 31 - it)
        c = _count_ge_bf16(coarse_ref, L, _rank_to_f32(cand).astype(BF16))
        return jnp.where(c >= k_sel, cand, rank)
    base = lax.fori_loop(0, 16, coarse_body, jnp.zeros((1, TQ), I32)) - jnp.int32(1 << 16)

    def fine_body(it, st):
        off, cnt = st
        cand = off | lax.shift_left(jnp.int32(1), 16 - it)
        c = _count_ge(score_ref, L, _rank_to_f32(base + cand))
        take = c >= k_sel
        return jnp.where(take, cand, off), jnp.where(take, c, cnt)
    off, cnt = lax.fori_loop(
        0, 17, fine_body, (jnp.zeros((1, TQ), I32), jnp.full((1, TQ), float(L), F32)))
    rank = base + off
    thr = _rank_to_f32(rank)
    tied = jnp.max(cnt) > k_sel

    @pl.when(jnp.logical_not(tied))
    def _():
        _store_bias(bias_ref, jnp.where(score_ref[0:L, :] >= thr, 0.0, MASK_VALUE), L)

    @pl.when(tied)
    def _():
        above = _rank_to_f32(rank + 1)
        need = k_sel - _count_ge(score_ref, L, above)
        big = jnp.int32(1 << 30)
        sc = score_ref[0:L, :]
        s_idx = lax.broadcasted_iota(I32, (L, TQ), 0)
        eqi_ref[0:L, :] = jnp.where(sc >= above, big, jnp.where(sc >= thr, s_idx, big))
        n_bits = max(1, int(np.ceil(np.log2(L))))

        def idx_body(it, j):
            cand = j | lax.shift_left(jnp.int32(1), n_bits - 1 - it)
            cnt_lt = float(L) - _count_ge(eqi_ref, L, cand)
            return jnp.where(cnt_lt < need, cand, j)
        j_sel = lax.fori_loop(0, n_bits, idx_body, jnp.zeros((1, TQ), I32))
        sel = jnp.where(score_ref[0:L, :] >= above, 0.0,
                        jnp.where(eqi_ref[0:L, :] <= j_sel, 0.0, MASK_VALUE))
        _store_bias(bias_ref, sel, L)


def _tile_offsets():
    return (lax.broadcasted_iota(I32, (KT, TQ), 1)
            - lax.broadcasted_iota(I32, (KT, TQ), 0)).astype(F32)


def _masked_halves(q_ref, n_groups):
    lane = lax.broadcasted_iota(I32, (TQ, LANES), 1)
    low = jnp.where(lane < 64, 1.0, 0.0)
    high = 1.0 - low
    out = []
    for g in range(n_groups):
        q2 = q_ref[:, g * LANES:(g + 1) * LANES].astype(F32)
        out += [(q2 * low).astype(BF16), (q2 * high).astype(BF16)]
    return out


ONES_ROWS = 16


def _with_ones(vt):
    return jnp.concatenate([vt, jnp.ones((ONES_ROWS, vt.shape[1]), BF16)], axis=0)


def _online_steps(ss, values, maxes, accs, next_scores):
    out, nxt = [], []
    for c, (s, v, m, (acc_ref, i)) in enumerate(zip(ss, values, maxes, accs)):
        m_new = jnp.maximum(m, jnp.max(s, axis=0, keepdims=True))
        p = jnp.exp2((s - m_new).astype(BF16))
        acc_ref[i] = jnp.exp2(m - m_new) * acc_ref[i] + _dot(_with_ones(v()), p)
        out.append(m_new)
        if next_scores is not None:
            nxt.append(next_scores[c]())
    return tuple(out), nxt


def _softmax_init(*acc_refs):
    for acc_ref in acc_refs:
        acc_ref[...] = jnp.zeros(acc_ref.shape, F32)
    n = sum(r.shape[0] for r in acc_refs)
    return tuple(jnp.full((1, TQ), -jnp.inf, F32) for _ in range(n))


def _softmax_result(acc_ref, i):
    rows = acc_ref.shape[1] - ONES_ROWS
    return acc_ref[i, 0:rows, :] / acc_ref[i, rows:rows + 1, :]


def _split3(x):
    a = x.astype(BF16)
    r = x - a.astype(F32)
    b = r.astype(BF16)
    return a, b, (r - b.astype(F32)).astype(BF16)


def _alibi_tables(S, slopes):
    pos = jnp.arange(S, dtype=I32)
    hi = (pos // 64 * 64).astype(F32)
    lo = (pos % 64).astype(F32)
    lam = jnp.full((S,), LOG2E, F32)
    k_cols = [*_split3(lam), *_split3(lam), *_split3(lam * hi), *_split3(lam * lo)]
    kaug = jnp.pad(jnp.stack(k_cols, axis=1), ((0, 0), (0, LANES - len(k_cols))))
    qaug = []
    for sl in slopes:
        q_cols = [-sl * hi] * 3 + [-sl * lo] * 3 + [jnp.full((S,), sl, F32)] * 6
        qaug.append(jnp.pad(jnp.stack(q_cols, axis=1), ((0, 0), (0, LANES - len(q_cols)))))
    return kaug, jnp.stack(qaug).astype(BF16)


def _attn_kernel(qb, qcat_ref, kcat_ref, wt_ref, qa_ref, ka_ref, vat_ref, kaug_ref, qaug_a_ref,
                 lam_ref, sg_ref, qb_ref, kb_ref, vbt_ref, qaug_b_ref, oa_ref, ob_ref,
                 score_ref, coarse_ref, eqi_ref, bias_ref, acc_a_ref, acc_b_ref):
    _select_bias(qb, qcat_ref, kcat_ref, wt_ref, score_ref, coarse_ref, eqi_ref, bias_ref)
    n_b = 2 * H_B
    qms_a = _masked_halves(qa_ref, H_A // 2)
    qms_b = _masked_halves(qb_ref, H_B)
    qx_a = [jnp.concatenate([qms_a[h], qaug_a_ref[h]], axis=1) for h in range(H_A)]
    qx_b = [jnp.concatenate([qms_b[i], qaug_b_ref[i // 2]], axis=1) for i in range(n_b)]

    def mix(xs):
        return [xs[i // 2 + (i % 2) * H_A] for i in range(H_A + n_b)]
    accs = mix([(acc_a_ref, h) for h in range(H_A)] + [(acc_b_ref, i) for i in range(n_b)])

    def values(j):
        return ([lambda h=h: vat_ref[j, h * D_A:(h + 1) * D_A, :] for h in range(H_A)]
                + [lambda i=i: vbt_ref[j, (i // 2) * LANES:(i // 2 + 1) * LANES, :]
                   for i in range(n_b)])

    def tile_scores(j):
        rows = slice(j * KT, (j + 1) * KT)
        bias_t = bias_ref[rows, :]
        if j < qb:
            kaug_t = kaug_ref[rows, :]
            kx_a = [jnp.concatenate([ka_ref[rows, g * LANES:(g + 1) * LANES], kaug_t], axis=1)
                    for g in range(H_A // 2)]
            kx_b = [jnp.concatenate([kb_ref[rows, h * LANES:(h + 1) * LANES], kaug_t], axis=1)
                    for h in range(H_B)]
            return mix([lambda h=h: _dot_nt(kx_a[h // 2], qx_a[h]) + bias_t for h in range(H_A)]
                       + [lambda i=i: _dot_nt(kx_b[i // 2], qx_b[i]) for i in range(n_b)])
        dist = jnp.abs(_tile_offsets())
        allowed = ((lax.broadcasted_iota(I32, (KT, TQ), 0) // CHUNK)
                   <= (lax.broadcasted_iota(I32, (KT, TQ), 1) // CHUNK))
        biases_b = [jnp.where(allowed, -(LOG2E * sl) * dist, MASK_VALUE) for sl in SLOPES_B]
        return mix(
            [lambda h=h: (_dot_nt(ka_ref[rows, (h // 2) * LANES:(h // 2 + 1) * LANES], qms_a[h])
                          + (bias_t - (LOG2E * SLOPES_A[h]) * dist)) for h in range(H_A)]
            + [lambda i=i: (_dot_nt(kb_ref[rows, (i // 2) * LANES:(i // 2 + 1) * LANES], qms_b[i])
                            + biases_b[i // 2]) for i in range(n_b)])

    maxes = _softmax_init(acc_a_ref, acc_b_ref)
    ss = [f() for f in tile_scores(0)]
    for j in range(qb + 1):
        maxes, ss = _online_steps(ss, mix(values(j)), maxes, accs,
                                  tile_scores(j + 1) if j < qb else None)

    out_t = jnp.concatenate([_softmax_result(acc_a_ref, h) for h in range(H_A)], axis=0)
    oa_ref[...] = out_t.T.astype(BF16)

    lam = (jnp.exp(jnp.sum(lam_ref[0:1, :] * lam_ref[1:2, :], axis=-1, keepdims=True))
           - jnp.exp(jnp.sum(lam_ref[2:3, :] * lam_ref[3:4, :], axis=-1, keepdims=True))
           + LAMBDA_INIT)
    ys = []
    for h in range(H_B):
        o = _softmax_result(acc_b_ref, 2 * h) - lam * _softmax_result(acc_b_ref, 2 * h + 1)
        ys.append(o * lax.rsqrt(jnp.mean(o * o, axis=0, keepdims=True) + RMS_EPS))
    y = jnp.concatenate(ys, axis=0).T
    ob_ref[...] = (y * sg_ref[...] * (1.0 - LAMBDA_INIT)).astype(BF16)


def _attn_block(qb, qcat, kcat, wt, qa, ka, vat, kaug, qaug_a, lam4, subln_g, qbn, kbn, vbt,
                qaug_b):
    B, S = qa.shape[0], qa.shape[1]
    L = (qb + 1) * TQ

    def q_rows(w):
        return pl.BlockSpec((None, TQ, w), lambda b: (b, qb, 0))

    def key_rows(w):
        return pl.BlockSpec((None, L, w), lambda b: (b, 0, 0))

    def value_tiles(w):
        return pl.BlockSpec((None, qb + 1, w, KT), lambda b: (b, 0, 0, 0))

    def qaug_spec(h):
        return pl.BlockSpec((h, TQ, LANES), lambda b: (0, qb, 0))

    return pl.pallas_call(
        functools.partial(_attn_kernel, qb),
        grid=(B,),
        in_specs=[q_rows(H_I * IDX_K), key_rows(IDX_K),
                  pl.BlockSpec((None, None, WI_ROWS, TQ),
                               lambda b: (b, qb // (TM_PROJ // TQ), 0, qb % (TM_PROJ // TQ))),
                  q_rows(W_A), key_rows(W_A), value_tiles(W_A),
                  pl.BlockSpec((L, LANES), lambda b: (0, 0)), qaug_spec(H_A),
                  pl.BlockSpec((4, D_B), lambda b: (0, 0)),
                  pl.BlockSpec((1, W_B), lambda b: (0, 0)),
                  q_rows(W_B), key_rows(W_B), value_tiles(W_B), qaug_spec(H_B)],
        out_specs=[q_rows(W_A), q_rows(W_B)],
        out_shape=[jax.ShapeDtypeStruct(qa.shape, BF16), jax.ShapeDtypeStruct(qbn.shape, BF16)],
        input_output_aliases={3: 0, 10: 1},
        scratch_shapes=[pltpu.VMEM((S, TQ), F32), pltpu.VMEM((S, TQ), BF16),
                        pltpu.VMEM((S, TQ), I32), pltpu.VMEM((S, TQ), F32),
                        pltpu.VMEM((H_A, D_A + ONES_ROWS, TQ), F32),
                        pltpu.VMEM((2 * H_B, 2 * D_B + ONES_ROWS, TQ), F32)],
        compiler_params=pltpu.CompilerParams(dimension_semantics=("parallel",),
                                             vmem_limit_bytes=VMEM_LIMIT),
        name=f"attn_q{qb}",
    )(qcat, kcat, wt, qa, ka, vat, kaug, qaug_a,
      lam4, jnp.tile(subln_g, H_B).reshape(1, W_B), qbn, kbn, vbt, qaug_b)


def _ffn_kernel(x_ref, mod_ref, ya_ref, yb_ref, gates_ref, wua_ref, wub_ref, wo_ref, g2_ref,
                w1_ref, w3_ref, w2_ref, o_ref):
    gates = gates_ref[...].astype(F32)
    merged = (gates[:, :D_MODEL] * _dot(ya_ref[...], wua_ref[...])
              + gates[:, D_MODEL:] * _dot(yb_ref[...], wub_ref[...]))
    x1 = x_ref[...] + mod_ref[2:3, :] * _dot(merged.astype(BF16), wo_ref[...])
    ms = jnp.mean(x1 * x1, axis=-1, keepdims=True)
    h2 = x1 * lax.rsqrt(ms + RMS_EPS) * g2_ref[...]
    h2 = (h2 * (1.0 + mod_ref[4:5, :]) + mod_ref[3:4, :]).astype(BF16)
    u = _dot(h2, w1_ref[...])
    act = (u * jax.nn.sigmoid(u) * _dot(h2, w3_ref[...])).astype(BF16)
    o_ref[...] = x1 + mod_ref[5:6, :] * _dot(act, w2_ref[...])


def _merge_ffn(x2, mod3, ya, yb, gates, w_up_a, w_up_b, w_o, norm2_g, w_ff1, w_ff3, w_ff2, S):
    N = x2.shape[0]
    tm = TM_FFN
    per_batch = S // tm

    def row_spec(w):
        return pl.BlockSpec((tm, w), lambda t: (t, 0))

    return pl.pallas_call(
        _ffn_kernel,
        grid=(N // tm,),
        in_specs=[row_spec(D_MODEL),
                  pl.BlockSpec((None, 6, D_MODEL), lambda t: (t // per_batch, 0, 0)),
                  row_spec(W_A), row_spec(W_B), row_spec(W_G),
                  _const_spec((W_A, D_MODEL)), _const_spec((W_B, D_MODEL)),
                  _const_spec((D_MODEL, D_MODEL)), _const_spec((1, D_MODEL)),
                  _const_spec((D_MODEL, D_FF)), _const_spec((D_MODEL, D_FF)),
                  _const_spec((D_FF, D_MODEL))],
        out_specs=row_spec(D_MODEL),
        out_shape=jax.ShapeDtypeStruct((N, D_MODEL), F32),
        compiler_params=pltpu.CompilerParams(dimension_semantics=("parallel",),
                                             vmem_limit_bytes=VMEM_LIMIT),
        name="merge_ffn",
    )(x2, mod3, ya, yb, gates, w_up_a.astype(BF16), w_up_b.astype(BF16), w_o.astype(BF16),
      norm2_g.reshape(1, D_MODEL), w_ff1.astype(BF16), w_ff3.astype(BF16), w_ff2.astype(BF16))


def kernel(x, c, w_ada, b_ada, norm1_g, w_in, qn_a, kn_a, qn_b, kn_b, lam_q1, lam_k1, lam_q2,
           lam_k2, subln_g, w_up_a, w_up_b, w_o, norm2_g, w_ff1, w_ff3, w_ff2):
    B, S, D = x.shape
    assert D == D_MODEL and S % TQ == 0 and w_ada.shape[0] == 1
    N = B * S
    x2 = x.reshape(N, D)
    mod3 = _modulation(c, w_ada[0], b_ada[0]).reshape(B, 6, D)

    (qa, ka, qbn, kbn, vat, vbt, qcat, kcat, wi, gates) = _projection(
        x2, mod3, norm1_g[0], w_in[0], qn_a[0], kn_a[0], qn_b[0], kn_b[0], S)
    r3 = lambda a: a.reshape(B, S, a.shape[-1])
    qa, ka, qbn, kbn, qcat, kcat = map(r3, (qa, ka, qbn, kbn, qcat, kcat))
    vat, vbt = (a.reshape(B, S // KT, a.shape[1], KT) for a in (vat, vbt))
    wt = wi.reshape(B, S // TM_PROJ, WI_ROWS, TM_PROJ)
    lam4 = jnp.stack([lam_q1[0], lam_k1[0], lam_q2[0], lam_k2[0]])

    kaug, qaug_a = _alibi_tables(S, SLOPES_A)
    _, qaug_b = _alibi_tables(S, SLOPES_B)
    ya, yb = qa, qbn
    for qb in range(S // TQ):
        ya, yb = _attn_block(qb, qcat, kcat, wt, ya, ka, vat, kaug, qaug_a,
                             lam4, subln_g[0], yb, kbn, vbt, qaug_b)
    ya = ya.reshape(N, W_A)
    yb = yb.reshape(N, W_B)

    out = _merge_ffn(x2, mod3, ya, yb, gates, w_up_a[0], w_up_b[0], w_o[0], norm2_g[0],
                     w_ff1[0], w_ff3[0], w_ff2[0], S)
    return out.reshape(B, S, D)
```

```python
import functools

import numpy as np
import jax
import jax.numpy as jnp
from jax import lax
from jax.experimental import pallas as pl
from jax.experimental.pallas import tpu as pltpu

F32 = jnp.float32
BF16 = jnp.bfloat16
I32 = jnp.int32

D_MODEL = 1024
CHUNK = 64
H_A, D_A = 8, 64
H_I, D_I = 4, 64
TOPK_MAX = 256
H_B, D_B = 4, 64
D_FF = 2816
RMS_EPS = 1e-6
MASK_VALUE = -1e30
LAMBDA_INIT = 0.2
LOG2E = 1.4426950408889634
SLOPES_A = [2.0 ** (-8.0 * (h + 1) / H_A) for h in range(H_A)]
SLOPES_B = [2.0 ** (-8.0 * (h + 1) / H_B) for h in range(H_B)]

W_A = H_A * D_A
W_B = H_B * 2 * D_B
W_QKV = 3 * W_A + 3 * W_B
W_IDX = H_I * D_I + D_I + H_I
W_IDX_PAD = 384
W_G = 2 * D_MODEL

LANES = 128
IDX_K = 2 * LANES
BF16_EXACT_INT = 256
WI_ROWS = 8

TQ = 256
KT = TQ
TM_PROJ = 2 * KT
TM_FFN = 512
CNT_ACCS = 4
VMEM_LIMIT = 56 * 1024 * 1024

NT_DIMS = (((1,), (1,)), ((), ()))


def _dot(a, b):
    return jnp.dot(a, b, preferred_element_type=F32)


def _dot_nt(a, b):
    return lax.dot_general(a, b, NT_DIMS, preferred_element_type=F32)


def _split(x):
    hi = x.astype(BF16)
    lo = (x - hi.astype(F32)).astype(BF16)
    return hi, lo


def _dot3(a, b_hi, b_lo):
    a_hi, a_lo = _split(a)
    return _dot(a_hi, b_hi) + _dot(a_hi, b_lo) + _dot(a_lo, b_hi)


def _const_spec(shape):
    nd = len(shape)
    return pl.BlockSpec(shape, lambda *_: (0,) * nd, pipeline_mode=pl.Buffered(1))


def _mod_kernel(c_ref, w_ref, b_ref, o_ref):
    c = c_ref[...]
    cs = c * jax.nn.sigmoid(c)
    w_hi, w_lo = _split(w_ref[...])
    o_ref[...] = _dot3(cs, w_hi, w_lo) + b_ref[...]


def _modulation(c, w_ada, b_ada):
    B = c.shape[0]
    n = w_ada.shape[1]
    tn = 1024
    return pl.pallas_call(
        _mod_kernel,
        grid=(n // tn,),
        in_specs=[pl.BlockSpec((B, D_MODEL), lambda j: (0, 0)),
                  pl.BlockSpec((D_MODEL, tn), lambda j: (0, j)),
                  pl.BlockSpec((1, tn), lambda j: (0, j))],
        out_specs=pl.BlockSpec((B, tn), lambda j: (0, j)),
        out_shape=jax.ShapeDtypeStruct((B, n), F32),
        compiler_params=pltpu.CompilerParams(vmem_limit_bytes=VMEM_LIMIT),
        name="adaln_mod",
    )(c, w_ada, b_ada.reshape(1, n))


def _proj_kernel(x_ref, mod_ref, g1_ref, wqk_ref, wvt_ref, wih_ref, wil_ref, wg_ref, grp_ref,
                 gains_ref,
                 qa_ref, ka_ref, qb_ref, kb_ref, vat_ref, vbt_ref, qcat_ref, kcat_ref, wi_ref,
                 gates_ref):
    x = x_ref[...]
    ms = jnp.mean(x * x, axis=-1, keepdims=True)
    h = x * lax.rsqrt(ms + RMS_EPS) * g1_ref[...]
    h = h * (1.0 + mod_ref[1:2, :]) + mod_ref[0:1, :]
    h_hi, h_lo = _split(h)

    qk = _dot(h_hi, wqk_ref[...])
    vt = _dot_nt(wvt_ref[...], h_hi)
    pidx = (_dot(h_hi, wih_ref[...]) + _dot(h_hi, wil_ref[...]) + _dot(h_lo, wih_ref[...]))
    gate_logits = _dot(h_hi, wg_ref[...])

    grp = grp_ref[...]
    zs = [qk[:, i * W_A:(i + 1) * W_A] for i in range(4)]
    squares = [(z * z).astype(BF16) for z in zs]
    means = [_dot(sq, grp) * (1.0 / D_A) for sq in squares]
    for i, out_ref in enumerate((qa_ref, ka_ref, qb_ref, kb_ref)):
        out_ref[...] = (zs[i] * lax.rsqrt(means[i] + RMS_EPS) * gains_ref[i:i + 1, :]).astype(BF16)

    for s in range(x.shape[0] // KT):
        vat_ref[s] = vt[0:W_A, s * KT:(s + 1) * KT].astype(BF16)
        vbt_ref[s] = vt[W_A:, s * KT:(s + 1) * KT].astype(BF16)

    low_half = lax.broadcasted_iota(I32, (x.shape[0], LANES), 1) < D_I
    for g in range(H_I // 2):
        pair = pidx[:, g * LANES:(g + 1) * LANES]
        hi = pair.astype(BF16).astype(F32)
        lo = pair - hi
        hi_sw = pltpu.roll(hi, D_I, 1)
        lo_sw = pltpu.roll(lo, D_I, 1)
        base = 2 * g * IDX_K
        for part, val in enumerate((jnp.where(low_half, hi, lo_sw), jnp.where(low_half, hi, 0.0),
                                    jnp.where(low_half, hi_sw, lo), jnp.where(low_half, hi_sw, 0.0))):
            qcat_ref[:, base + part * LANES:base + (part + 1) * LANES] = val.astype(BF16)
    tail = pidx[:, 2 * LANES:3 * LANES]
    hi = tail.astype(BF16).astype(F32)
    lo = tail - hi
    kcat_ref[:, 0:LANES] = jnp.where(low_half, hi, pltpu.roll(hi, D_I, 1)).astype(BF16)
    kcat_ref[:, LANES:2 * LANES] = jnp.where(low_half, lo, 0.0).astype(BF16)
    wi_ref[...] = tail.T[D_I:D_I + WI_ROWS, :]

    gates_ref[...] = jax.nn.sigmoid(gate_logits).astype(BF16)


def _projection(x2, mod3, norm1_g, w_in, qn_a, kn_a, qn_b, kn_b, S):
    N = x2.shape[0]
    tm = TM_PROJ
    per_batch = S // tm
    w_qk = jnp.concatenate([w_in[:, 0:2 * W_A], w_in[:, 3 * W_A:3 * W_A + 2 * W_B]],
                           axis=1).astype(BF16)
    w_vt = jnp.concatenate([w_in[:, 2 * W_A:3 * W_A], w_in[:, 3 * W_A + 2 * W_B:W_QKV]],
                           axis=1).T.astype(BF16)
    w_idx = jnp.pad(w_in[:, W_QKV:W_QKV + W_IDX], ((0, 0), (0, W_IDX_PAD - W_IDX)))
    w_idx_hi = w_idx.astype(BF16)
    w_idx_lo = (w_idx - w_idx_hi.astype(F32)).astype(BF16)
    w_g = w_in[:, W_QKV + W_IDX:].astype(BF16)
    grp = jnp.asarray(np.kron(np.eye(W_A // D_A), np.ones((D_A, D_A))), BF16)
    gains = jnp.stack([jnp.tile(qn_a, H_A) * (LOG2E * D_A ** -0.5), jnp.tile(kn_a, H_A),
                       jnp.tile(qn_b, 2 * H_B) * (LOG2E * D_B ** -0.5), jnp.tile(kn_b, 2 * H_B)])

    def row_spec(w):
        return pl.BlockSpec((tm, w), lambda t: (t, 0))

    def rows(w, dt):
        return row_spec(w), jax.ShapeDtypeStruct((N, w), dt)

    def transposed(w):
        return (pl.BlockSpec((tm // KT, w, KT), lambda t: (t, 0, 0)),
                jax.ShapeDtypeStruct((N // KT, w, KT), BF16))

    outs = [rows(W_A, BF16), rows(W_A, BF16), rows(W_B, BF16), rows(W_B, BF16),
            transposed(W_A), transposed(W_B),
            rows(H_I * IDX_K, BF16), rows(IDX_K, BF16),
            (pl.BlockSpec((None, WI_ROWS, tm), lambda t: (t, 0, 0)),
             jax.ShapeDtypeStruct((N // tm, WI_ROWS, tm), F32)),
            rows(W_G, BF16)]
    return pl.pallas_call(
        _proj_kernel,
        grid=(N // tm,),
        in_specs=[row_spec(D_MODEL),
                  pl.BlockSpec((None, 6, D_MODEL), lambda t: (t // per_batch, 0, 0)),
                  _const_spec((1, D_MODEL)),
                  _const_spec((D_MODEL, 2 * W_A + 2 * W_B)),
                  _const_spec((W_A + W_B, D_MODEL)),
                  _const_spec((D_MODEL, W_IDX_PAD)),
                  _const_spec((D_MODEL, W_IDX_PAD)),
                  _const_spec((D_MODEL, W_G)),
                  _const_spec((W_A, W_A)),
                  _const_spec((4, W_A))],
        out_specs=[o[0] for o in outs],
        out_shape=[o[1] for o in outs],
        compiler_params=pltpu.CompilerParams(dimension_semantics=("parallel",),
                                             vmem_limit_bytes=VMEM_LIMIT),
        name="in_proj",
    )(x2, mod3, norm1_g.reshape(1, D_MODEL), w_qk, w_vt, w_idx_hi, w_idx_lo, w_g, grp, gains)


def _count_ge(ref, n_rows, cand):
    accs = [jnp.zeros((8, TQ), F32) for _ in range(CNT_ACCS)]
    for r in range(n_rows // 8):
        blk = ref[r * 8:(r + 1) * 8, :]
        accs[r % CNT_ACCS] = accs[r % CNT_ACCS] + jnp.where(blk >= cand, 1.0, 0.0)
    return jnp.sum(functools.reduce(lambda a, b: a + b, accs), axis=0, keepdims=True)


def _count_ge_bf16(ref, n_rows, cand):
    rows = 16
    one, zero = jnp.ones((), BF16), jnp.zeros((), BF16)

    assert n_rows // rows <= BF16_EXACT_INT
    accs = [jnp.zeros((rows, TQ), BF16) for _ in range(CNT_ACCS)]
    for r in range(n_rows // rows):
        blk = ref[r * rows:(r + 1) * rows, :]
        accs[r % CNT_ACCS] = accs[r % CNT_ACCS] + jnp.where(blk >= cand, one, zero)
    total = functools.reduce(lambda a, b: a + b, [a.astype(F32) for a in accs])
    return jnp.sum(total, axis=0, keepdims=True)


def _rank_to_f32(rank):
    key = rank ^ jnp.int32(-2 ** 31)
    return pltpu.bitcast(jnp.where(key < 0, key ^ jnp.int32(0x7FFFFFFF), key), F32)


def _store_bias(bias_ref, bias_t, L):
    bias_ref[0:L, :] = bias_t


def _select_bias(qb, qcat_ref, kcat_ref, wt_ref, score_ref, coarse_ref, eqi_ref, bias_ref):
    L = (qb + 1) * TQ
    admissible = ((lax.broadcasted_iota(I32, (KT, TQ), 0) // CHUNK)
                  <= (lax.broadcasted_iota(I32, (KT, TQ), 1) // CHUNK))
    if L <= TOPK_MAX:
        _store_bias(bias_ref, jnp.where(admissible, 0.0, MASK_VALUE), L)
        return

    for j in range(L // KT):
        rows = slice(j * KT, (j + 1) * KT)
        score = jnp.zeros((KT, TQ), F32)
        for h in range(H_I):
            logits = _dot_nt(kcat_ref[rows, :], qcat_ref[:, h * IDX_K:(h + 1) * IDX_K])
            score = score + wt_ref[h:h + 1, :] * jnp.maximum(logits, 0.0)
        if j == qb:
            score = jnp.where(admissible, score, -jnp.inf)
        score_ref[rows, :] = score
        coarse_ref[rows, :] = score.astype(BF16)
    k_sel = float(TOPK_MAX)

    def coarse_body(it, rank):
        cand = rank | lax.shift_left(jnp.int32(1), 31 - it)
        c = _count_ge_bf16(coarse_ref, L, _rank_to_f32(cand).astype(BF16))
        return jnp.where(c >= k_sel, cand, rank)
    base = lax.fori_loop(0, 16, coarse_body, jnp.zeros((1, TQ), I32)) - jnp.int32(1 << 16)

    def fine_body(it, st):
        off, cnt = st
        cand = off | lax.shift_left(jnp.int32(1), 16 - it)
        c = _count_ge(score_ref, L, _rank_to_f32(base + cand))
        take = c >= k_sel
        return jnp.where(take, cand, off), jnp.where(take, c, cnt)
    off, cnt = lax.fori_loop(
        0, 17, fine_body, (jnp.zeros((1, TQ), I32), jnp.full((1, TQ), float(L), F32)))
    rank = base + off
    thr = _rank_to_f32(rank)
    tied = jnp.max(cnt) > k_sel

    @pl.when(jnp.logical_not(tied))
    def _():
        _store_bias(bias_ref, jnp.where(score_ref[0:L, :] >= thr, 0.0, MASK_VALUE), L)

    @pl.when(tied)
    def _():
        above = _rank_to_f32(rank + 1)
        need = k_sel - _count_ge(score_ref, L, above)
        big = jnp.int32(1 << 30)
        sc = score_ref[0:L, :]
        s_idx = lax.broadcasted_iota(I32, (L, TQ), 0)
        eqi_ref[0:L, :] = jnp.where(sc >= above, big, jnp.where(sc >= thr, s_idx, big))
        n_bits = max(1, int(np.ceil(np.log2(L))))

        def idx_body(it, j):
            cand = j | lax.shift_left(jnp.int32(1), n_bits - 1 - it)
            cnt_lt = float(L) - _count_ge(eqi_ref, L, cand)
            return jnp.where(cnt_lt < need, cand, j)
        j_sel = lax.fori_loop(0, n_bits, idx_body, jnp.zeros((1, TQ), I32))
        sel = jnp.where(score_ref[0:L, :] >= above, 0.0,
                        jnp.where(eqi_ref[0:L, :] <= j_sel, 0.0, MASK_VALUE))
        _store_bias(bias_ref, sel, L)


def _tile_offsets():
    return (lax.broadcasted_iota(I32, (KT, TQ), 1)
            - lax.broadcasted_iota(I32, (KT, TQ), 0)).astype(F32)


def _masked_halves(q_ref, n_groups):
    lane = lax.broadcasted_iota(I32, (TQ, LANES), 1)
    low = jnp.where(lane < 64, 1.0, 0.0)
    high = 1.0 - low
    out = []
    for g in range(n_groups):
        q2 = q_ref[:, g * LANES:(g + 1) * LANES].astype(F32)
        out += [(q2 * low).astype(BF16), (q2 * high).astype(BF16)]
    return out


ONES_ROWS = 16


def _with_ones(vt):
    return jnp.concatenate([vt, jnp.ones((ONES_ROWS, vt.shape[1]), BF16)], axis=0)


def _online_steps(ss, values, maxes, accs, next_scores):
    out, nxt = [], []
    for c, (s, v, m, (acc_ref, i)) in enumerate(zip(ss, values, maxes, accs)):
        m_new = jnp.maximum(m, jnp.max(s, axis=0, keepdims=True))
        p = jnp.exp2((s - m_new).astype(BF16))
        acc_ref[i] = jnp.exp2(m - m_new) * acc_ref[i] + _dot(_with_ones(v()), p)
        out.append(m_new)
        if next_scores is not None:
            nxt.append(next_scores[c]())
    return tuple(out), nxt


def _softmax_init(*acc_refs):
    for acc_ref in acc_refs:
        acc_ref[...] = jnp.zeros(acc_ref.shape, F32)
    n = sum(r.shape[0] for r in acc_refs)
    return tuple(jnp.full((1, TQ), -jnp.inf, F32) for _ in range(n))


def _softmax_result(acc_ref, i):
    rows = acc_ref.shape[1] - ONES_ROWS
    return acc_ref[i, 0:rows, :] / acc_ref[i, rows:rows + 1, :]


def _split3(x):
    a = x.astype(BF16)
    r = x - a.astype(F32)
    b = r.astype(BF16)
    return a, b, (r - b.astype(F32)).astype(BF16)


def _alibi_tables(S, slopes):
    pos = jnp.arange(S, dtype=I32)
    hi = (pos // 64 * 64).astype(F32)
    lo = (pos % 64).astype(F32)
    lam = jnp.full((S,), LOG2E, F32)
    k_cols = [*_split3(lam), *_split3(lam), *_split3(lam * hi), *_split3(lam * lo)]
    kaug = jnp.pad(jnp.stack(k_cols, axis=1), ((0, 0), (0, LANES - len(k_cols))))
    qaug = []
    for sl in slopes:
        q_cols = [-sl * hi] * 3 + [-sl * lo] * 3 + [jnp.full((S,), sl, F32)] * 6
        qaug.append(jnp.pad(jnp.stack(q_cols, axis=1), ((0, 0), (0, LANES - len(q_cols)))))
    return kaug, jnp.stack(qaug).astype(BF16)


def _attn_kernel(qb, qcat_ref, kcat_ref, wt_ref, qa_ref, ka_ref, vat_ref, kaug_ref, qaug_a_ref,
                 lam_ref, sg_ref, qb_ref, kb_ref, vbt_ref, qaug_b_ref, oa_ref, ob_ref,
                 score_ref, coarse_ref, eqi_ref, bias_ref, acc_a_ref, acc_b_ref):
    n_b = 2 * H_B
    qms_a = _masked_halves(qa_ref, H_A // 2)
    qms_b = _masked_halves(qb_ref, H_B)
    qx_a = [jnp.concatenate([qms_a[h], qaug_a_ref[h]], axis=1) for h in range(H_A)]
    qx_b = [jnp.concatenate([qms_b[i], qaug_b_ref[i // 2]], axis=1) for i in range(n_b)]
    _select_bias(qb, qcat_ref, kcat_ref, wt_ref, score_ref, coarse_ref, eqi_ref, bias_ref)

    def mix(xs):
        return [xs[i // 2 + (i % 2) * H_A] for i in range(H_A + n_b)]
    accs = mix([(acc_a_ref, h) for h in range(H_A)] + [(acc_b_ref, i) for i in range(n_b)])

    def values(j):
        return ([lambda h=h: vat_ref[j, h * D_A:(h + 1) * D_A, :] for h in range(H_A)]
                + [lambda i=i: vbt_ref[j, (i // 2) * LANES:(i // 2 + 1) * LANES, :]
                   for i in range(n_b)])

    def tile_scores(j):
        rows = slice(j * KT, (j + 1) * KT)
        bias_t = bias_ref[rows, :]
        if j < qb:
            kaug_t = kaug_ref[rows, :]
            kx_a = [jnp.concatenate([ka_ref[rows, g * LANES:(g + 1) * LANES], kaug_t], axis=1)
                    for g in range(H_A // 2)]
            kx_b = [jnp.concatenate([kb_ref[rows, h * LANES:(h + 1) * LANES], kaug_t], axis=1)
                    for h in range(H_B)]
            return mix([lambda h=h: _dot_nt(kx_a[h // 2], qx_a[h]) + bias_t for h in range(H_A)]
                       + [lambda i=i: _dot_nt(kx_b[i // 2], qx_b[i]) for i in range(n_b)])
        dist = jnp.abs(_tile_offsets())
        allowed = ((lax.broadcasted_iota(I32, (KT, TQ), 0) // CHUNK)
                   <= (lax.broadcasted_iota(I32, (KT, TQ), 1) // CHUNK))
        biases_b = [jnp.where(allowed, -(LOG2E * sl) * dist, MASK_VALUE) for sl in SLOPES_B]
        return mix(
            [lambda h=h: (_dot_nt(ka_ref[rows, (h // 2) * LANES:(h // 2 + 1) * LANES], qms_a[h])
                          + (bias_t - (LOG2E * SLOPES_A[h]) * dist)) for h in range(H_A)]
            + [lambda i=i: (_dot_nt(kb_ref[rows, (i // 2) * LANES:(i // 2 + 1) * LANES], qms_b[i])
                            + biases_b[i // 2]) for i in range(n_b)])

    maxes = _softmax_init(acc_a_ref, acc_b_ref)
    ss = [f() for f in tile_scores(0)]
    for j in range(qb + 1):
        maxes, ss = _online_steps(ss, mix(values(j)), maxes, accs,
                                  tile_scores(j + 1) if j < qb else None)

    out_t = jnp.concatenate([_softmax_result(acc_a_ref, h) for h in range(H_A)], axis=0)
    oa_ref[...] = out_t.T.astype(BF16)

    lam = (jnp.exp(jnp.sum(lam_ref[0:1, :] * lam_ref[1:2, :], axis=-1, keepdims=True))
           - jnp.exp(jnp.sum(lam_ref[2:3, :] * lam_ref[3:4, :], axis=-1, keepdims=True))
           + LAMBDA_INIT)
    ys = []
    for h in range(H_B):
        o = _softmax_result(acc_b_ref, 2 * h) - lam * _softmax_result(acc_b_ref, 2 * h + 1)
        ys.append(o * lax.rsqrt(jnp.mean(o * o, axis=0, keepdims=True) + RMS_EPS))
    y = jnp.concatenate(ys, axis=0).T
    ob_ref[...] = (y * sg_ref[...] * (1.0 - LAMBDA_INIT)).astype(BF16)


def _attn_block(qb, qcat, kcat, wt, qa, ka, vat, kaug, qaug_a, lam4, subln_g, qbn, kbn, vbt,
                qaug_b):
    B, S = qa.shape[0], qa.shape[1]
    L = (qb + 1) * TQ

    def q_rows(w):
        return pl.BlockSpec((None, TQ, w), lambda b: (b, qb, 0))

    def key_rows(w):
        return pl.BlockSpec((None, L, w), lambda b: (b, 0, 0))

    def value_tiles(w):
        return pl.BlockSpec((None, qb + 1, w, KT), lambda b: (b, 0, 0, 0))

    def qaug_spec(h):
        return pl.BlockSpec((h, TQ, LANES), lambda b: (0, qb, 0))

    return pl.pallas_call(
        functools.partial(_attn_kernel, qb),
        grid=(B,),
        in_specs=[q_rows(H_I * IDX_K), key_rows(IDX_K),
                  pl.BlockSpec((None, None, WI_ROWS, TQ),
                               lambda b: (b, qb // (TM_PROJ // TQ), 0, qb % (TM_PROJ // TQ))),
                  q_rows(W_A), key_rows(W_A), value_tiles(W_A),
                  pl.BlockSpec((L, LANES), lambda b: (0, 0)), qaug_spec(H_A),
                  pl.BlockSpec((4, D_B), lambda b: (0, 0)),
                  pl.BlockSpec((1, W_B), lambda b: (0, 0)),
                  q_rows(W_B), key_rows(W_B), value_tiles(W_B), qaug_spec(H_B)],
        out_specs=[q_rows(W_A), q_rows(W_B)],
        out_shape=[jax.ShapeDtypeStruct(qa.shape, BF16), jax.ShapeDtypeStruct(qbn.shape, BF16)],
        input_output_aliases={3: 0, 10: 1},
        scratch_shapes=[pltpu.VMEM((S, TQ), F32), pltpu.VMEM((S, TQ), BF16),
                        pltpu.VMEM((S, TQ), I32), pltpu.VMEM((S, TQ), F32),
                        pltpu.VMEM((H_A, D_A + ONES_ROWS, TQ), F32),
                        pltpu.VMEM((2 * H_B, 2 * D_B + ONES_ROWS, TQ), F32)],
        compiler_params=pltpu.CompilerParams(dimension_semantics=("parallel",),
                                             vmem_limit_bytes=VMEM_LIMIT),
        name=f"attn_q{qb}",
    )(qcat, kcat, wt, qa, ka, vat, kaug, qaug_a,
      lam4, jnp.tile(subln_g, H_B).reshape(1, W_B), qbn, kbn, vbt, qaug_b)


def _ffn_kernel(x_ref, mod_ref, ya_ref, yb_ref, gates_ref, wua_ref, wub_ref, wo_ref, g2_ref,
                w1_ref, w3_ref, w2_ref, o_ref):
    gates = gates_ref[...].astype(F32)
    merged = (gates[:, :D_MODEL] * _dot(ya_ref[...], wua_ref[...])
              + gates[:, D_MODEL:] * _dot(yb_ref[...], wub_ref[...]))
    x1 = x_ref[...] + mod_ref[2:3, :] * _dot(merged.astype(BF16), wo_ref[...])
    ms = jnp.mean(x1 * x1, axis=-1, keepdims=True)
    h2 = x1 * lax.rsqrt(ms + RMS_EPS) * g2_ref[...]
    h2 = (h2 * (1.0 + mod_ref[4:5, :]) + mod_ref[3:4, :]).astype(BF16)
    u = _dot(h2, w1_ref[...])
    act = (u * jax.nn.sigmoid(u) * _dot(h2, w3_ref[...])).astype(BF16)
    o_ref[...] = x1 + mod_ref[5:6, :] * _dot(act, w2_ref[...])


def _merge_ffn(x2, mod3, ya, yb, gates, w_up_a, w_up_b, w_o, norm2_g, w_ff1, w_ff3, w_ff2, S):
    N = x2.shape[0]
    tm = TM_FFN
    per_batch = S // tm

    def row_spec(w):
        return pl.BlockSpec((tm, w), lambda t: (t, 0))

    return pl.pallas_call(
        _ffn_kernel,
        grid=(N // tm,),
        in_specs=[row_spec(D_MODEL),
                  pl.BlockSpec((None, 6, D_MODEL), lambda t: (t // per_batch, 0, 0)),
                  row_spec(W_A), row_spec(W_B), row_spec(W_G),
                  _const_spec((W_A, D_MODEL)), _const_spec((W_B, D_MODEL)),
                  _const_spec((D_MODEL, D_MODEL)), _const_spec((1, D_MODEL)),
                  _const_spec((D_MODEL, D_FF)), _const_spec((D_MODEL, D_FF)),
                  _const_spec((D_FF, D_MODEL))],
        out_specs=row_spec(D_MODEL),
        out_shape=jax.ShapeDtypeStruct((N, D_MODEL), F32),
        compiler_params=pltpu.CompilerParams(dimension_semantics=("parallel",),
                                             vmem_limit_bytes=VMEM_LIMIT),
        name="merge_ffn",
    )(x2, mod3, ya, yb, gates, w_up_a.astype(BF16), w_up_b.astype(BF16), w_o.astype(BF16),
      norm2_g.reshape(1, D_MODEL), w_ff1.astype(BF16), w_ff3.astype(BF16), w_ff2.astype(BF16))


def kernel(x, c, w_ada, b_ada, norm1_g, w_in, qn_a, kn_a, qn_b, kn_b, lam_q1, lam_k1, lam_q2,
           lam_k2, subln_g, w_up_a, w_up_b, w_o, norm2_g, w_ff1, w_ff3, w_ff2):
    B, S, D = x.shape
    assert D == D_MODEL and S % TQ == 0 and w_ada.shape[0] == 1
    N = B * S
    x2 = x.reshape(N, D)
    mod3 = _modulation(c, w_ada[0], b_ada[0]).reshape(B, 6, D)

    (qa, ka, qbn, kbn, vat, vbt, qcat, kcat, wi, gates) = _projection(
        x2, mod3, norm1_g[0], w_in[0], qn_a[0], kn_a[0], qn_b[0], kn_b[0], S)
    r3 = lambda a: a.reshape(B, S, a.shape[-1])
    qa, ka, qbn, kbn, qcat, kcat = map(r3, (qa, ka, qbn, kbn, qcat, kcat))
    vat, vbt = (a.reshape(B, S // KT, a.shape[1], KT) for a in (vat, vbt))
    wt = wi.reshape(B, S // TM_PROJ, WI_ROWS, TM_PROJ)
    lam4 = jnp.stack([lam_q1[0], lam_k1[0], lam_q2[0], lam_k2[0]])

    kaug, qaug_a = _alibi_tables(S, SLOPES_A)
    _, qaug_b = _alibi_tables(S, SLOPES_B)
    ya, yb = qa, qbn
    for qb in range(S // TQ):
        ya, yb = _attn_block(qb, qcat, kcat, wt, ya, ka, vat, kaug, qaug_a,
                             lam4, subln_g[0], yb, kbn, vbt, qaug_b)
    ya = ya.reshape(N, W_A)
    yb = yb.reshape(N, W_B)

    out = _merge_ffn(x2, mod3, ya, yb, gates, w_up_a[0], w_up_b[0], w_o[0], norm2_g[0],
                     w_ff1[0], w_ff3[0], w_ff2[0], S)
    return out.reshape(B, S, D)
```
